```python
import math
import jax
import jax.numpy as jnp
from jax import lax
import numpy as np

D_MODEL = 1024
BATCH = 16
SEQ = 2048
DEPTH = 4

GRID_W = 64
CTX_LEN = 256

N_MOD = 9
ADA_INIT = 0.5

FFN_DIM = 2816
MACARON_WEIGHT = 0.5

MLA_HEADS = 8
MLA_Q_RANK = 384
MLA_KV_RANK = 256
MLA_NOPE = 64
MLA_ROPE = 32
MLA_V = 64
MLA_QK_DIM = MLA_NOPE + MLA_ROPE
MLA_SCALE = MLA_QK_DIM ** -0.5

DIFF_HEADS = 4
DIFF_HEAD_DIM = 64
DIFF_WIDTH = DIFF_HEADS * 2 * DIFF_HEAD_DIM
DIFF_SCALE = DIFF_HEAD_DIM ** -0.5

ATTN_IN_DIM = MLA_Q_RANK + MLA_KV_RANK + MLA_ROPE + 3 * DIFF_WIDTH
ATTN_SPLITS = (MLA_Q_RANK,
               MLA_Q_RANK + MLA_KV_RANK,
               MLA_Q_RANK + MLA_KV_RANK + MLA_ROPE,
               MLA_Q_RANK + MLA_KV_RANK + MLA_ROPE + DIFF_WIDTH,
               MLA_Q_RANK + MLA_KV_RANK + MLA_ROPE + 2 * DIFF_WIDTH)
ATTN_MIX_DIM = MLA_HEADS * MLA_V + DIFF_WIDTH
Q_BLOCK = 128
ROPE_BASE = 10000.0

SSD_INNER = 2 * D_MODEL
SSD_HEAD_DIM = 64
SSD_HEADS = SSD_INNER // SSD_HEAD_DIM
SSD_GROUPS = 4
SSD_STATE = 128
SSD_CONV = 5
SSD_CHUNK = 128
SSD_CONV_DIM = SSD_INNER + 2 * SSD_GROUPS * SSD_STATE
SSD_IN_DIM = SSD_INNER + SSD_CONV_DIM + 2 * SSD_HEADS

N_ATTN_LAYERS = (DEPTH + 1) // 2
N_SSD_LAYERS = DEPTH // 2
DEEPNORM_ALPHA = (2.0 * DEPTH) ** 0.25
DEEPNORM_BETA = (8.0 * DEPTH) ** -0.25
LN_EPS = 1e-6
RMS_EPS = 1e-6

kernel_name = 'hybrid_mla_diffattn_ssd_dit_trunk'


def layer_norm(t, g, b):
    tf = t.astype(jnp.float32)
    mu = jnp.mean(tf, axis=-1, keepdims=True)
    var = jnp.mean(jnp.square(tf - mu), axis=-1, keepdims=True)
    return ((tf - mu) * lax.rsqrt(var + LN_EPS) * g + b).astype(t.dtype)


def rms_norm(t, g):
    tf = t.astype(jnp.float32)
    return (tf * lax.rsqrt(jnp.mean(tf * tf, axis=-1, keepdims=True) + RMS_EPS) * g).astype(t.dtype)


def modulate(t, shift, scale):
    return t * (1 + scale) + shift


def deepnorm_update(t, r, g, b):
    return layer_norm(DEEPNORM_ALPHA * t + r, g, b)


def swiglu(t, w_gu, w_down):
    g, u = jnp.split(t @ w_gu, 2, axis=-1)
    return (jax.nn.silu(g) * u) @ w_down


def lambda_init_for(layer):
    return 0.8 - 0.6 * math.exp(-0.3 * layer)


def axial_rope(n_tok, rot_dim):
    rows = n_tok // GRID_W
    row = jnp.repeat(jnp.arange(rows, dtype=jnp.float32), GRID_W)
    col = jnp.tile(jnp.arange(GRID_W, dtype=jnp.float32), rows)
    n_freq = rot_dim // 4
    inv = ROPE_BASE ** (-jnp.arange(n_freq, dtype=jnp.float32) / n_freq)
    ang = jnp.concatenate([row[:, None] * inv, col[:, None] * inv], axis=-1)
    return jnp.cos(ang), jnp.sin(ang)


def apply_rope(t, cos, sin):
    t1, t2 = jnp.split(t, 2, axis=-1)
    return jnp.concatenate([t1 * cos - t2 * sin, t1 * sin + t2 * cos], axis=-1).astype(t.dtype)


def block_attention(q, k, v, scale):
    bsz, nq, h, dk = q.shape
    blk = min(Q_BLOCK, nq)
    qb = jnp.moveaxis(q.reshape(bsz, nq // blk, blk, h, dk), 1, 0)

    def one_block(qi):
        s = jnp.einsum('bqhd,bkhd->bhqk', qi, k).astype(jnp.float32) * scale
        p = jax.nn.softmax(s, axis=-1).astype(v.dtype)
        return jnp.einsum('bhqk,bkhd->bqhd', p, v)

    out = lax.map(one_block, qb)
    return jnp.moveaxis(out, 0, 1).reshape(bsz, nq, h, v.shape[-1])


def attn_project(u, w_in, q_norm_g, w_uq, kv_norm_g, w_ukv, rope_mla, rope_diff):
    bsz, n, _ = u.shape
    cq, ckv, kr, dq, dk, dv = jnp.split(u @ w_in, ATTN_SPLITS, axis=-1)
    q = (rms_norm(cq, q_norm_g) @ w_uq).reshape(bsz, n, MLA_HEADS, MLA_QK_DIM)
    kv = (rms_norm(ckv, kv_norm_g) @ w_ukv).reshape(bsz, n, MLA_HEADS, MLA_NOPE + MLA_V)
    q_nope, q_rope = jnp.split(q, [MLA_NOPE], axis=-1)
    k_nope, v_a = jnp.split(kv, [MLA_NOPE], axis=-1)
    dq = dq.reshape(bsz, n, DIFF_HEADS, 2, DIFF_HEAD_DIM)
    dk = dk.reshape(bsz, n, DIFF_HEADS, 2, DIFF_HEAD_DIM)
    dv = dv.reshape(bsz, n, DIFF_HEADS, 2 * DIFF_HEAD_DIM)
    if rope_mla is not None:
        cos, sin = rope_mla
        q_rope = apply_rope(q_rope, cos[None, :, None], sin[None, :, None])
        kr = apply_rope(kr, cos[None], sin[None])
        cos, sin = rope_diff
        dq = apply_rope(dq, cos[None, :, None, None], sin[None, :, None, None])
        dk = apply_rope(dk, cos[None, :, None, None], sin[None, :, None, None])
    q_a = jnp.concatenate([q_nope, q_rope], axis=-1)
    k_a = jnp.concatenate([k_nope, jnp.broadcast_to(kr[:, :, None, :], (bsz, n, MLA_HEADS, MLA_ROPE))], axis=-1)
    return q_a, k_a, v_a, dq, dk, dv


def attn_heads_out(q_a, k_a, v_a, q_d, k_d, v_d, lam, lam_init, subln_g, w_out):
    bsz, n = q_a.shape[:2]
    o_a = block_attention(q_a, k_a, v_a, MLA_SCALE)
    o_1 = block_attention(q_d[..., 0, :], k_d[..., 0, :], v_d, DIFF_SCALE)
    o_2 = block_attention(q_d[..., 1, :], k_d[..., 1, :], v_d, DIFF_SCALE)
    o_d = rms_norm(o_1 - lam.astype(o_1.dtype) * o_2, subln_g) * (1.0 - lam_init)
    o = jnp.concatenate([o_a.reshape(bsz, n, -1), o_d.reshape(bsz, n, -1)], axis=-1)
    return o @ w_out


def attn_mixer(u_x, u_c, w_in, q_norm_g, w_uq, kv_norm_g, w_ukv, lam_q1, lam_k1, lam_q2, lam_k2,
               subln_g, w_out, lam_init, need_ctx_out):
    n_lat = u_x.shape[1]
    rope_mla = axial_rope(n_lat, MLA_ROPE)
    rope_diff = axial_rope(n_lat, DIFF_HEAD_DIM)
    qa_x, ka_x, va_x, qd_x, kd_x, vd_x = attn_project(u_x, w_in, q_norm_g, w_uq, kv_norm_g, w_ukv, rope_mla, rope_diff)
    qa_c, ka_c, va_c, qd_c, kd_c, vd_c = attn_project(u_c, w_in, q_norm_g, w_uq, kv_norm_g, w_ukv, None, None)
    lam = (jnp.exp(jnp.sum(lam_q1.astype(jnp.float32) * lam_k1)) -
           jnp.exp(jnp.sum(lam_q2.astype(jnp.float32) * lam_k2)) + lam_init)
    y_x = attn_heads_out(qa_x, jnp.concatenate([ka_c, ka_x], axis=1), jnp.concatenate([va_c, va_x], axis=1),
                         qd_x, jnp.concatenate([kd_c, kd_x], axis=1), jnp.concatenate([vd_c, vd_x], axis=1),
                         lam, lam_init, subln_g, w_out)
    y_c = attn_heads_out(qa_c, ka_c, va_c, qd_c, kd_c, vd_c, lam, lam_init, subln_g, w_out) if need_ctx_out else None
    return y_x, y_c


def depthwise_conv(t, w, b):
    k = w.shape[0]
    out = lax.conv_general_dilated(t, w[:, None, :].astype(t.dtype), window_strides=(1,),
                                   padding=[(k // 2, k // 2)], dimension_numbers=('NWC', 'WIO', 'NWC'),
                                   feature_group_count=t.shape[-1])
    return out + b


def ssd_project(u, w_in, conv_w, conv_b, dt_bias):
    bsz, n, _ = u.shape
    z, xbc, dt = jnp.split(u @ w_in, [SSD_INNER, SSD_INNER + SSD_CONV_DIM], axis=-1)
    xbc = jax.nn.silu(depthwise_conv(xbc, conv_w, conv_b))
    xs, bm, cm = jnp.split(xbc, [SSD_INNER, SSD_INNER + SSD_GROUPS * SSD_STATE], axis=-1)
    dt = jax.nn.softplus(dt.astype(jnp.float32).reshape(bsz, n, 2, SSD_HEADS) + dt_bias)
    return (z, xs.reshape(bsz, n, SSD_HEADS, SSD_HEAD_DIM), bm.reshape(bsz, n, SSD_GROUPS, SSD_STATE),
            cm.reshape(bsz, n, SSD_GROUPS, SSD_STATE), dt)


def ssd_scan(x, dt, a, bm, cm, h0, with_output):
    bsz, seq_len, nh, hp = x.shape
    g, ns = bm.shape[-2:]
    e = nh // g
    t = min(SSD_CHUNK, seq_len)
    nc = seq_len // t
    xdt = (x.astype(jnp.float32) * dt[..., None]).reshape(bsz, nc, t, g, e, hp)
    da = (dt * a).reshape(bsz, nc, t, g, e)
    bmc = bm.astype(jnp.float32).reshape(bsz, nc, t, g, ns)
    cmc = cm.astype(jnp.float32).reshape(bsz, nc, t, g, ns)
    cum = jnp.cumsum(da, axis=2)
    decay_to_end = jnp.exp(cum[:, :, -1:] - cum)
    states = jnp.einsum('bctgn,bctge,bctgep->bcgepn', bmc, decay_to_end, xdt)
    chunk_decay = jnp.exp(cum[:, :, -1])

    def step(h, inp):
        s, dec = inp
        return h * dec[..., None, None] + s, h

    final, entry = lax.scan(step, h0.reshape(bsz, g, e, hp, ns),
                            (jnp.moveaxis(states, 1, 0), jnp.moveaxis(chunk_decay, 1, 0)))
    final = final.reshape(bsz, nh, hp, ns)
    if not with_output:
        return None, final
    entry = jnp.moveaxis(entry, 0, 1)
    seg = cum[:, :, :, None] - cum[:, :, None, :]
    mask = jnp.tril(jnp.ones((t, t), dtype=bool))[:, :, None, None]
    lmat = jnp.exp(jnp.where(mask, seg, -jnp.inf))
    cb = jnp.einsum('bctgn,bcsgn->bctsg', cmc, bmc)
    y_diag = jnp.einsum('bctsg,bctsge,bcsgep->bctgep', cb, lmat, xdt)
    y_off = jnp.einsum('bctgn,bcgepn,bctge->bctgep', cmc, entry, jnp.exp(cum))
    return (y_diag + y_off).reshape(bsz, seq_len, nh, hp), final


def seq_dir(t, backward):
    return jnp.flip(t, axis=1) if backward else t


def ssd_gated_out(y, z, norm_g, w_out):
    bsz, n = z.shape[:2]
    gy = y.reshape(bsz, n, SSD_INNER) * jax.nn.silu(z.astype(jnp.float32))
    gy = gy.reshape(bsz, n, SSD_GROUPS, SSD_INNER // SSD_GROUPS)
    gy = gy * lax.rsqrt(jnp.mean(gy * gy, axis=-1, keepdims=True) + RMS_EPS)
    gy = gy.reshape(bsz, n, SSD_INNER) * norm_g
    return gy.astype(z.dtype) @ w_out


def ssd_mixer(u_x, u_c, w_in, conv_w, conv_b, a_log, dt_bias, d_skip, norm_g, w_out, need_ctx_out):
    z_x, x_x, b_x, c_x, dt_x = ssd_project(u_x, w_in, conv_w, conv_b, dt_bias)
    z_c, x_c, b_c, c_c, dt_c = ssd_project(u_c, w_in, conv_w, conv_b, dt_bias)
    a = -jnp.exp(a_log.astype(jnp.float32))
    h0 = jnp.zeros((u_x.shape[0], SSD_HEADS, SSD_HEAD_DIM, SSD_STATE), jnp.float32)
    ys_x, ys_c = [], []
    for di, backward in enumerate((False, True)):
        y_cd, h_ctx = ssd_scan(seq_dir(x_c, backward), seq_dir(dt_c[:, :, di], backward), a[di],
                               seq_dir(b_c, backward), seq_dir(c_c, backward), h0, need_ctx_out)
        y_xd, _ = ssd_scan(seq_dir(x_x, backward), seq_dir(dt_x[:, :, di], backward), a[di],
                           seq_dir(b_x, backward), seq_dir(c_x, backward), h_ctx, True)
        ys_x.append(seq_dir(y_xd, backward) + d_skip[di][:, None] * x_x.astype(jnp.float32))
        if need_ctx_out:
            ys_c.append(seq_dir(y_cd, backward) + d_skip[di][:, None] * x_c.astype(jnp.float32))
    y_x = ssd_gated_out(ys_x[0] + ys_x[1], z_x, norm_g, w_out)
    y_c = ssd_gated_out(ys_c[0] + ys_c[1], z_c, norm_g, w_out) if need_ctx_out else None
    return y_x, y_c


def setup_inputs(seed: int = 0) -> dict:
    key = jax.random.key(seed)
    keys = iter(jax.random.split(key, 40))

    def normal(shape, scale):
        return jax.random.normal(next(keys), shape, jnp.float32) * scale

    d = D_MODEL
    na, ns = N_ATTN_LAYERS, N_SSD_LAYERS
    beta = DEEPNORM_BETA
    inputs = {}
    inputs['x'] = normal((BATCH, SEQ, d), 1.0)
    inputs['c'] = normal((BATCH, d), 1.0)
    inputs['ctx'] = normal((BATCH, CTX_LEN, d), 1.0)
    inputs['c_ctx'] = normal((d,), 1.0)
    inputs['ada_w'] = normal((DEPTH, d, N_MOD * d), ADA_INIT * d ** -0.5)
    inputs['ada_b'] = normal((DEPTH, N_MOD * d), 0.02)
    inputs['ln_g'] = 1.0 + normal((DEPTH, 3, d), 0.02)
    inputs['ln_b'] = normal((DEPTH, 3, d), 0.02)
    inputs['ffn1_w_gu'] = normal((DEPTH, d, 2 * FFN_DIM), d ** -0.5)
    inputs['ffn1_w_down'] = normal((DEPTH, FFN_DIM, d), beta * FFN_DIM ** -0.5)
    inputs['ffn2_w_gu'] = normal((DEPTH, d, 2 * FFN_DIM), d ** -0.5)
    inputs['ffn2_w_down'] = normal((DEPTH, FFN_DIM, d), beta * FFN_DIM ** -0.5)
    inputs['attn_w_in'] = normal((na, d, ATTN_IN_DIM), d ** -0.5)
    inputs['mla_q_norm_g'] = 1.0 + normal((na, MLA_Q_RANK), 0.02)
    inputs['mla_w_uq'] = normal((na, MLA_Q_RANK, MLA_HEADS * MLA_QK_DIM), MLA_Q_RANK ** -0.5)
    inputs['mla_kv_norm_g'] = 1.0 + normal((na, MLA_KV_RANK), 0.02)
    inputs['mla_w_ukv'] = normal((na, MLA_KV_RANK, MLA_HEADS * (MLA_NOPE + MLA_V)), MLA_KV_RANK ** -0.5)
    inputs['diff_lam_q1'] = normal((na, DIFF_HEAD_DIM), 0.1)
    inputs['diff_lam_k1'] = normal((na, DIFF_HEAD_DIM), 0.1)
    inputs['diff_lam_q2'] = normal((na, DIFF_HEAD_DIM), 0.1)
    inputs['diff_lam_k2'] = normal((na, DIFF_HEAD_DIM), 0.1)
    inputs['diff_subln_g'] = 1.0 + normal((na, 2 * DIFF_HEAD_DIM), 0.02)
    inputs['attn_w_out'] = normal((na, ATTN_MIX_DIM, d), beta * ATTN_MIX_DIM ** -0.5)
    inputs['ssd_w_in'] = normal((ns, d, SSD_IN_DIM), d ** -0.5)
    inputs['ssd_conv_w'] = normal((ns, SSD_CONV, SSD_CONV_DIM), SSD_CONV ** -0.5)
    inputs['ssd_conv_b'] = normal((ns, SSD_CONV_DIM), 0.02)
    inputs['ssd_a_log'] = jnp.log(jax.random.uniform(next(keys), (ns, 2, SSD_HEADS), jnp.float32, 1.0, 16.0))
    dt0 = jnp.exp(jax.random.uniform(next(keys), (ns, 2, SSD_HEADS), jnp.float32, math.log(1e-3), math.log(1e-1)))
    inputs['ssd_dt_bias'] = dt0 + jnp.log(-jnp.expm1(-dt0))
    inputs['ssd_d'] = 1.0 + normal((ns, 2, SSD_HEADS), 0.02)
    inputs['ssd_norm_g'] = 1.0 + normal((ns, SSD_INNER), 0.02)
    inputs['ssd_w_out'] = normal((ns, SSD_INNER, d), beta * SSD_INNER ** -0.5)
    return inputs


def reference(x, c, ctx, c_ctx, ada_w, ada_b, ln_g, ln_b, ffn1_w_gu, ffn1_w_down, ffn2_w_gu, ffn2_w_down,
              attn_w_in, mla_q_norm_g, mla_w_uq, mla_kv_norm_g, mla_w_ukv, diff_lam_q1, diff_lam_k1,
              diff_lam_q2, diff_lam_k2, diff_subln_g, attn_w_out, ssd_w_in, ssd_conv_w, ssd_conv_b,
              ssd_a_log, ssd_dt_bias, ssd_d, ssd_norm_g, ssd_w_out):
    h_x, h_c = x, ctx
    for l in range(DEPTH):
        last = l == DEPTH - 1
        m_x = jnp.split((jax.nn.silu(c) @ ada_w[l] + ada_b[l])[:, None, :], N_MOD, axis=-1)
        m_c = jnp.split(jax.nn.silu(c_ctx) @ ada_w[l] + ada_b[l], N_MOD, axis=-1)

        h_x = deepnorm_update(h_x, MACARON_WEIGHT * m_x[2] * swiglu(modulate(h_x, m_x[0], m_x[1]), ffn1_w_gu[l], ffn1_w_down[l]),
                              ln_g[l, 0], ln_b[l, 0])
        h_c = deepnorm_update(h_c, MACARON_WEIGHT * m_c[2] * swiglu(modulate(h_c, m_c[0], m_c[1]), ffn1_w_gu[l], ffn1_w_down[l]),
                              ln_g[l, 0], ln_b[l, 0])

        u_x = modulate(h_x, m_x[3], m_x[4])
        u_c = modulate(h_c, m_c[3], m_c[4])
        if l % 2 == 0:
            a = l // 2
            y_x, y_c = attn_mixer(u_x, u_c, attn_w_in[a], mla_q_norm_g[a], mla_w_uq[a], mla_kv_norm_g[a], mla_w_ukv[a],
                                  diff_lam_q1[a], diff_lam_k1[a], diff_lam_q2[a], diff_lam_k2[a], diff_subln_g[a],
                                  attn_w_out[a], lambda_init_for(l), not last)
        else:
            s = l // 2
            y_x, y_c = ssd_mixer(u_x, u_c, ssd_w_in[s], ssd_conv_w[s], ssd_conv_b[s], ssd_a_log[s], ssd_dt_bias[s],
                                 ssd_d[s], ssd_norm_g[s], ssd_w_out[s], not last)
        h_x = deepnorm_update(h_x, m_x[5] * y_x, ln_g[l, 1], ln_b[l, 1])

        h_x = deepnorm_update(h_x, MACARON_WEIGHT * m_x[8] * swiglu(modulate(h_x, m_x[6], m_x[7]), ffn2_w_gu[l], ffn2_w_down[l]),
                              ln_g[l, 2], ln_b[l, 2])
        if not last:
            h_c = deepnorm_update(h_c, m_c[5] * y_c, ln_g[l, 1], ln_b[l, 1])
            h_c = deepnorm_update(h_c, MACARON_WEIGHT * m_c[8] * swiglu(modulate(h_c, m_c[6], m_c[7]), ffn2_w_gu[l], ffn2_w_down[l]),
                                  ln_g[l, 2], ln_b[l, 2])
    return h_x
```

```python
import functools
import math

import jax
import jax.numpy as jnp
from jax import lax
from jax.experimental import pallas as pl
from jax.experimental.pallas import tpu as pltpu

F32 = jnp.float32
BF16 = jnp.bfloat16

D_MODEL = 1024
DEPTH = 4
GRID_W = 64
N_MOD = 9
FFN_DIM = 2816
MACARON_WEIGHT = 0.5
MLA_HEADS = 8
MLA_Q_RANK = 384
MLA_KV_RANK = 256
MLA_NOPE = 64
MLA_ROPE = 32
MLA_V = 64
MLA_QK_DIM = MLA_NOPE + MLA_ROPE
MLA_SCALE = MLA_QK_DIM ** -0.5
DIFF_HEADS = 4
DIFF_HEAD_DIM = 64
DIFF_WIDTH = DIFF_HEADS * 2 * DIFF_HEAD_DIM
DIFF_SCALE = DIFF_HEAD_DIM ** -0.5
ROPE_BASE = 10000.0
SSD_INNER = 2 * D_MODEL
SSD_HEAD_DIM = 64
SSD_HEADS = SSD_INNER // SSD_HEAD_DIM
SSD_GROUPS = 4
SSD_STATE = 128
SSD_CONV = 5
SSD_CHUNK = 128
SSD_CONV_DIM = SSD_INNER + 2 * SSD_GROUPS * SSD_STATE
DEEPNORM_ALPHA = (2.0 * DEPTH) ** 0.25
LN_EPS = 1e-6
RMS_EPS = 1e-6

LANES = 128
V7X_VMEM_LIMIT = 56 * 1024 * 1024
HALO = 16

HEAD_BLOCK = LANES
MOD_ROWS = 24


def _cparams(sem):
    return pltpu.CompilerParams(dimension_semantics=sem, vmem_limit_bytes=V7X_VMEM_LIMIT)


def _sigmoid(v):
    return 1.0 / (1.0 + jnp.exp(-v))


def _layer_norm_rows(v, g, b):
    mu = jnp.mean(v, axis=-1, keepdims=True)
    c = v - mu
    var = jnp.mean(c * c, axis=-1, keepdims=True)
    return c * lax.rsqrt(var + LN_EPS) * g + b


def _rms_rows(v, g):
    return v * lax.rsqrt(jnp.mean(v * v, axis=-1, keepdims=True) + RMS_EPS) * g


def _bdot(a, b):
    return jnp.dot(a, b, preferred_element_type=F32)


def _bdot_nt(a, b):
    return lax.dot_general(a, b, (((1,), (1,)), ((), ())), preferred_element_type=F32)


def _ada_kernel(c_ref, w_ref, b_ref, o_ref):
    c = c_ref[...]
    s = (c * _sigmoid(c)).astype(BF16)
    o_ref[...] = _bdot(s, w_ref[...].astype(BF16)) + b_ref[...]


def _ada_call(cond, ada_w, ada_b):
    depth, d, n = ada_w.shape
    tn = n // 8
    return pl.pallas_call(
        _ada_kernel,
        grid=(depth, n // tn),
        in_specs=[
            pl.BlockSpec((MOD_ROWS, d), lambda l, j: (0, 0)),
            pl.BlockSpec((None, d, tn), lambda l, j: (l, 0, j)),
            pl.BlockSpec((None, 1, tn), lambda l, j: (l, 0, j)),
        ],
        out_specs=pl.BlockSpec((None, MOD_ROWS, tn), lambda l, j: (l, 0, j)),
        out_shape=jax.ShapeDtypeStruct((depth, MOD_ROWS, n), F32),
        compiler_params=_cparams(("parallel", "parallel")),
        name="ada_mod",
    )(cond, ada_w, ada_b.reshape(depth, 1, n))


class _Geom:
    def __init__(self, batch, seq, ctx):
        self.batch, self.seq, self.ctx = batch, seq, ctx
        self.n_lat = batch * seq
        self.n_ctx = batch * ctx
        self.n_tok = self.n_lat + self.n_ctx

    def mod_spec(self, layer, tm):
        n_lat_tiles = self.n_lat // tm
        tiles_per_batch = self.seq // tm
        batch = self.batch

        def index(i):
            return (layer, jnp.where(i < n_lat_tiles, i // tiles_per_batch, batch), 0, 0)

        return pl.BlockSpec((None, None, N_MOD, D_MODEL), index)


def _row_spec(tm, width):
    return pl.BlockSpec((tm, width), lambda i: (i, 0))


def _full_spec(shape):
    nd = len(shape)
    return pl.BlockSpec(shape, lambda i: (0,) * nd)


def _ffn_kernel(h_ref, mod_ref, wgu_ref, wd_ref, g_ref, b_ref, o_ref, acc_ref, *, k0, fc):
    h = h_ref[...]
    shift = mod_ref[k0:k0 + 1, :]
    scale = mod_ref[k0 + 1:k0 + 2, :]
    gate = mod_ref[k0 + 2:k0 + 3, :]
    t = (h * (1.0 + scale) + shift).astype(BF16)
    for j in range(FFN_DIM // fc):
        gj = _bdot(t, wgu_ref[:, j * fc:(j + 1) * fc])
        uj = _bdot(t, wgu_ref[:, FFN_DIM + j * fc:FFN_DIM + (j + 1) * fc])
        a = (gj * _sigmoid(gj) * uj).astype(BF16)
        y = _bdot(a, wd_ref[j * fc:(j + 1) * fc, :])
        if j == 0:
            acc_ref[...] = y
        else:
            acc_ref[...] += y
    r = (MACARON_WEIGHT * gate) * acc_ref[...]
    o_ref[...] = _layer_norm_rows(DEEPNORM_ALPHA * h + r, g_ref[...], b_ref[...])


def _ffn_call(geom, h, mods, layer, k0, w_gu, w_down, ln_g, ln_b, n_rows, tm, fc):
    kern = functools.partial(_ffn_kernel, k0=k0, fc=fc)
    return pl.pallas_call(
        kern,
        grid=(n_rows // tm,),
        in_specs=[
            _row_spec(tm, D_MODEL),
            geom.mod_spec(layer, tm),
            _full_spec(w_gu.shape),
            _full_spec(w_down.shape),
            _full_spec((1, D_MODEL)),
            _full_spec((1, D_MODEL)),
        ],
        out_specs=_row_spec(tm, D_MODEL),
        out_shape=jax.ShapeDtypeStruct((n_rows, D_MODEL), F32),
        scratch_shapes=[pltpu.VMEM((tm, D_MODEL), F32)],
        compiler_params=_cparams(("parallel",)),
        name="ffn",
    )(h, mods, w_gu, w_down, ln_g.reshape(1, -1), ln_b.reshape(1, -1))


ATTN_PROJ_COLS = MLA_Q_RANK + MLA_KV_RANK + HEAD_BLOCK + 3 * DIFF_WIDTH
_C_CKV = MLA_Q_RANK
_C_KR = MLA_Q_RANK + MLA_KV_RANK
_C_DQ = _C_KR + HEAD_BLOCK
_C_DK = _C_DQ + DIFF_WIDTH
_C_DV = _C_DK + DIFF_WIDTH
_QA_W = MLA_HEADS * HEAD_BLOCK
_VA_W = MLA_HEADS * MLA_V


def _rope_block(v, tab_ref, t0, shift):
    left = pltpu.roll(v, LANES - shift, axis=1)
    right = pltpu.roll(v, shift, axis=1)
    return v * tab_ref[t0] + left * tab_ref[t0 + 1] + right * tab_ref[t0 + 2]


def _attn_in_kernel(h_ref, mod_ref, tab_ref, win_ref, qg_ref, wuq_ref, kvg_ref, wukv_ref, place_ref,
                    qa_ref, ka_ref, va_ref, dq_ref, dk_ref, dv_ref):
    h = h_ref[...]
    u = (h * (1.0 + mod_ref[4:5, :]) + mod_ref[3:4, :]).astype(BF16)
    proj = _bdot(u, win_ref[...])
    cqn = _rms_rows(proj[:, :_C_CKV], qg_ref[...]).astype(BF16)
    q = _bdot(cqn, wuq_ref[...])
    for hd in range(MLA_HEADS):
        sl = slice(hd * HEAD_BLOCK, (hd + 1) * HEAD_BLOCK)
        qa_ref[:, sl] = _rope_block(q[:, sl], tab_ref, 0, MLA_ROPE // 2).astype(BF16)
    ckvn = _rms_rows(proj[:, _C_CKV:_C_KR], kvg_ref[...]).astype(BF16)
    kv = _bdot(ckvn, wukv_ref[...])
    kr = _rope_block(proj[:, _C_KR:_C_DQ], tab_ref, 3, MLA_ROPE // 2).astype(BF16)
    ka_ref[...] = (kv[:, :_QA_W] + _bdot(kr, place_ref[...])).astype(BF16)
    va_ref[...] = kv[:, _QA_W:].astype(BF16)
    for hd in range(DIFF_HEADS):
        sq = slice(_C_DQ + hd * HEAD_BLOCK, _C_DQ + (hd + 1) * HEAD_BLOCK)
        sk = slice(_C_DK + hd * HEAD_BLOCK, _C_DK + (hd + 1) * HEAD_BLOCK)
        so = slice(hd * HEAD_BLOCK, (hd + 1) * HEAD_BLOCK)
        dq_ref[:, so] = _rope_block(proj[:, sq], tab_ref, 6, DIFF_HEAD_DIM // 2).astype(BF16)
        dk_ref[:, so] = _rope_block(proj[:, sk], tab_ref, 9, DIFF_HEAD_DIM // 2).astype(BF16)
    dv_ref[...] = proj[:, _C_DV:].astype(BF16)


def _attn_in_call(geom, h, mods, layer, tables, w_in, q_g, w_uq, kv_g, w_ukv, place, tm):
    n_lat_tiles = geom.n_lat // tm
    tiles_per_seq = geom.seq // tm

    def tab_index(i):
        return (0, jnp.where(i < n_lat_tiles, i % tiles_per_seq, tiles_per_seq), 0)

    n = geom.n_tok
    widths = (_QA_W, _QA_W, _VA_W, DIFF_WIDTH, DIFF_WIDTH, DIFF_WIDTH)
    return pl.pallas_call(
        _attn_in_kernel,
        grid=(n // tm,),
        in_specs=[
            _row_spec(tm, D_MODEL),
            geom.mod_spec(layer, tm),
            pl.BlockSpec((12, tm, LANES), tab_index),
            _full_spec(w_in.shape),
            _full_spec(q_g.shape),
            _full_spec(w_uq.shape),
            _full_spec(kv_g.shape),
            _full_spec(w_ukv.shape),
            _full_spec(place.shape),
        ],
        out_specs=[_row_spec(tm, w) for w in widths],
        out_shape=[jax.ShapeDtypeStruct((n, w), BF16) for w in widths],
        compiler_params=_cparams(("parallel",)),
        name="attn_in",
    )(h, mods, tables, w_in, q_g, w_uq, kv_g, w_ukv, place)


def _rope_tables(seq, tm):
    rows = seq // GRID_W
    row = jnp.repeat(jnp.arange(rows, dtype=F32), GRID_W)
    col = jnp.tile(jnp.arange(GRID_W, dtype=F32), rows)

    def angles(rot_dim):
        n_freq = rot_dim // 4
        inv = ROPE_BASE ** (-jnp.arange(n_freq, dtype=F32) / n_freq)
        return jnp.concatenate([row[:, None] * inv, col[:, None] * inv], axis=-1)

    def build(ang, first_lo, half, live, scale):
        lane = jnp.arange(LANES)
        cos, sin = jnp.cos(ang), jnp.sin(ang)
        zeros = jnp.zeros((seq, LANES), F32)
        a = jnp.where(lane < live, 1.0, 0.0)[None, :] + zeros
        bm, cm = zeros, zeros
        for lo in first_lo:
            a = a.at[:, lo:lo + half].set(cos).at[:, lo + half:lo + 2 * half].set(cos)
            bm = bm.at[:, lo:lo + half].set(-sin)
            cm = cm.at[:, lo + half:lo + 2 * half].set(sin)
        ident = jnp.where(lane < live, 1.0, 0.0)[None, :] + jnp.zeros((tm, LANES), F32)
        z = jnp.zeros((tm, LANES), F32)
        return [jnp.concatenate([a, ident]) * scale, jnp.concatenate([bm, z]) * scale,
                jnp.concatenate([cm, z]) * scale]

    ang_m = angles(MLA_ROPE)
    ang_d = angles(DIFF_HEAD_DIM)
    tabs = (build(ang_m, (MLA_NOPE,), MLA_ROPE // 2, MLA_QK_DIM, MLA_SCALE)
            + build(ang_m, (0,), MLA_ROPE // 2, MLA_ROPE, 1.0)
            + build(ang_d, (0, DIFF_HEAD_DIM), DIFF_HEAD_DIM // 2, LANES, DIFF_SCALE)
            + build(ang_d, (0, DIFF_HEAD_DIM), DIFF_HEAD_DIM // 2, LANES, 1.0))
    return jnp.stack(tabs)


def _softmax_pv(q, keys, vals):
    s = [_bdot_nt(q, k) for k in keys]
    m = s[0].max(axis=-1, keepdims=True)
    for si in s[1:]:
        m = jnp.maximum(m, si.max(axis=-1, keepdims=True))
    num, den = None, None
    for si, v in zip(s, vals):
        p = jnp.exp(si - m)
        d = p.sum(axis=-1, keepdims=True)
        o = _bdot(p.astype(BF16), v)
        num = o if num is None else num + o
        den = d if den is None else den + d
    return num / den


def _mla_pair(q_ref, k_refs, v_refs):
    lane = lax.broadcasted_iota(jnp.int32, (q_ref.shape[0], HEAD_BLOCK), 1)
    vals = [v[...] for v in v_refs]
    o0 = _softmax_pv(q_ref[:, :HEAD_BLOCK], [k[:, :HEAD_BLOCK] for k in k_refs], vals)
    o1 = _softmax_pv(q_ref[:, HEAD_BLOCK:], [k[:, HEAD_BLOCK:] for k in k_refs], vals)
    return jnp.where(lane < MLA_V, o0, o1)


def _mla_kernel(q_ref, kx_ref, kc_ref, vx_ref, vc_ref, o_ref, *, n_lat_steps, with_ctx):
    qi = pl.program_id(2)

    def latent():
        o_ref[...] = _mla_pair(q_ref, (kc_ref, kx_ref), (vc_ref, vx_ref)).astype(o_ref.dtype)

    def context():
        o_ref[...] = _mla_pair(q_ref, (kc_ref,), (vc_ref,)).astype(o_ref.dtype)

    if with_ctx:
        pl.when(qi < n_lat_steps)(latent)
        pl.when(qi == n_lat_steps)(context)
    else:
        latent()


def _diff_heads(q_ref, k_refs, v_refs, lam, lam_init, g):
    q = q_ref[...]
    lane = lax.broadcasted_iota(jnp.int32, q.shape, 1)
    zero = jnp.zeros_like(q)
    keys = [k[...] for k in k_refs]
    vals = [v[...] for v in v_refs]
    o1 = _softmax_pv(jnp.where(lane < DIFF_HEAD_DIM, q, zero), keys, vals)
    o2 = _softmax_pv(jnp.where(lane >= DIFF_HEAD_DIM, q, zero), keys, vals)
    return _rms_rows(o1 - lam * o2, g) * (1.0 - lam_init)


def _diff_kernel(lam_ref, g_ref, q_ref, kx_ref, kc_ref, vx_ref, vc_ref, o_ref, *, n_lat_steps, with_ctx,
                 lam_init):
    qi = pl.program_id(2)
    lv = lam_ref[...]
    lam = (jnp.exp(jnp.sum(lv[0:1] * lv[1:2], axis=-1, keepdims=True))
           - jnp.exp(jnp.sum(lv[2:3] * lv[3:4], axis=-1, keepdims=True)) + lam_init)
    g = g_ref[...]

    def latent():
        o_ref[...] = _diff_heads(q_ref, (kc_ref, kx_ref), (vc_ref, vx_ref), lam, lam_init, g).astype(o_ref.dtype)

    def context():
        o_ref[...] = _diff_heads(q_ref, (kc_ref,), (vc_ref,), lam, lam_init, g).astype(o_ref.dtype)

    if with_ctx:
        pl.when(qi < n_lat_steps)(latent)
        pl.when(qi == n_lat_steps)(context)
    else:
        latent()


def _attn_specs(geom, tq, q_w, k_w, v_w, with_ctx):
    n_lat_steps = geom.seq // tq
    ctx_q0 = geom.n_lat // tq
    ctx_k0 = geom.n_lat // geom.ctx

    def q_index(b, p, qi):
        if with_ctx:
            return (jnp.where(qi < n_lat_steps, b * n_lat_steps + qi, ctx_q0 + b), p)
        return (b * n_lat_steps + qi, p)

    specs = [
        pl.BlockSpec((tq, q_w), q_index),
        pl.BlockSpec((geom.seq, k_w), lambda b, p, qi: (b, p)),
        pl.BlockSpec((geom.ctx, k_w), lambda b, p, qi: (ctx_k0 + b, p)),
        pl.BlockSpec((geom.seq, v_w), lambda b, p, qi: (b, p)),
        pl.BlockSpec((geom.ctx, v_w), lambda b, p, qi: (ctx_k0 + b, p)),
    ]
    out_spec = pl.BlockSpec((tq, HEAD_BLOCK), q_index)
    return specs, out_spec, n_lat_steps


def _mla_call(geom, qa, ka, va, with_ctx, tq):
    specs, out_spec, n_lat_steps = _attn_specs(geom, tq, 2 * HEAD_BLOCK, 2 * HEAD_BLOCK, HEAD_BLOCK, with_ctx)
    kern = functools.partial(_mla_kernel, n_lat_steps=n_lat_steps, with_ctx=with_ctx)
    return pl.pallas_call(
        kern,
        grid=(geom.batch, MLA_HEADS // 2, n_lat_steps + int(with_ctx)),
        in_specs=specs,
        out_specs=out_spec,
        out_shape=jax.ShapeDtypeStruct((geom.n_tok, _VA_W), BF16),
        compiler_params=_cparams(("parallel", "parallel", "arbitrary")),
        name="mla_attn",
    )(qa, ka, ka, va, va)


def _diff_call(geom, lam_vecs, subln_g, dq, dk, dv, with_ctx, tq, lam_init):
    specs, out_spec, n_lat_steps = _attn_specs(geom, tq, HEAD_BLOCK, HEAD_BLOCK, HEAD_BLOCK, with_ctx)
    kern = functools.partial(_diff_kernel, n_lat_steps=n_lat_steps, with_ctx=with_ctx, lam_init=lam_init)
    small = [pl.BlockSpec(lam_vecs.shape, lambda b, p, qi: (0, 0)),
             pl.BlockSpec(subln_g.shape, lambda b, p, qi: (0, 0))]
    return pl.pallas_call(
        kern,
        grid=(geom.batch, DIFF_HEADS, n_lat_steps + int(with_ctx)),
        in_specs=small + specs,
        out_specs=out_spec,
        out_shape=jax.ShapeDtypeStruct((geom.n_tok, DIFF_WIDTH), BF16),
        compiler_params=_cparams(("parallel", "parallel", "arbitrary")),
        name="diff_attn",
    )(lam_vecs, subln_g, dq, dk, dk, dv, dv)


def _attn_out_kernel(oa_ref, od_ref, h_ref, mod_ref, w_ref, g_ref, b_ref, o_ref):
    y = _bdot(oa_ref[...], w_ref[:_VA_W, :]) + _bdot(od_ref[...], w_ref[_VA_W:, :])
    r = mod_ref[5:6, :] * y
    o_ref[...] = _layer_norm_rows(DEEPNORM_ALPHA * h_ref[...] + r, g_ref[...], b_ref[...])


def _attn_out_call(geom, oa, od, h, mods, layer, w_out, ln_g, ln_b, n_rows, tm):
    return pl.pallas_call(
        _attn_out_kernel,
        grid=(n_rows // tm,),
        in_specs=[
            _row_spec(tm, _VA_W),
            _row_spec(tm, DIFF_WIDTH),
            _row_spec(tm, D_MODEL),
            geom.mod_spec(layer, tm),
            _full_spec(w_out.shape),
            _full_spec((1, D_MODEL)),
            _full_spec((1, D_MODEL)),
        ],
        out_specs=_row_spec(tm, D_MODEL),
        out_shape=jax.ShapeDtypeStruct((n_rows, D_MODEL), F32),
        compiler_params=_cparams(("parallel",)),
        name="attn_out",
    )(oa, od, h, mods, w_out, ln_g.reshape(1, -1), ln_b.reshape(1, -1))


SSD_PROJ_COLS = SSD_INNER + SSD_CONV_DIM + 2 * LANES
_S_XBC = SSD_INNER
_S_DT = SSD_INNER + SSD_CONV_DIM


def _ssd_in_kernel(h_ref, mod_ref, w_ref, z_ref, xbc_ref, dt_ref, *, nc):
    h = h_ref[...]
    u = (h * (1.0 + mod_ref[4:5, :]) + mod_ref[3:4, :]).astype(BF16)
    for lo in range(0, SSD_INNER, nc):
        z_ref[:, lo:lo + nc] = _bdot(u, w_ref[:, lo:lo + nc]).astype(BF16)
    for lo in range(0, SSD_CONV_DIM, nc):
        xbc_ref[:, lo:lo + nc] = _bdot(u, w_ref[:, _S_XBC + lo:_S_XBC + lo + nc]).astype(BF16)
    dt = _bdot(u, w_ref[:, _S_DT:])
    dt_ref[0] = dt[:, :LANES]
    dt_ref[1] = dt[:, LANES:]


def _ssd_in_call(geom, h, mods, layer, w_in, tm):
    n = geom.n_tok
    kern = functools.partial(_ssd_in_kernel, nc=512)
    return pl.pallas_call(
        kern,
        grid=(n // tm,),
        in_specs=[_row_spec(tm, D_MODEL), geom.mod_spec(layer, tm), _full_spec(w_in.shape)],
        out_specs=[_row_spec(tm, SSD_INNER), _row_spec(tm, SSD_CONV_DIM),
                   pl.BlockSpec((2, tm, LANES), lambda i: (0, i, 0))],
        out_shape=[jax.ShapeDtypeStruct((n, SSD_INNER), BF16), jax.ShapeDtypeStruct((n, SSD_CONV_DIM), BF16),
                   jax.ShapeDtypeStruct((2, n, LANES), F32)],
        compiler_params=_cparams(("parallel",)),
        name="ssd_in",
    )(h, mods, w_in)


def _conv_kernel(x_ref, prev_ref, next_ref, w_ref, b_ref, o_ref, pad_ref, *, tm, tiles_per_seq, n_lat_tiles, cw):
    i = pl.program_id(0)
    is_lat = i < n_lat_tiles
    pos = i % tiles_per_seq
    has_prev = jnp.logical_and(is_lat, pos > 0)
    has_next = jnp.logical_and(is_lat, pos < tiles_per_seq - 1)
    half = SSD_CONV // 2
    for lo in range(0, SSD_CONV_DIM, cw):
        cs = slice(lo, lo + cw)
        prev = prev_ref[:, cs].astype(F32)
        nxt = next_ref[:, cs].astype(F32)
        pad_ref[0:HALO, :] = jnp.where(has_prev, prev, jnp.zeros_like(prev))
        pad_ref[HALO:HALO + tm, :] = x_ref[:, cs].astype(F32)
        pad_ref[HALO + tm:, :] = jnp.where(has_next, nxt, jnp.zeros_like(nxt))
        acc = b_ref[:, cs] + jnp.zeros((tm, cw), F32)
        for k in range(SSD_CONV):
            off = HALO - half + k
            acc = acc + pad_ref[off:off + tm, :] * w_ref[k:k + 1, cs]
        o_ref[:, cs] = (acc * _sigmoid(acc)).astype(o_ref.dtype)


def _conv_call(geom, xbc, conv_w, conv_b, tm):
    n = geom.n_tok
    cw = 512
    tiles_per_seq = geom.seq // tm
    n_lat_tiles = geom.n_lat // tm
    hb = tm // HALO
    last_hb = n // HALO - 1
    kern = functools.partial(_conv_kernel, tm=tm, tiles_per_seq=tiles_per_seq, n_lat_tiles=n_lat_tiles, cw=cw)
    return pl.pallas_call(
        kern,
        grid=(n // tm,),
        in_specs=[
            _row_spec(tm, SSD_CONV_DIM),
            pl.BlockSpec((HALO, SSD_CONV_DIM), lambda i: (jnp.maximum(i * hb - 1, 0), 0)),
            pl.BlockSpec((HALO, SSD_CONV_DIM), lambda i: (jnp.minimum((i + 1) * hb, last_hb), 0)),
            _full_spec(conv_w.shape),
            _full_spec((1, SSD_CONV_DIM)),
        ],
        out_specs=_row_spec(tm, SSD_CONV_DIM),
        out_shape=jax.ShapeDtypeStruct((n, SSD_CONV_DIM), BF16),
        scratch_shapes=[pltpu.VMEM((tm + 2 * HALO, cw), F32)],
        compiler_params=_cparams(("parallel",)),
        name="ssd_conv",
    )(xbc, xbc, xbc, conv_w, conv_b.reshape(1, -1))


_GROUP_W = SSD_INNER // SSD_GROUPS
_HEADS_PER_GROUP = SSD_HEADS // SSD_GROUPS
_B_COL = SSD_INNER
_C_COL = SSD_INNER + SSD_GROUPS * SSD_STATE


def _expand_heads(v, expand):
    hi = v.astype(BF16)
    lo = (v - hi.astype(F32)).astype(BF16)
    return _bdot(hi, expand) + _bdot(lo, expand)


def _scan_kernel(xbc_ref, dt_ref, bias_ref, alog_ref, y_ref, state_ref):
    d = pl.program_id(1)
    s = pl.program_id(2)
    t = SSD_CHUNK

    @pl.when(s == 0)
    def _():
        state_ref[...] = jnp.zeros_like(state_ref)

    lane = lax.broadcasted_iota(jnp.int32, (1, LANES), 1)
    a = jnp.where(lane < SSD_HEADS, -jnp.exp(alog_ref[...]), 0.0)
    raw = dt_ref[...] + bias_ref[...]
    dt = jnp.maximum(raw, 0.0) + jnp.log1p(jnp.exp(-jnp.abs(raw)))
    da = dt * a
    row = lax.broadcasted_iota(jnp.int32, (t, t), 0)
    col = lax.broadcasted_iota(jnp.int32, (t, t), 1)
    mask = (row - col) * (1 - 2 * d) >= 0
    cum = jnp.dot(mask.astype(F32), da, preferred_element_type=F32, precision=lax.Precision.HIGHEST)
    total = jnp.sum(da, axis=0, keepdims=True)
    cum_t = cum.T
    dt_t = dt.T
    ecum = jnp.exp(cum)
    dtdec = dt * jnp.exp(total - cum)
    cdec = jnp.broadcast_to(jnp.exp(total), (8, LANES))

    erow = lax.broadcasted_iota(jnp.int32, (LANES, SSD_INNER), 0)
    ecol = lax.broadcasted_iota(jnp.int32, (LANES, SSD_INNER), 1)
    expand = (jnp.right_shift(ecol, SSD_HEAD_DIM.bit_length() - 1) == erow).astype(BF16)
    ex = _expand_heads(jnp.concatenate([dtdec, ecum, cdec], axis=0), expand)
    w_state = ex[0:t]
    w_off = ex[t:2 * t]
    w_carry = ex[2 * t:2 * t + 1]

    lane_t = lax.broadcasted_iota(jnp.int32, (t, LANES), 1)
    neg_inf = jnp.full((t, t), -jnp.inf, F32)
    for g in range(SSD_GROUPS):
        gs = slice(g * _GROUP_W, (g + 1) * _GROUP_W)
        b_g = xbc_ref[:, _B_COL + g * SSD_STATE:_B_COL + (g + 1) * SSD_STATE]
        c_g = xbc_ref[:, _C_COL + g * SSD_STATE:_C_COL + (g + 1) * SSD_STATE]
        cb = _bdot_nt(c_g, b_g)
        st = state_ref[g]
        y_off = _bdot(c_g, st.astype(BF16)) * w_off[:, gs]
        for pr in range(_HEADS_PER_GROUP // 2):
            xs = slice(g * _GROUP_W + pr * LANES, g * _GROUP_W + (pr + 1) * LANES)
            x_pair = xbc_ref[:, xs]
            ys = []
            for e in range(2):
                hd = g * _HEADS_PER_GROUP + 2 * pr + e
                seg = cum[:, hd:hd + 1] - cum_t[hd:hd + 1, :]
                lmat = jnp.exp(jnp.where(mask, seg, neg_inf))
                m = (cb * lmat * dt_t[hd:hd + 1, :]).astype(BF16)
                ys.append(_bdot(m, x_pair))
            y_pair = jnp.where(lane_t < SSD_HEAD_DIM, ys[0], ys[1])
            y_ref[:, xs] = y_pair + y_off[:, pr * LANES:(pr + 1) * LANES]
        xw = (xbc_ref[:, gs].astype(F32) * w_state[:, gs]).astype(BF16)
        b_t = b_g.astype(F32).T.astype(BF16)
        state_ref[g] = st * w_carry[:, gs] + _bdot(b_t, xw)


def _scan_call(geom, xbc, dt, dt_bias, a_log):
    t = SSD_CHUNK
    ncc = geom.ctx // t
    nlc = geom.seq // t
    ctx0 = geom.n_lat // t

    def row_block(b, d, s):
        jc = s + d * (ncc - 1 - 2 * s)
        sl = s - ncc
        jl = sl + d * (nlc - 1 - 2 * sl)
        return jnp.where(s < ncc, ctx0 + b * ncc + jc, b * nlc + jl)

    def pad_lanes(v):
        return jnp.pad(v.astype(F32), ((0, 0), (0, LANES - v.shape[-1]))).reshape(2, 1, LANES)

    return pl.pallas_call(
        _scan_kernel,
        grid=(geom.batch, 2, ncc + nlc),
        in_specs=[
            pl.BlockSpec((t, SSD_CONV_DIM), lambda b, d, s: (row_block(b, d, s), 0)),
            pl.BlockSpec((None, t, LANES), lambda b, d, s: (d, row_block(b, d, s), 0)),
            pl.BlockSpec((None, 1, LANES), lambda b, d, s: (d, 0, 0)),
            pl.BlockSpec((None, 1, LANES), lambda b, d, s: (d, 0, 0)),
        ],
        out_specs=pl.BlockSpec((None, t, SSD_INNER), lambda b, d, s: (d, row_block(b, d, s), 0)),
        out_shape=jax.ShapeDtypeStruct((2, geom.n_tok, SSD_INNER), F32),
        scratch_shapes=[pltpu.VMEM((SSD_GROUPS, SSD_STATE, _GROUP_W), F32)],
        compiler_params=_cparams(("parallel", "parallel", "arbitrary")),
        name="ssd_scan",
    )(xbc, dt, pad_lanes(dt_bias), pad_lanes(a_log))


def _ssd_out_kernel(yf_ref, yb_ref, xbc_ref, z_ref, dskip_ref, ng_ref, h_ref, mod_ref, w_ref, g_ref, b_ref, o_ref):
    dsk = dskip_ref[0:1, :] + dskip_ref[1:2, :]
    y = None
    for g in range(SSD_GROUPS):
        gs = slice(g * _GROUP_W, (g + 1) * _GROUP_W)
        z = z_ref[:, gs].astype(F32)
        gy = (yf_ref[:, gs] + yb_ref[:, gs] + dsk[:, gs] * xbc_ref[:, gs].astype(F32)) * (z * _sigmoid(z))
        gy = _rms_rows(gy, ng_ref[:, gs]).astype(BF16)
        part = _bdot(gy, w_ref[gs, :])
        y = part if y is None else y + part
    r = mod_ref[5:6, :] * y
    o_ref[...] = _layer_norm_rows(DEEPNORM_ALPHA * h_ref[...] + r, g_ref[...], b_ref[...])


def _ssd_out_call(geom, y, xbc, z, dskip, norm_g, h, mods, layer, w_out, ln_g, ln_b, n_rows, tm):
    return pl.pallas_call(
        _ssd_out_kernel,
        grid=(n_rows // tm,),
        in_specs=[
            pl.BlockSpec((None, tm, SSD_INNER), lambda i: (0, i, 0)),
            pl.BlockSpec((None, tm, SSD_INNER), lambda i: (1, i, 0)),
            _row_spec(tm, SSD_INNER),
            _row_spec(tm, SSD_INNER),
            _full_spec(dskip.shape),
            _full_spec((1, SSD_INNER)),
            _row_spec(tm, D_MODEL),
            geom.mod_spec(layer, tm),
            _full_spec(w_out.shape),
            _full_spec((1, D_MODEL)),
            _full_spec((1, D_MODEL)),
        ],
        out_specs=_row_spec(tm, D_MODEL),
        out_shape=jax.ShapeDtypeStruct((n_rows, D_MODEL), F32),
        compiler_params=_cparams(("parallel",)),
        name="ssd_out",
    )(y, y, xbc, z, dskip, norm_g.reshape(1, -1), h, mods, w_out, ln_g.reshape(1, -1), ln_b.reshape(1, -1))


def _attn_weights(w_in, w_uq, w_ukv):
    d = w_in.shape[0]
    kr_end = MLA_Q_RANK + MLA_KV_RANK + MLA_ROPE
    w_in_p = jnp.concatenate([w_in[:, :kr_end], jnp.zeros((d, HEAD_BLOCK - MLA_ROPE), w_in.dtype), w_in[:, kr_end:]],
                             axis=1).astype(BF16)
    pad_q = HEAD_BLOCK - MLA_QK_DIM
    w_uq_p = jnp.pad(w_uq.reshape(MLA_Q_RANK, MLA_HEADS, MLA_QK_DIM), ((0, 0), (0, 0), (0, pad_q)))
    w_uq_p = w_uq_p.reshape(MLA_Q_RANK, _QA_W).astype(BF16)
    kv = w_ukv.reshape(MLA_KV_RANK, MLA_HEADS, MLA_NOPE + MLA_V)
    w_uk_p = jnp.pad(kv[:, :, :MLA_NOPE], ((0, 0), (0, 0), (0, HEAD_BLOCK - MLA_NOPE))).reshape(MLA_KV_RANK, _QA_W)
    w_uv = kv[:, :, MLA_NOPE:].reshape(MLA_KV_RANK, _VA_W)
    w_ukv_p = jnp.concatenate([w_uk_p, w_uv], axis=1).astype(BF16)
    src = jnp.arange(HEAD_BLOCK)[:, None]
    dst = jnp.arange(_QA_W)[None, :]
    place = ((dst % HEAD_BLOCK == src + MLA_NOPE) & (src < MLA_ROPE)).astype(BF16)
    return w_in_p, w_uq_p, w_ukv_p, place


def _ssd_weights(w_in):
    d = w_in.shape[0]
    z = jnp.zeros((d, LANES - SSD_HEADS), w_in.dtype)
    return jnp.concatenate([w_in[:, :_S_DT], w_in[:, _S_DT:_S_DT + SSD_HEADS], z, w_in[:, _S_DT + SSD_HEADS:], z],
                           axis=1).astype(BF16)


def _lambda_init_for(layer):
    return 0.8 - 0.6 * math.exp(-0.3 * layer)


def _pick_tile(seq, ctx, n_ctx, want):
    tm = want
    while seq % tm or n_ctx % tm:
        tm //= 2
    return tm


def kernel(x, c, ctx, c_ctx, ada_w, ada_b, ln_g, ln_b, ffn1_w_gu, ffn1_w_down, ffn2_w_gu, ffn2_w_down, attn_w_in,
           mla_q_norm_g, mla_w_uq, mla_kv_norm_g, mla_w_ukv, diff_lam_q1, diff_lam_k1, diff_lam_q2, diff_lam_k2,
           diff_subln_g, attn_w_out, ssd_w_in, ssd_conv_w, ssd_conv_b, ssd_a_log, ssd_dt_bias, ssd_d, ssd_norm_g,
           ssd_w_out):
    batch, seq, d = x.shape
    n_ctx_tok = ctx.shape[1]
    geom = _Geom(batch, seq, n_ctx_tok)
    assert d == D_MODEL and batch + 1 <= MOD_ROWS
    assert seq % GRID_W == 0 and seq % SSD_CHUNK == 0 and n_ctx_tok % SSD_CHUNK == 0
    tq = n_ctx_tok
    assert seq % tq == 0
    tm_ffn = _pick_tile(seq, n_ctx_tok, geom.n_ctx, 512)
    tm_proj = _pick_tile(seq, n_ctx_tok, geom.n_ctx, 256)
    tm_conv = n_ctx_tok
    fc = 256

    cond = jnp.concatenate([c, c_ctx[None, :], jnp.zeros((MOD_ROWS - batch - 1, d), F32)], axis=0)
    mods = _ada_call(cond, ada_w, ada_b).reshape(DEPTH, MOD_ROWS, N_MOD, d)
    tables = _rope_tables(seq, tm_proj)

    h = jnp.concatenate([x.reshape(batch * seq, d), ctx.reshape(batch * n_ctx_tok, d)], axis=0)
    for l in range(DEPTH):
        last = l == DEPTH - 1
        h = _ffn_call(geom, h, mods, l, 0, ffn1_w_gu[l].astype(BF16), ffn1_w_down[l].astype(BF16),
                      ln_g[l, 0], ln_b[l, 0], geom.n_tok, tm_ffn, fc)
        n_out = geom.n_lat if last else geom.n_tok
        if l % 2 == 0:
            a = l // 2
            w_in_p, w_uq_p, w_ukv_p, place = _attn_weights(attn_w_in[a], mla_w_uq[a], mla_w_ukv[a])
            qa, ka, va, dq, dk, dv = _attn_in_call(geom, h, mods, l, tables, w_in_p, mla_q_norm_g[a].reshape(1, -1),
                                                   w_uq_p, mla_kv_norm_g[a].reshape(1, -1), w_ukv_p, place, tm_proj)
            oa = _mla_call(geom, qa, ka, va, not last, tq)
            lam_vecs = jnp.stack([diff_lam_q1[a], diff_lam_k1[a], diff_lam_q2[a], diff_lam_k2[a]])
            od = _diff_call(geom, lam_vecs, diff_subln_g[a].reshape(1, -1), dq, dk, dv, not last, tq,
                            _lambda_init_for(l))
            h = _attn_out_call(geom, oa, od, h, mods, l, attn_w_out[a].astype(BF16), ln_g[l, 1], ln_b[l, 1],
                               n_out, tm_ffn)
        else:
            s = l // 2
            z, xbc_raw, dt = _ssd_in_call(geom, h, mods, l, _ssd_weights(ssd_w_in[s]), tm_proj)
            xbc = _conv_call(geom, xbc_raw, ssd_conv_w[s], ssd_conv_b[s], tm_conv)
            y = _scan_call(geom, xbc, dt, ssd_dt_bias[s], ssd_a_log[s])
            dskip = jnp.repeat(ssd_d[s], SSD_HEAD_DIM, axis=-1)
            h = _ssd_out_call(geom, y, xbc, z, dskip, ssd_norm_g[s], h, mods, l, ssd_w_out[s].astype(BF16),
                              ln_g[l, 1], ln_b[l, 1], n_out, tm_proj)
        h = _ffn_call(geom, h, mods, l, 6, ffn2_w_gu[l].astype(BF16), ffn2_w_down[l].astype(BF16),
                      ln_g[l, 2], ln_b[l, 2], n_out, tm_ffn, fc)
    return h[:geom.n_lat].reshape(batch, seq, d)
```

```python
import functools
import math

import jax
import jax.numpy as jnp
from jax import lax
from jax.experimental import pallas as pl
from jax.experimental.pallas import tpu as pltpu

F32 = jnp.float32
BF16 = jnp.bfloat16

D_MODEL = 1024
DEPTH = 4
GRID_W = 64
N_MOD = 9
FFN_DIM = 2816
MACARON_WEIGHT = 0.5
MLA_HEADS = 8
MLA_Q_RANK = 384
MLA_KV_RANK = 256
MLA_NOPE = 64
MLA_ROPE = 32
MLA_V = 64
MLA_QK_DIM = MLA_NOPE + MLA_ROPE
MLA_SCALE = MLA_QK_DIM ** -0.5
DIFF_HEADS = 4
DIFF_HEAD_DIM = 64
DIFF_WIDTH = DIFF_HEADS * 2 * DIFF_HEAD_DIM
DIFF_SCALE = DIFF_HEAD_DIM ** -0.5
ROPE_BASE = 10000.0
SSD_INNER = 2 * D_MODEL
SSD_HEAD_DIM = 64
SSD_HEADS = SSD_INNER // SSD_HEAD_DIM
SSD_GROUPS = 4
SSD_STATE = 128
SSD_CONV = 5
SSD_CHUNK = 128
SSD_CONV_DIM = SSD_INNER + 2 * SSD_GROUPS * SSD_STATE
DEEPNORM_ALPHA = (2.0 * DEPTH) ** 0.25
LN_EPS = 1e-6
RMS_EPS = 1e-6

LANES = 128
V7X_VMEM_LIMIT = 56 * 1024 * 1024
HALO = 16

HEAD_BLOCK = LANES
MOD_ROWS = 24


def _cparams(sem):
    return pltpu.CompilerParams(dimension_semantics=sem, vmem_limit_bytes=V7X_VMEM_LIMIT)


def _sigmoid(v):
    return 1.0 / (1.0 + jnp.exp(-v))


def _layer_norm_rows(v, g, b):
    mu = jnp.mean(v, axis=-1, keepdims=True)
    c = v - mu
    var = jnp.mean(c * c, axis=-1, keepdims=True)
    return c * lax.rsqrt(var + LN_EPS) * g + b


def _rms_rows(v, g):
    return v * lax.rsqrt(jnp.mean(v * v, axis=-1, keepdims=True) + RMS_EPS) * g


def _bdot(a, b):
    return jnp.dot(a, b, preferred_element_type=F32)


def _bdot_nt(a, b):
    return lax.dot_general(a, b, (((1,), (1,)), ((), ())), preferred_element_type=F32)


def _ada_kernel(c_ref, w_ref, b_ref, o_ref):
    c = c_ref[...]
    s = (c * _sigmoid(c)).astype(BF16)
    o_ref[...] = _bdot(s, w_ref[...].astype(BF16)) + b_ref[...]


def _ada_call(cond, ada_w, ada_b):
    depth, d, n = ada_w.shape
    tn = n // 8
    return pl.pallas_call(
        _ada_kernel,
        grid=(depth, n // tn),
        in_specs=[
            pl.BlockSpec((MOD_ROWS, d), lambda l, j: (0, 0)),
            pl.BlockSpec((None, d, tn), lambda l, j: (l, 0, j)),
            pl.BlockSpec((None, 1, tn), lambda l, j: (l, 0, j)),
        ],
        out_specs=pl.BlockSpec((None, MOD_ROWS, tn), lambda l, j: (l, 0, j)),
        out_shape=jax.ShapeDtypeStruct((depth, MOD_ROWS, n), F32),
        compiler_params=_cparams(("parallel", "parallel")),
        name="ada_mod",
    )(cond, ada_w, ada_b.reshape(depth, 1, n))


class _Geom:
    def __init__(self, batch, seq, ctx):
        self.batch, self.seq, self.ctx = batch, seq, ctx
        self.n_lat = batch * seq
        self.n_ctx = batch * ctx
        self.n_tok = self.n_lat + self.n_ctx

    def mod_spec(self, layer, tm):
        n_lat_tiles = self.n_lat // tm
        tiles_per_batch = self.seq // tm
        batch = self.batch

        def index(i):
            return (layer, jnp.where(i < n_lat_tiles, i // tiles_per_batch, batch), 0, 0)

        return pl.BlockSpec((None, None, N_MOD, D_MODEL), index)


def _row_spec(tm, width):
    return pl.BlockSpec((tm, width), lambda i: (i, 0))


def _full_spec(shape):
    nd = len(shape)
    return pl.BlockSpec(shape, lambda i: (0,) * nd)


def _ffn_kernel(h_ref, mod_ref, wgu_ref, wd_ref, g_ref, b_ref, o_ref, acc_ref, *, k0, fc):
    h = h_ref[...]
    shift = mod_ref[k0:k0 + 1, :]
    scale = mod_ref[k0 + 1:k0 + 2, :]
    gate = mod_ref[k0 + 2:k0 + 3, :]
    t = (h * (1.0 + scale) + shift).astype(BF16)
    for j in range(FFN_DIM // fc):
        gj = _bdot(t, wgu_ref[:, j * fc:(j + 1) * fc])
        uj = _bdot(t, wgu_ref[:, FFN_DIM + j * fc:FFN_DIM + (j + 1) * fc])
        a = (gj * _sigmoid(gj) * uj).astype(BF16)
        y = _bdot(a, wd_ref[j * fc:(j + 1) * fc, :])
        if j == 0:
            acc_ref[...] = y
        else:
            acc_ref[...] += y
    r = (MACARON_WEIGHT * gate) * acc_ref[...]
    o_ref[...] = _layer_norm_rows(DEEPNORM_ALPHA * h + r, g_ref[...], b_ref[...])


def _ffn_call(geom, h, mods, layer, k0, w_gu, w_down, ln_g, ln_b, n_rows, tm, fc):
    kern = functools.partial(_ffn_kernel, k0=k0, fc=fc)
    return pl.pallas_call(
        kern,
        grid=(n_rows // tm,),
        in_specs=[
            _row_spec(tm, D_MODEL),
            geom.mod_spec(layer, tm),
            _full_spec(w_gu.shape),
            _full_spec(w_down.shape),
            _full_spec((1, D_MODEL)),
            _full_spec((1, D_MODEL)),
        ],
        out_specs=_row_spec(tm, D_MODEL),
        out_shape=jax.ShapeDtypeStruct((n_rows, D_MODEL), F32),
        scratch_shapes=[pltpu.VMEM((tm, D_MODEL), F32)],
        compiler_params=_cparams(("parallel",)),
        name="ffn",
    )(h, mods, w_gu, w_down, ln_g.reshape(1, -1), ln_b.reshape(1, -1))


_QA_W = MLA_HEADS * HEAD_BLOCK
_VA_W = MLA_HEADS * MLA_V
_R_CKV = MLA_Q_RANK
_R_KR = MLA_Q_RANK + MLA_KV_RANK
_R_DK = _R_KR + HEAD_BLOCK
ATTN_ROW_COLS = _R_DK + DIFF_WIDTH
_MLA_HALF = MLA_ROPE // 2
_DIFF_HALF = DIFF_HEAD_DIM // 2
_T_SIN_M = _MLA_HALF
_T_COS_D = 2 * _MLA_HALF
_T_SIN_D = _T_COS_D + _DIFF_HALF
ROPE_T_ROWS = _T_SIN_D + _DIFF_HALF
ATT_KC = 256
LOG2E = math.log2(math.e)
MLA_QSCALE = MLA_SCALE * LOG2E
DIFF_QSCALE = DIFF_SCALE * LOG2E


def _rope_block(v, tab_ref, t0, shift):
    left = pltpu.roll(v, LANES - shift, axis=1)
    right = pltpu.roll(v, shift, axis=1)
    return v * tab_ref[t0] + left * tab_ref[t0 + 1] + right * tab_ref[t0 + 2]


def _rope_rows(dst_ref, src, r0, half, cos, sin, scale):
    t1 = src[r0:r0 + half]
    t2 = src[r0 + half:r0 + 2 * half]
    dst_ref[r0:r0 + half, :] = ((t1 * cos - t2 * sin) * scale).astype(dst_ref.dtype)
    dst_ref[r0 + half:r0 + 2 * half, :] = ((t1 * sin + t2 * cos) * scale).astype(dst_ref.dtype)


def _attn_in_kernel(h_ref, mod_ref, rtab_ref, ttab_ref, wrow_ref, wcol_ref, qg_ref, wuqt_ref, kvg_ref, wuk_ref,
                    wuvt_ref, place_ref, qat_ref, ka_ref, vat_ref, dqt_ref, dk_ref, dvt_ref):
    h = h_ref[...]
    u = (h * (1.0 + mod_ref[4:5, :]) + mod_ref[3:4, :]).astype(BF16)
    proj = _bdot(u, wrow_ref[...])
    projt = _bdot_nt(wcol_ref[...], u)

    cqn = _rms_rows(proj[:, :_R_CKV], qg_ref[...]).astype(BF16)
    qt = _bdot_nt(wuqt_ref[...], cqn)
    cos_m = ttab_ref[0:_T_SIN_M, :]
    sin_m = ttab_ref[_T_SIN_M:_T_COS_D, :]
    for hd in range(MLA_HEADS):
        r0 = hd * HEAD_BLOCK
        qat_ref[r0:r0 + MLA_NOPE, :] = (qt[r0:r0 + MLA_NOPE] * MLA_QSCALE).astype(BF16)
        _rope_rows(qat_ref, qt, r0 + MLA_NOPE, _MLA_HALF, cos_m, sin_m, MLA_QSCALE)
        qat_ref[r0 + MLA_QK_DIM:r0 + HEAD_BLOCK, :] = jnp.zeros((HEAD_BLOCK - MLA_QK_DIM, qt.shape[1]), BF16)

    ckvn = _rms_rows(proj[:, _R_CKV:_R_KR], kvg_ref[...]).astype(BF16)
    kr = _rope_block(proj[:, _R_KR:_R_DK], rtab_ref, 0, _MLA_HALF).astype(BF16)
    ka_ref[...] = (_bdot(ckvn, wuk_ref[...]) + _bdot(kr, place_ref[...])).astype(BF16)
    vat_ref[...] = _bdot_nt(wuvt_ref[...], ckvn).astype(BF16)

    cos_d = ttab_ref[_T_COS_D:_T_SIN_D, :]
    sin_d = ttab_ref[_T_SIN_D:ROPE_T_ROWS, :]
    for sub in range(2 * DIFF_HEADS):
        _rope_rows(dqt_ref, projt, sub * DIFF_HEAD_DIM, _DIFF_HALF, cos_d, sin_d, DIFF_QSCALE)
    dvt_ref[...] = projt[DIFF_WIDTH:].astype(BF16)
    for hd in range(DIFF_HEADS):
        sk = slice(_R_DK + hd * HEAD_BLOCK, _R_DK + (hd + 1) * HEAD_BLOCK)
        so = slice(hd * HEAD_BLOCK, (hd + 1) * HEAD_BLOCK)
        dk_ref[:, so] = _rope_block(proj[:, sk], rtab_ref, 3, _DIFF_HALF).astype(BF16)


def _attn_in_call(geom, h, mods, layer, rtab, ttab, wts, q_g, kv_g, tm):
    n_lat_tiles = geom.n_lat // tm
    tiles_per_seq = geom.seq // tm

    def pos_block(i):
        return jnp.where(i < n_lat_tiles, i % tiles_per_seq, tiles_per_seq)

    n = geom.n_tok
    w_row, w_col, w_uqt, w_uk, w_uvt, place = wts

    def tok_major(width):
        return _row_spec(tm, width), jax.ShapeDtypeStruct((n, width), BF16)

    def chan_major(rows):
        return pl.BlockSpec((rows, tm), lambda i: (0, i)), jax.ShapeDtypeStruct((rows, n), BF16)

    def chan_major_chunked(rows):
        return (pl.BlockSpec((None, rows, tm), lambda i: (i, 0, 0)),
                jax.ShapeDtypeStruct((n // tm, rows, tm), BF16))

    assert tm == ATT_KC
    outs = [chan_major(_QA_W), tok_major(_QA_W), chan_major_chunked(_VA_W),
            chan_major(DIFF_WIDTH), tok_major(DIFF_WIDTH), chan_major_chunked(DIFF_WIDTH)]
    return pl.pallas_call(
        _attn_in_kernel,
        grid=(n // tm,),
        in_specs=[
            _row_spec(tm, D_MODEL),
            geom.mod_spec(layer, tm),
            pl.BlockSpec((6, tm, LANES), lambda i: (0, pos_block(i), 0)),
            pl.BlockSpec((ROPE_T_ROWS, tm), lambda i: (0, pos_block(i))),
            _full_spec(w_row.shape),
            _full_spec(w_col.shape),
            _full_spec(q_g.shape),
            _full_spec(w_uqt.shape),
            _full_spec(kv_g.shape),
            _full_spec(w_uk.shape),
            _full_spec(w_uvt.shape),
            _full_spec(place.shape),
        ],
        out_specs=[o[0] for o in outs],
        out_shape=[o[1] for o in outs],
        compiler_params=_cparams(("parallel",)),
        name="attn_in",
    )(h, mods, rtab, ttab, w_row, w_col, q_g, w_uqt, kv_g, w_uk, w_uvt, place)


def _rope_tables(seq, tm):
    rows = seq // GRID_W
    row = jnp.repeat(jnp.arange(rows, dtype=F32), GRID_W)
    col = jnp.tile(jnp.arange(GRID_W, dtype=F32), rows)

    def angles(rot_dim):
        n_freq = rot_dim // 4
        inv = ROPE_BASE ** (-jnp.arange(n_freq, dtype=F32) / n_freq)
        return jnp.concatenate([row[:, None] * inv, col[:, None] * inv], axis=-1)

    def build(ang, first_lo, half, live):
        lane = jnp.arange(LANES)
        cos, sin = jnp.cos(ang), jnp.sin(ang)
        zeros = jnp.zeros((seq, LANES), F32)
        a = jnp.where(lane < live, 1.0, 0.0)[None, :] + zeros
        bm, cm = zeros, zeros
        for lo in first_lo:
            a = a.at[:, lo:lo + half].set(cos).at[:, lo + half:lo + 2 * half].set(cos)
            bm = bm.at[:, lo:lo + half].set(-sin)
            cm = cm.at[:, lo + half:lo + 2 * half].set(sin)
        ident = jnp.where(lane < live, 1.0, 0.0)[None, :] + jnp.zeros((tm, LANES), F32)
        z = jnp.zeros((tm, LANES), F32)
        return [jnp.concatenate([a, ident]), jnp.concatenate([bm, z]), jnp.concatenate([cm, z])]

    ang_m = angles(MLA_ROPE)
    ang_d = angles(DIFF_HEAD_DIM)
    rtab = jnp.stack(build(ang_m, (0,), _MLA_HALF, MLA_ROPE)
                     + build(ang_d, (0, DIFF_HEAD_DIM), _DIFF_HALF, LANES))

    def chan(ang, fn, fill):
        return jnp.concatenate([fn(ang).T, jnp.full((ang.shape[1], tm), fill, F32)], axis=1)

    ttab = jnp.concatenate([chan(ang_m, jnp.cos, 1.0), chan(ang_m, jnp.sin, 0.0),
                            chan(ang_d, jnp.cos, 1.0), chan(ang_d, jnp.sin, 0.0)], axis=0)
    return rtab, ttab


def _softmax_pv_t(qt, keys, vals_t):
    s = [_bdot(k, qt) for k in keys]
    m = s[0].max(axis=0, keepdims=True)
    for si in s[1:]:
        m = jnp.maximum(m, si.max(axis=0, keepdims=True))
    num, den = None, None
    for si, vt in zip(s, vals_t):
        p = jnp.exp2(si - m)
        d = p.sum(axis=0, keepdims=True)
        o = _bdot(vt, p.astype(BF16))
        num = o if num is None else num + o
        den = d if den is None else den + d
    return num / den


def _fold_rows(v, op):
    parts = [v[r:r + 8] for r in range(0, v.shape[0], 8)]
    while len(parts) > 1:
        parts = [op(parts[i], parts[i + 1]) for i in range(0, len(parts) - 1, 2)] + (
            [parts[-1]] if len(parts) % 2 else [])
    return parts[0]


def _attn_pipelined(qi, n_q, q_cur, q_nxt, k_lanes, kx_ref, kc_ref, vx_ref, vc_ref, s_refs, m_ref, acc_ref, o_ref,
                    combine):
    n_maps = len(q_cur)
    tq = q_cur[0].shape[1]
    chunks = ([(kc_ref, vc_ref, c) for c in range(kc_ref.shape[0])]
              + [(kx_ref, vx_ref, c) for c in range(kx_ref.shape[0])])
    neg = jnp.full((8, tq), -jnp.inf, F32)
    zero = jnp.zeros((8, tq), F32)

    def score_chunk(dst_ref, ci, q_list, mrun):
        k_ref, _, c = chunks[ci]
        out = []
        for j in range(n_maps):
            s = _bdot(k_ref[c, :, k_lanes[j]], q_list[j])
            dst_ref[j, ci] = s
            out.append(jnp.maximum(mrun[j], _fold_rows(s, jnp.maximum)))
        return out

    def exp_chunk(src_ref, ci, m_cur, lrun):
        _, v_ref, c = chunks[ci]
        vt = v_ref[c]
        out = []
        for j in range(n_maps):
            p = jnp.exp2(src_ref[j, ci] - m_cur[j])
            out.append(lrun[j] + _fold_rows(p, jnp.add))
            o = _bdot(vt, p.astype(BF16))
            if ci == 0:
                acc_ref[j] = o
            else:
                acc_ref[j] += o
        return out

    @pl.when(qi == 0)
    def _():
        mrun = [neg] * n_maps
        for ci in range(len(chunks)):
            mrun = score_chunk(s_refs[0], ci, q_cur, mrun)
        for j in range(n_maps):
            m_ref[0, j] = mrun[j]

    def step(slot, with_next):
        cur_ref, nxt_ref = s_refs[slot], s_refs[1 - slot]
        m_cur = [jnp.max(m_ref[slot, j], axis=0, keepdims=True) for j in range(n_maps)]
        lrun, mrun = [zero] * n_maps, [neg] * n_maps
        for ci in range(len(chunks)):
            if with_next:
                mrun = score_chunk(nxt_ref, ci, q_nxt, mrun)
            lrun = exp_chunk(cur_ref, ci, m_cur, lrun)
        if with_next:
            for j in range(n_maps):
                m_ref[1 - slot, j] = mrun[j]
        outs = [acc_ref[j] / jnp.sum(lrun[j], axis=0, keepdims=True) for j in range(n_maps)]
        o_ref[...] = combine(outs).astype(o_ref.dtype)

    for slot in (0, 1):
        pl.when(jnp.logical_and(qi % 2 == slot, qi < n_q - 1))(functools.partial(step, slot, True))
    pl.when(qi == n_q - 1)(functools.partial(step, (n_q - 1) % 2, False))


def _mla_maps(q_ref):
    return [q_ref[:HEAD_BLOCK, :], q_ref[HEAD_BLOCK:, :]]


_MLA_K_LANES = (slice(0, HEAD_BLOCK), slice(HEAD_BLOCK, 2 * HEAD_BLOCK))
_DIFF_K_LANES = (slice(0, HEAD_BLOCK), slice(0, HEAD_BLOCK))


def _mla_combine(outs):
    row = lax.broadcasted_iota(jnp.int32, outs[0].shape, 0)
    return jnp.where(row < MLA_V, outs[0], outs[1]).T


def _diff_maps(q_ref):
    q = q_ref[...]
    row = lax.broadcasted_iota(jnp.int32, q.shape, 0)
    zero = jnp.zeros_like(q)
    return [jnp.where(row < DIFF_HEAD_DIM, q, zero), jnp.where(row >= DIFF_HEAD_DIM, q, zero)]


def _diff_lambda(lam_ref, lam_init):
    lv = lam_ref[...]
    return (jnp.exp(jnp.sum(lv[0:1] * lv[1:2], axis=-1, keepdims=True))
            - jnp.exp(jnp.sum(lv[2:3] * lv[3:4], axis=-1, keepdims=True)) + lam_init)


def _diff_combine(outs, lam, lam_init, g):
    return _rms_rows((outs[0] - lam * outs[1]).T, g) * (1.0 - lam_init)


def _mla_lat_kernel(qc_ref, qn_ref, kx_ref, kc_ref, vx_ref, vc_ref, o_ref, s0_ref, s1_ref, m_ref, acc_ref, *, n_q):
    _attn_pipelined(pl.program_id(2), n_q, _mla_maps(qc_ref), _mla_maps(qn_ref), _MLA_K_LANES, kx_ref, kc_ref,
                    vx_ref, vc_ref, (s0_ref, s1_ref), m_ref, acc_ref, o_ref, _mla_combine)


def _diff_lat_kernel(lam_ref, g_ref, qc_ref, qn_ref, kx_ref, kc_ref, vx_ref, vc_ref, o_ref, s0_ref, s1_ref, m_ref,
                     acc_ref, *, n_q, lam_init):
    combine = functools.partial(_diff_combine, lam=_diff_lambda(lam_ref, lam_init), lam_init=lam_init, g=g_ref[...])
    _attn_pipelined(pl.program_id(2), n_q, _diff_maps(qc_ref), _diff_maps(qn_ref), _DIFF_K_LANES, kx_ref, kc_ref,
                    vx_ref, vc_ref, (s0_ref, s1_ref), m_ref, acc_ref, o_ref, combine)


def _ctx_attention(q_maps, k_lanes, kc_ref, vc_ref):
    n_cc = kc_ref.shape[0]
    vals = [vc_ref[cc] for cc in range(n_cc)]
    return [_softmax_pv_t(q, [kc_ref[cc, :, lanes] for cc in range(n_cc)], vals)
            for q, lanes in zip(q_maps, k_lanes)]


def _mla_ctx_kernel(q_ref, kc_ref, vc_ref, o_ref):
    o_ref[...] = _mla_combine(_ctx_attention(_mla_maps(q_ref), _MLA_K_LANES, kc_ref, vc_ref)).astype(o_ref.dtype)


def _diff_ctx_kernel(lam_ref, g_ref, q_ref, kc_ref, vc_ref, o_ref, *, lam_init):
    outs = _ctx_attention(_diff_maps(q_ref), _DIFF_K_LANES, kc_ref, vc_ref)
    o_ref[...] = _diff_combine(outs, _diff_lambda(lam_ref, lam_init), lam_init, g_ref[...]).astype(o_ref.dtype)


def _attn_lat_call(geom, kern, name, small, qt, k3, vt3, qk_w, n_groups, tq):
    n_q = geom.seq // tq
    n_lc = geom.seq // ATT_KC
    n_cc = geom.ctx // ATT_KC
    ctx_c0 = geom.n_lat // geom.ctx
    small_specs = [pl.BlockSpec(a.shape, lambda b, p, qi: (0, 0)) for a in small]
    specs = small_specs + [
        pl.BlockSpec((qk_w, tq), lambda b, p, qi: (p, b * n_q + qi)),
        pl.BlockSpec((qk_w, tq), lambda b, p, qi: (p, b * n_q + jnp.minimum(qi + 1, n_q - 1))),
        pl.BlockSpec((n_lc, ATT_KC, qk_w), lambda b, p, qi: (b, 0, p)),
        pl.BlockSpec((n_cc, ATT_KC, qk_w), lambda b, p, qi: (ctx_c0 + b, 0, p)),
        pl.BlockSpec((n_lc, HEAD_BLOCK, ATT_KC), lambda b, p, qi: (b, p, 0)),
        pl.BlockSpec((n_cc, HEAD_BLOCK, ATT_KC), lambda b, p, qi: (ctx_c0 + b, p, 0)),
    ]
    return pl.pallas_call(
        functools.partial(kern, n_q=n_q),
        grid=(geom.batch, n_groups, n_q),
        in_specs=specs,
        out_specs=pl.BlockSpec((tq, HEAD_BLOCK), lambda b, p, qi: (b * n_q + qi, p)),
        out_shape=jax.ShapeDtypeStruct((geom.n_lat, n_groups * HEAD_BLOCK), BF16),
        scratch_shapes=[pltpu.VMEM((2, n_cc + n_lc, ATT_KC, tq), F32),
                        pltpu.VMEM((2, n_cc + n_lc, ATT_KC, tq), F32),
                        pltpu.VMEM((2, 2, 8, tq), F32),
                        pltpu.VMEM((2, HEAD_BLOCK, tq), F32)],
        compiler_params=_cparams(("parallel", "parallel", "arbitrary")),
        name=name,
    )(*small, qt, qt, k3, k3, vt3, vt3)


def _attn_ctx_call(geom, kern, name, small, qt, k3, vt3, qk_w, n_groups):
    n_cc = geom.ctx // ATT_KC
    ctx_c0 = geom.n_lat // geom.ctx
    small_specs = [pl.BlockSpec(a.shape, lambda b, p: (0, 0)) for a in small]
    specs = small_specs + [
        pl.BlockSpec((qk_w, geom.ctx), lambda b, p: (p, ctx_c0 + b)),
        pl.BlockSpec((n_cc, ATT_KC, qk_w), lambda b, p: (ctx_c0 + b, 0, p)),
        pl.BlockSpec((n_cc, HEAD_BLOCK, ATT_KC), lambda b, p: (ctx_c0 + b, p, 0)),
    ]
    return pl.pallas_call(
        kern,
        grid=(geom.batch, n_groups),
        in_specs=specs,
        out_specs=pl.BlockSpec((geom.ctx, HEAD_BLOCK), lambda b, p: (b, p)),
        out_shape=jax.ShapeDtypeStruct((geom.n_ctx, n_groups * HEAD_BLOCK), BF16),
        compiler_params=_cparams(("parallel", "parallel")),
        name=name,
    )(*small, qt, k3, vt3)


def _attention(geom, qa, ka, va, dq, dk, dv, lam_vecs, subln_g, lam_init, with_ctx, tq):
    n_chunks = geom.n_tok // ATT_KC
    ka3 = ka.reshape(n_chunks, ATT_KC, _QA_W)
    dk3 = dk.reshape(n_chunks, ATT_KC, DIFF_WIDTH)
    small = [lam_vecs, subln_g]
    lat = (_attn_lat_call(geom, _mla_lat_kernel, "mla_attn", [], qa, ka3, va, 2 * HEAD_BLOCK, MLA_HEADS // 2, tq),
           _attn_lat_call(geom, functools.partial(_diff_lat_kernel, lam_init=lam_init), "diff_attn", small, dq, dk3,
                          dv, HEAD_BLOCK, DIFF_HEADS, tq))
    if not with_ctx:
        return lat, None
    ctx = (_attn_ctx_call(geom, _mla_ctx_kernel, "mla_attn_ctx", [], qa, ka3, va, 2 * HEAD_BLOCK, MLA_HEADS // 2),
           _attn_ctx_call(geom, functools.partial(_diff_ctx_kernel, lam_init=lam_init), "diff_attn_ctx", small, dq,
                          dk3, dv, HEAD_BLOCK, DIFF_HEADS))
    return lat, ctx


def _attn_out_kernel(*refs, n_lat_tiles, with_ctx):
    if with_ctx:
        oa_ref, od_ref, oac_ref, odc_ref, h_ref, mod_ref, w_ref, g_ref, b_ref, o_ref = refs
        is_lat = pl.program_id(0) < n_lat_tiles
        oa = jnp.where(is_lat, oa_ref[...], oac_ref[...])
        od = jnp.where(is_lat, od_ref[...], odc_ref[...])
    else:
        oa_ref, od_ref, h_ref, mod_ref, w_ref, g_ref, b_ref, o_ref = refs
        oa, od = oa_ref[...], od_ref[...]
    y = _bdot(oa, w_ref[:_VA_W, :]) + _bdot(od, w_ref[_VA_W:, :])
    r = mod_ref[5:6, :] * y
    o_ref[...] = _layer_norm_rows(DEEPNORM_ALPHA * h_ref[...] + r, g_ref[...], b_ref[...])


def _attn_out_call(geom, lat, ctx, h, mods, layer, w_out, ln_g, ln_b, tm):
    n_lat_tiles = geom.n_lat // tm
    with_ctx = ctx is not None
    n_rows = geom.n_tok if with_ctx else geom.n_lat
    head_specs = [pl.BlockSpec((tm, _VA_W), lambda i: (jnp.minimum(i, n_lat_tiles - 1), 0)),
                  pl.BlockSpec((tm, DIFF_WIDTH), lambda i: (jnp.minimum(i, n_lat_tiles - 1), 0))]
    heads = list(lat)
    if with_ctx:
        head_specs += [pl.BlockSpec((tm, _VA_W), lambda i: (jnp.maximum(i - n_lat_tiles, 0), 0)),
                       pl.BlockSpec((tm, DIFF_WIDTH), lambda i: (jnp.maximum(i - n_lat_tiles, 0), 0))]
        heads += list(ctx)
    return pl.pallas_call(
        functools.partial(_attn_out_kernel, n_lat_tiles=n_lat_tiles, with_ctx=with_ctx),
        grid=(n_rows // tm,),
        in_specs=head_specs + [
            _row_spec(tm, D_MODEL),
            geom.mod_spec(layer, tm),
            _full_spec(w_out.shape),
            _full_spec((1, D_MODEL)),
            _full_spec((1, D_MODEL)),
        ],
        out_specs=_row_spec(tm, D_MODEL),
        out_shape=jax.ShapeDtypeStruct((n_rows, D_MODEL), F32),
        compiler_params=_cparams(("parallel",)),
        name="attn_out",
    )(*heads, h, mods, w_out, ln_g.reshape(1, -1), ln_b.reshape(1, -1))


SSD_PROJ_COLS = SSD_INNER + SSD_CONV_DIM + 2 * LANES
_S_XBC = SSD_INNER
_S_DT = SSD_INNER + SSD_CONV_DIM


def _ssd_in_kernel(h_ref, mod_ref, w_ref, z_ref, xbc_ref, dt_ref, *, nc):
    h = h_ref[...]
    u = (h * (1.0 + mod_ref[4:5, :]) + mod_ref[3:4, :]).astype(BF16)
    for lo in range(0, SSD_INNER, nc):
        z_ref[:, lo:lo + nc] = _bdot(u, w_ref[:, lo:lo + nc]).astype(BF16)
    for lo in range(0, SSD_CONV_DIM, nc):
        xbc_ref[:, lo:lo + nc] = _bdot(u, w_ref[:, _S_XBC + lo:_S_XBC + lo + nc]).astype(BF16)
    dt = _bdot(u, w_ref[:, _S_DT:])
    dt_ref[0] = dt[:, :LANES]
    dt_ref[1] = dt[:, LANES:]


def _ssd_in_call(geom, h, mods, layer, w_in, tm):
    n = geom.n_tok
    kern = functools.partial(_ssd_in_kernel, nc=512)
    return pl.pallas_call(
        kern,
        grid=(n // tm,),
        in_specs=[_row_spec(tm, D_MODEL), geom.mod_spec(layer, tm), _full_spec(w_in.shape)],
        out_specs=[_row_spec(tm, SSD_INNER), _row_spec(tm, SSD_CONV_DIM),
                   pl.BlockSpec((2, tm, LANES), lambda i: (0, i, 0))],
        out_shape=[jax.ShapeDtypeStruct((n, SSD_INNER), BF16), jax.ShapeDtypeStruct((n, SSD_CONV_DIM), BF16),
                   jax.ShapeDtypeStruct((2, n, LANES), F32)],
        compiler_params=_cparams(("parallel",)),
        name="ssd_in",
    )(h, mods, w_in)


def _conv_kernel(x_ref, prev_ref, next_ref, w_ref, b_ref, o_ref, pad_ref, *, tm, tiles_per_seq, n_lat_tiles, cw):
    i = pl.program_id(0)
    is_lat = i < n_lat_tiles
    pos = i % tiles_per_seq
    has_prev = jnp.logical_and(is_lat, pos > 0)
    has_next = jnp.logical_and(is_lat, pos < tiles_per_seq - 1)
    half = SSD_CONV // 2
    for lo in range(0, SSD_CONV_DIM, cw):
        cs = slice(lo, lo + cw)
        prev = prev_ref[:, cs].astype(F32)
        nxt = next_ref[:, cs].astype(F32)
        pad_ref[0:HALO, :] = jnp.where(has_prev, prev, jnp.zeros_like(prev))
        pad_ref[HALO:HALO + tm, :] = x_ref[:, cs].astype(F32)
        pad_ref[HALO + tm:, :] = jnp.where(has_next, nxt, jnp.zeros_like(nxt))
        acc = b_ref[:, cs] + jnp.zeros((tm, cw), F32)
        for k in range(SSD_CONV):
            off = HALO - half + k
            acc = acc + pad_ref[off:off + tm, :] * w_ref[k:k + 1, cs]
        o_ref[:, cs] = (acc * _sigmoid(acc)).astype(o_ref.dtype)


def _conv_call(geom, xbc, conv_w, conv_b, tm):
    n = geom.n_tok
    cw = 512
    tiles_per_seq = geom.seq // tm
    n_lat_tiles = geom.n_lat // tm
    hb = tm // HALO
    last_hb = n // HALO - 1
    kern = functools.partial(_conv_kernel, tm=tm, tiles_per_seq=tiles_per_seq, n_lat_tiles=n_lat_tiles, cw=cw)
    return pl.pallas_call(
        kern,
        grid=(n // tm,),
        in_specs=[
            _row_spec(tm, SSD_CONV_DIM),
            pl.BlockSpec((HALO, SSD_CONV_DIM), lambda i: (jnp.maximum(i * hb - 1, 0), 0)),
            pl.BlockSpec((HALO, SSD_CONV_DIM), lambda i: (jnp.minimum((i + 1) * hb, last_hb), 0)),
            _full_spec(conv_w.shape),
            _full_spec((1, SSD_CONV_DIM)),
        ],
        out_specs=_row_spec(tm, SSD_CONV_DIM),
        out_shape=jax.ShapeDtypeStruct((n, SSD_CONV_DIM), BF16),
        scratch_shapes=[pltpu.VMEM((tm + 2 * HALO, cw), F32)],
        compiler_params=_cparams(("parallel",)),
        name="ssd_conv",
    )(xbc, xbc, xbc, conv_w, conv_b.reshape(1, -1))


_GROUP_W = SSD_INNER // SSD_GROUPS
_HEADS_PER_GROUP = SSD_HEADS // SSD_GROUPS
_B_COL = SSD_INNER
_C_COL = SSD_INNER + SSD_GROUPS * SSD_STATE


def _expand_heads(v, expand):
    return _bdot(v.astype(BF16), expand)


def _scan_kernel(xbc_ref, dt_ref, bias_ref, alog_ref, y_ref, state_ref):
    d = pl.program_id(1)
    s = pl.program_id(2)
    t = SSD_CHUNK

    @pl.when(s == 0)
    def _():
        state_ref[...] = jnp.zeros_like(state_ref)

    lane = lax.broadcasted_iota(jnp.int32, (1, LANES), 1)
    a = jnp.where(lane < SSD_HEADS, -jnp.exp(alog_ref[...]), 0.0)
    raw = dt_ref[...] + bias_ref[...]
    dt = jnp.maximum(raw, 0.0) + jnp.log1p(jnp.exp(-jnp.abs(raw)))
    da = dt * a
    row = lax.broadcasted_iota(jnp.int32, (t, t), 0)
    col = lax.broadcasted_iota(jnp.int32, (t, t), 1)
    mask = (row - col) * (1 - 2 * d) >= 0
    cum = jnp.dot(mask.astype(F32), da, preferred_element_type=F32, precision=lax.Precision.HIGHEST)
    total = jnp.sum(da, axis=0, keepdims=True)
    src_t = (cum - jnp.log(dt)).T
    ecum = jnp.exp(cum)
    dtdec = dt * jnp.exp(total - cum)
    cdec = jnp.broadcast_to(jnp.exp(total), (8, LANES))

    erow = lax.broadcasted_iota(jnp.int32, (LANES, SSD_INNER), 0)
    ecol = lax.broadcasted_iota(jnp.int32, (LANES, SSD_INNER), 1)
    expand = (jnp.right_shift(ecol, SSD_HEAD_DIM.bit_length() - 1) == erow).astype(BF16)
    ex = _expand_heads(jnp.concatenate([dtdec, ecum, cdec], axis=0), expand)
    w_state = ex[0:t]
    w_off = ex[t:2 * t]
    w_carry = ex[2 * t:2 * t + 1]

    lane_t = lax.broadcasted_iota(jnp.int32, (t, LANES), 1)
    neg_inf = jnp.full((t, t), -jnp.inf, F32)
    zero_x = jnp.zeros((t, LANES), BF16)
    for g in range(SSD_GROUPS):
        gs = slice(g * _GROUP_W, (g + 1) * _GROUP_W)
        b_g = xbc_ref[:, _B_COL + g * SSD_STATE:_B_COL + (g + 1) * SSD_STATE]
        c_g = xbc_ref[:, _C_COL + g * SSD_STATE:_C_COL + (g + 1) * SSD_STATE]
        cb = _bdot_nt(c_g, b_g)
        st = state_ref[g]
        y_off = _bdot(c_g, st.astype(BF16)) * w_off[:, gs]
        for pr in range(_HEADS_PER_GROUP // 2):
            xs = slice(g * _GROUP_W + pr * LANES, g * _GROUP_W + (pr + 1) * LANES)
            x_pair = xbc_ref[:, xs]
            ms = []
            for e in range(2):
                hd = g * _HEADS_PER_GROUP + 2 * pr + e
                seg = cum[:, hd:hd + 1] - src_t[hd:hd + 1, :]
                ms.append((cb * jnp.exp(jnp.where(mask, seg, neg_inf))).astype(BF16))
            x_blk = jnp.concatenate([jnp.where(lane_t < SSD_HEAD_DIM, x_pair, zero_x),
                                     jnp.where(lane_t >= SSD_HEAD_DIM, x_pair, zero_x)], axis=0)
            y_pair = _bdot(jnp.concatenate(ms, axis=1), x_blk)
            y_ref[:, xs] = y_pair + y_off[:, pr * LANES:(pr + 1) * LANES]
        xw = (xbc_ref[:, gs].astype(F32) * w_state[:, gs]).astype(BF16)
        b_t = b_g.astype(F32).T.astype(BF16)
        state_ref[g] = st * w_carry[:, gs] + _bdot(b_t, xw)


def _scan_call(geom, xbc, dt, dt_bias, a_log):
    t = SSD_CHUNK
    ncc = geom.ctx // t
    nlc = geom.seq // t
    ctx0 = geom.n_lat // t

    def row_block(b, d, s):
        jc = s + d * (ncc - 1 - 2 * s)
        sl = s - ncc
        jl = sl + d * (nlc - 1 - 2 * sl)
        return jnp.where(s < ncc, ctx0 + b * ncc + jc, b * nlc + jl)

    def pad_lanes(v):
        return jnp.pad(v.astype(F32), ((0, 0), (0, LANES - v.shape[-1]))).reshape(2, 1, LANES)

    return pl.pallas_call(
        _scan_kernel,
        grid=(geom.batch, 2, ncc + nlc),
        in_specs=[
            pl.BlockSpec((t, SSD_CONV_DIM), lambda b, d, s: (row_block(b, d, s), 0)),
            pl.BlockSpec((None, t, LANES), lambda b, d, s: (d, row_block(b, d, s), 0)),
            pl.BlockSpec((None, 1, LANES), lambda b, d, s: (d, 0, 0)),
            pl.BlockSpec((None, 1, LANES), lambda b, d, s: (d, 0, 0)),
        ],
        out_specs=pl.BlockSpec((None, t, SSD_INNER), lambda b, d, s: (d, row_block(b, d, s), 0)),
        out_shape=jax.ShapeDtypeStruct((2, geom.n_tok, SSD_INNER), F32),
        scratch_shapes=[pltpu.VMEM((SSD_GROUPS, SSD_STATE, _GROUP_W), F32)],
        compiler_params=_cparams(("parallel", "parallel", "arbitrary")),
        name="ssd_scan",
    )(xbc, dt, pad_lanes(dt_bias), pad_lanes(a_log))


def _ssd_out_kernel(yf_ref, yb_ref, xbc_ref, z_ref, dskip_ref, ng_ref, h_ref, mod_ref, w_ref, g_ref, b_ref, o_ref):
    dsk = dskip_ref[0:1, :] + dskip_ref[1:2, :]
    y = None
    for g in range(SSD_GROUPS):
        gs = slice(g * _GROUP_W, (g + 1) * _GROUP_W)
        z = z_ref[:, gs].astype(F32)
        gy = (yf_ref[:, gs] + yb_ref[:, gs] + dsk[:, gs] * xbc_ref[:, gs].astype(F32)) * (z * _sigmoid(z))
        gy = _rms_rows(gy, ng_ref[:, gs]).astype(BF16)
        part = _bdot(gy, w_ref[gs, :])
        y = part if y is None else y + part
    r = mod_ref[5:6, :] * y
    o_ref[...] = _layer_norm_rows(DEEPNORM_ALPHA * h_ref[...] + r, g_ref[...], b_ref[...])


def _ssd_out_call(geom, y, xbc, z, dskip, norm_g, h, mods, layer, w_out, ln_g, ln_b, n_rows, tm):
    return pl.pallas_call(
        _ssd_out_kernel,
        grid=(n_rows // tm,),
        in_specs=[
            pl.BlockSpec((None, tm, SSD_INNER), lambda i: (0, i, 0)),
            pl.BlockSpec((None, tm, SSD_INNER), lambda i: (1, i, 0)),
            _row_spec(tm, SSD_INNER),
            _row_spec(tm, SSD_INNER),
            _full_spec(dskip.shape),
            _full_spec((1, SSD_INNER)),
            _row_spec(tm, D_MODEL),
            geom.mod_spec(layer, tm),
            _full_spec(w_out.shape),
            _full_spec((1, D_MODEL)),
            _full_spec((1, D_MODEL)),
        ],
        out_specs=_row_spec(tm, D_MODEL),
        out_shape=jax.ShapeDtypeStruct((n_rows, D_MODEL), F32),
        compiler_params=_cparams(("parallel",)),
        name="ssd_out",
    )(y, y, xbc, z, dskip, norm_g.reshape(1, -1), h, mods, w_out, ln_g.reshape(1, -1), ln_b.reshape(1, -1))


def _attn_weights(w_in, w_uq, w_ukv):
    d = w_in.shape[0]
    kr_end = MLA_Q_RANK + MLA_KV_RANK + MLA_ROPE
    dk_lo = kr_end + DIFF_WIDTH
    dv_lo = dk_lo + DIFF_WIDTH
    w_row = jnp.concatenate([w_in[:, :kr_end], jnp.zeros((d, HEAD_BLOCK - MLA_ROPE), w_in.dtype),
                             w_in[:, dk_lo:dv_lo]], axis=1).astype(BF16)
    w_col = jnp.concatenate([w_in[:, kr_end:dk_lo], w_in[:, dv_lo:]], axis=1).T.astype(BF16)
    pad_q = HEAD_BLOCK - MLA_QK_DIM
    w_uq_p = jnp.pad(w_uq.reshape(MLA_Q_RANK, MLA_HEADS, MLA_QK_DIM), ((0, 0), (0, 0), (0, pad_q)))
    w_uqt = w_uq_p.reshape(MLA_Q_RANK, _QA_W).T.astype(BF16)
    kv = w_ukv.reshape(MLA_KV_RANK, MLA_HEADS, MLA_NOPE + MLA_V)
    w_uk = jnp.pad(kv[:, :, :MLA_NOPE], ((0, 0), (0, 0), (0, HEAD_BLOCK - MLA_NOPE)))
    w_uk = w_uk.reshape(MLA_KV_RANK, _QA_W).astype(BF16)
    w_uvt = kv[:, :, MLA_NOPE:].reshape(MLA_KV_RANK, _VA_W).T.astype(BF16)
    src = jnp.arange(HEAD_BLOCK)[:, None]
    dst = jnp.arange(_QA_W)[None, :]
    place = ((dst % HEAD_BLOCK == src + MLA_NOPE) & (src < MLA_ROPE)).astype(BF16)
    return w_row, w_col, w_uqt, w_uk, w_uvt, place


def _ssd_weights(w_in):
    d = w_in.shape[0]
    z = jnp.zeros((d, LANES - SSD_HEADS), w_in.dtype)
    return jnp.concatenate([w_in[:, :_S_DT], w_in[:, _S_DT:_S_DT + SSD_HEADS], z, w_in[:, _S_DT + SSD_HEADS:], z],
                           axis=1).astype(BF16)


def _lambda_init_for(layer):
    return 0.8 - 0.6 * math.exp(-0.3 * layer)


def _pick_tile(seq, ctx, n_ctx, want):
    tm = want
    while seq % tm or n_ctx % tm:
        tm //= 2
    return tm


def kernel(x, c, ctx, c_ctx, ada_w, ada_b, ln_g, ln_b, ffn1_w_gu, ffn1_w_down, ffn2_w_gu, ffn2_w_down, attn_w_in,
           mla_q_norm_g, mla_w_uq, mla_kv_norm_g, mla_w_ukv, diff_lam_q1, diff_lam_k1, diff_lam_q2, diff_lam_k2,
           diff_subln_g, attn_w_out, ssd_w_in, ssd_conv_w, ssd_conv_b, ssd_a_log, ssd_dt_bias, ssd_d, ssd_norm_g,
           ssd_w_out):
    batch, seq, d = x.shape
    n_ctx_tok = ctx.shape[1]
    geom = _Geom(batch, seq, n_ctx_tok)
    assert d == D_MODEL and batch + 1 <= MOD_ROWS
    assert seq % GRID_W == 0 and seq % SSD_CHUNK == 0 and n_ctx_tok % SSD_CHUNK == 0
    tq = n_ctx_tok
    assert seq % tq == 0
    tm_ffn = _pick_tile(seq, n_ctx_tok, geom.n_ctx, 512)
    tm_proj = _pick_tile(seq, n_ctx_tok, geom.n_ctx, 256)
    tm_conv = n_ctx_tok
    fc = 256

    cond = jnp.concatenate([c, c_ctx[None, :], jnp.zeros((MOD_ROWS - batch - 1, d), F32)], axis=0)
    mods = _ada_call(cond, ada_w, ada_b).reshape(DEPTH, MOD_ROWS, N_MOD, d)
    rtab, ttab = _rope_tables(seq, tm_proj)

    h = jnp.concatenate([x.reshape(batch * seq, d), ctx.reshape(batch * n_ctx_tok, d)], axis=0)
    for l in range(DEPTH):
        last = l == DEPTH - 1
        h = _ffn_call(geom, h, mods, l, 0, ffn1_w_gu[l].astype(BF16), ffn1_w_down[l].astype(BF16),
                      ln_g[l, 0], ln_b[l, 0], geom.n_tok, tm_ffn, fc)
        n_out = geom.n_lat if last else geom.n_tok
        if l % 2 == 0:
            a = l // 2
            wts = _attn_weights(attn_w_in[a], mla_w_uq[a], mla_w_ukv[a])
            qa, ka, va, dq, dk, dv = _attn_in_call(geom, h, mods, l, rtab, ttab, wts, mla_q_norm_g[a].reshape(1, -1),
                                                   mla_kv_norm_g[a].reshape(1, -1), tm_proj)
            lam_vecs = jnp.stack([diff_lam_q1[a], diff_lam_k1[a], diff_lam_q2[a], diff_lam_k2[a]])
            lat, ctx_heads = _attention(geom, qa, ka, va, dq, dk, dv, lam_vecs, diff_subln_g[a].reshape(1, -1),
                                        _lambda_init_for(l), not last, tq)
            h = _attn_out_call(geom, lat, ctx_heads, h, mods, l, attn_w_out[a].astype(BF16), ln_g[l, 1],
                               ln_b[l, 1], tm_proj)
        else:
            s = l // 2
            z, xbc_raw, dt = _ssd_in_call(geom, h, mods, l, _ssd_weights(ssd_w_in[s]), tm_proj)
            xbc = _conv_call(geom, xbc_raw, ssd_conv_w[s], ssd_conv_b[s], tm_conv)
            y = _scan_call(geom, xbc, dt, ssd_dt_bias[s], ssd_a_log[s])
            dskip = jnp.repeat(ssd_d[s], SSD_HEAD_DIM, axis=-1)
            h = _ssd_out_call(geom, y, xbc, z, dskip, ssd_norm_g[s], h, mods, l, ssd_w_out[s].astype(BF16),
                              ln_g[l, 1], ln_b[l, 1], n_out, tm_proj)
        h = _ffn_call(geom, h, mods, l, 6, ffn2_w_gu[l].astype(BF16), ffn2_w_down[l].astype(BF16),
                      ln_g[l, 2], ln_b[l, 2], n_out, tm_ffn, fc)
    return h[:geom.n_lat].reshape(batch, seq, d)
```

```python
import functools
import math

import jax
import jax.numpy as jnp
from jax import lax
from jax.experimental import pallas as pl
from jax.experimental.pallas import tpu as pltpu

F32 = jnp.float32
BF16 = jnp.bfloat16

D_MODEL = 1024
DEPTH = 4
GRID_W = 64
N_MOD = 9
FFN_DIM = 2816
MACARON_WEIGHT = 0.5
MLA_HEADS = 8
MLA_Q_RANK = 384
MLA_KV_RANK = 256
MLA_NOPE = 64
MLA_ROPE = 32
MLA_V = 64
MLA_QK_DIM = MLA_NOPE + MLA_ROPE
MLA_SCALE = MLA_QK_DIM ** -0.5
DIFF_HEADS = 4
DIFF_HEAD_DIM = 64
DIFF_WIDTH = DIFF_HEADS * 2 * DIFF_HEAD_DIM
DIFF_SCALE = DIFF_HEAD_DIM ** -0.5
ROPE_BASE = 10000.0
SSD_INNER = 2 * D_MODEL
SSD_HEAD_DIM = 64
SSD_HEADS = SSD_INNER // SSD_HEAD_DIM
SSD_GROUPS = 4
SSD_STATE = 128
SSD_CONV = 5
SSD_CHUNK = 128
SSD_CONV_DIM = SSD_INNER + 2 * SSD_GROUPS * SSD_STATE
DEEPNORM_ALPHA = (2.0 * DEPTH) ** 0.25
LN_EPS = 1e-6
RMS_EPS = 1e-6

LANES = 128
V7X_VMEM_LIMIT = 56 * 1024 * 1024
HALO = 16

HEAD_BLOCK = LANES
MOD_ROWS = 24


def _cparams(sem):
    return pltpu.CompilerParams(dimension_semantics=sem, vmem_limit_bytes=V7X_VMEM_LIMIT)


def _sigmoid(v):
    return 1.0 / (1.0 + jnp.exp(-v))


def _layer_norm_rows(v, g, b):
    mu = jnp.mean(v, axis=-1, keepdims=True)
    c = v - mu
    var = jnp.mean(c * c, axis=-1, keepdims=True)
    return c * lax.rsqrt(var + LN_EPS) * g + b


def _rms_rows(v, g):
    return v * lax.rsqrt(jnp.mean(v * v, axis=-1, keepdims=True) + RMS_EPS) * g


def _bdot(a, b):
    return jnp.dot(a, b, preferred_element_type=F32)


def _bdot_nt(a, b):
    return lax.dot_general(a, b, (((1,), (1,)), ((), ())), preferred_element_type=F32)


def _ada_kernel(c_ref, w_ref, b_ref, o_ref):
    c = c_ref[...]
    s = (c * _sigmoid(c)).astype(BF16)
    o_ref[...] = _bdot(s, w_ref[...].astype(BF16)) + b_ref[...]


def _ada_call(cond, ada_w, ada_b):
    depth, d, n = ada_w.shape
    tn = n // 8
    return pl.pallas_call(
        _ada_kernel,
        grid=(depth, n // tn),
        in_specs=[
            pl.BlockSpec((MOD_ROWS, d), lambda l, j: (0, 0)),
            pl.BlockSpec((None, d, tn), lambda l, j: (l, 0, j)),
            pl.BlockSpec((None, 1, tn), lambda l, j: (l, 0, j)),
        ],
        out_specs=pl.BlockSpec((None, MOD_ROWS, tn), lambda l, j: (l, 0, j)),
        out_shape=jax.ShapeDtypeStruct((depth, MOD_ROWS, n), F32),
        compiler_params=_cparams(("parallel", "parallel")),
        name="ada_mod",
    )(cond, ada_w, ada_b.reshape(depth, 1, n))


class _Geom:
    def __init__(self, batch, seq, ctx):
        self.batch, self.seq, self.ctx = batch, seq, ctx
        self.n_lat = batch * seq
        self.n_ctx = batch * ctx
        self.n_tok = self.n_lat + self.n_ctx

    def mod_spec(self, layer, tm):
        n_lat_tiles = self.n_lat // tm
        tiles_per_batch = self.seq // tm
        batch = self.batch

        def index(i):
            return (layer, jnp.where(i < n_lat_tiles, i // tiles_per_batch, batch), 0, 0)

        return pl.BlockSpec((None, None, N_MOD, D_MODEL), index)


def _row_spec(tm, width):
    return pl.BlockSpec((tm, width), lambda i: (i, 0))


def _full_spec(shape):
    nd = len(shape)
    return pl.BlockSpec(shape, lambda i: (0,) * nd)


def _ffn_kernel(h_ref, mod_ref, wgu_ref, wd_ref, g_ref, b_ref, o_ref, acc_ref, *, k0, fc):
    h = h_ref[...]
    shift = mod_ref[k0:k0 + 1, :]
    scale = mod_ref[k0 + 1:k0 + 2, :]
    gate = mod_ref[k0 + 2:k0 + 3, :]
    t = (h * (1.0 + scale) + shift).astype(BF16)
    for j in range(FFN_DIM // fc):
        gj = _bdot(t, wgu_ref[:, j * fc:(j + 1) * fc])
        uj = _bdot(t, wgu_ref[:, FFN_DIM + j * fc:FFN_DIM + (j + 1) * fc])
        a = (gj * _sigmoid(gj) * uj).astype(BF16)
        y = _bdot(a, wd_ref[j * fc:(j + 1) * fc, :])
        if j == 0:
            acc_ref[...] = y
        else:
            acc_ref[...] += y
    r = (MACARON_WEIGHT * gate) * acc_ref[...]
    o_ref[...] = _layer_norm_rows(DEEPNORM_ALPHA * h + r, g_ref[...], b_ref[...])


def _ffn_call(geom, h, mods, layer, k0, w_gu, w_down, ln_g, ln_b, n_rows, tm, fc):
    kern = functools.partial(_ffn_kernel, k0=k0, fc=fc)
    return pl.pallas_call(
        kern,
        grid=(n_rows // tm,),
        in_specs=[
            _row_spec(tm, D_MODEL),
            geom.mod_spec(layer, tm),
            _full_spec(w_gu.shape),
            _full_spec(w_down.shape),
            _full_spec((1, D_MODEL)),
            _full_spec((1, D_MODEL)),
        ],
        out_specs=_row_spec(tm, D_MODEL),
        out_shape=jax.ShapeDtypeStruct((n_rows, D_MODEL), F32),
        scratch_shapes=[pltpu.VMEM((tm, D_MODEL), F32)],
        compiler_params=_cparams(("parallel",)),
        name="ffn",
    )(h, mods, w_gu, w_down, ln_g.reshape(1, -1), ln_b.reshape(1, -1))


_QA_W = MLA_HEADS * HEAD_BLOCK
_VA_W = MLA_HEADS * MLA_V
_R_CKV = MLA_Q_RANK
_R_KR = MLA_Q_RANK + MLA_KV_RANK
_R_DK = _R_KR + HEAD_BLOCK
ATTN_ROW_COLS = _R_DK + DIFF_WIDTH
_MLA_HALF = MLA_ROPE // 2
_DIFF_HALF = DIFF_HEAD_DIM // 2
_T_SIN_M = _MLA_HALF
_T_COS_D = 2 * _MLA_HALF
_T_SIN_D = _T_COS_D + _DIFF_HALF
ROPE_T_ROWS = _T_SIN_D + _DIFF_HALF
ATT_KC = 256
LOG2E = math.log2(math.e)
MLA_QSCALE = MLA_SCALE * LOG2E
DIFF_QSCALE = DIFF_SCALE * LOG2E


def _rope_block(v, tab_ref, t0, shift):
    left = pltpu.roll(v, LANES - shift, axis=1)
    right = pltpu.roll(v, shift, axis=1)
    return v * tab_ref[t0] + left * tab_ref[t0 + 1] + right * tab_ref[t0 + 2]


def _rope_rows(dst_ref, src, r0, half, cos, sin, scale):
    t1 = src[r0:r0 + half]
    t2 = src[r0 + half:r0 + 2 * half]
    dst_ref[r0:r0 + half, :] = ((t1 * cos - t2 * sin) * scale).astype(dst_ref.dtype)
    dst_ref[r0 + half:r0 + 2 * half, :] = ((t1 * sin + t2 * cos) * scale).astype(dst_ref.dtype)


def _attn_in_kernel(h_ref, mod_ref, rtab_ref, ttab_ref, wrow_ref, wcol_ref, qg_ref, wuqt_ref, kvg_ref, wuk_ref,
                    wuvt_ref, place_ref, qat_ref, ka_ref, vat_ref, dqt_ref, dk_ref, dvt_ref):
    h = h_ref[...]
    u = (h * (1.0 + mod_ref[4:5, :]) + mod_ref[3:4, :]).astype(BF16)
    proj = _bdot(u, wrow_ref[...])
    projt = _bdot_nt(wcol_ref[...], u)

    cqn = _rms_rows(proj[:, :_R_CKV], qg_ref[...]).astype(BF16)
    qt = _bdot_nt(wuqt_ref[...], cqn)
    cos_m = ttab_ref[0:_T_SIN_M, :]
    sin_m = ttab_ref[_T_SIN_M:_T_COS_D, :]
    for hd in range(MLA_HEADS):
        r0 = hd * HEAD_BLOCK
        qat_ref[r0:r0 + MLA_NOPE, :] = (qt[r0:r0 + MLA_NOPE] * MLA_QSCALE).astype(BF16)
        _rope_rows(qat_ref, qt, r0 + MLA_NOPE, _MLA_HALF, cos_m, sin_m, MLA_QSCALE)
        qat_ref[r0 + MLA_QK_DIM:r0 + HEAD_BLOCK, :] = jnp.zeros((HEAD_BLOCK - MLA_QK_DIM, qt.shape[1]), BF16)

    ckvn = _rms_rows(proj[:, _R_CKV:_R_KR], kvg_ref[...]).astype(BF16)
    kr = _rope_block(proj[:, _R_KR:_R_DK], rtab_ref, 0, _MLA_HALF).astype(BF16)
    ka_ref[...] = (_bdot(ckvn, wuk_ref[...]) + _bdot(kr, place_ref[...])).astype(BF16)
    vat_ref[...] = _bdot_nt(wuvt_ref[...], ckvn).astype(BF16)

    cos_d = ttab_ref[_T_COS_D:_T_SIN_D, :]
    sin_d = ttab_ref[_T_SIN_D:ROPE_T_ROWS, :]
    for sub in range(2 * DIFF_HEADS):
        _rope_rows(dqt_ref, projt, sub * DIFF_HEAD_DIM, _DIFF_HALF, cos_d, sin_d, DIFF_QSCALE)
    dvt_ref[...] = projt[DIFF_WIDTH:].astype(BF16)
    for hd in range(DIFF_HEADS):
        sk = slice(_R_DK + hd * HEAD_BLOCK, _R_DK + (hd + 1) * HEAD_BLOCK)
        so = slice(hd * HEAD_BLOCK, (hd + 1) * HEAD_BLOCK)
        dk_ref[:, so] = _rope_block(proj[:, sk], rtab_ref, 3, _DIFF_HALF).astype(BF16)


def _attn_in_call(geom, h, mods, layer, rtab, ttab, wts, q_g, kv_g, tm):
    n_lat_tiles = geom.n_lat // tm
    tiles_per_seq = geom.seq // tm

    def pos_block(i):
        return jnp.where(i < n_lat_tiles, i % tiles_per_seq, tiles_per_seq)

    n = geom.n_tok
    w_row, w_col, w_uqt, w_uk, w_uvt, place = wts

    def tok_major(width):
        return _row_spec(tm, width), jax.ShapeDtypeStruct((n, width), BF16)

    def chan_major(rows):
        return pl.BlockSpec((rows, tm), lambda i: (0, i)), jax.ShapeDtypeStruct((rows, n), BF16)

    def chan_major_chunked(rows):
        return (pl.BlockSpec((None, rows, tm), lambda i: (i, 0, 0)),
                jax.ShapeDtypeStruct((n // tm, rows, tm), BF16))

    assert tm == ATT_KC
    outs = [chan_major(_QA_W), tok_major(_QA_W), chan_major_chunked(_VA_W),
            chan_major(DIFF_WIDTH), tok_major(DIFF_WIDTH), chan_major_chunked(DIFF_WIDTH)]
    return pl.pallas_call(
        _attn_in_kernel,
        grid=(n // tm,),
        in_specs=[
            _row_spec(tm, D_MODEL),
            geom.mod_spec(layer, tm),
            pl.BlockSpec((6, tm, LANES), lambda i: (0, pos_block(i), 0)),
            pl.BlockSpec((ROPE_T_ROWS, tm), lambda i: (0, pos_block(i))),
            _full_spec(w_row.shape),
            _full_spec(w_col.shape),
            _full_spec(q_g.shape),
            _full_spec(w_uqt.shape),
            _full_spec(kv_g.shape),
            _full_spec(w_uk.shape),
            _full_spec(w_uvt.shape),
            _full_spec(place.shape),
        ],
        out_specs=[o[0] for o in outs],
        out_shape=[o[1] for o in outs],
        compiler_params=_cparams(("parallel",)),
        name="attn_in",
    )(h, mods, rtab, ttab, w_row, w_col, q_g, w_uqt, kv_g, w_uk, w_uvt, place)


def _rope_tables(seq, tm):
    rows = seq // GRID_W
    row = jnp.repeat(jnp.arange(rows, dtype=F32), GRID_W)
    col = jnp.tile(jnp.arange(GRID_W, dtype=F32), rows)

    def angles(rot_dim):
        n_freq = rot_dim // 4
        inv = ROPE_BASE ** (-jnp.arange(n_freq, dtype=F32) / n_freq)
        return jnp.concatenate([row[:, None] * inv, col[:, None] * inv], axis=-1)

    def build(ang, first_lo, half, live):
        lane = jnp.arange(LANES)
        cos, sin = jnp.cos(ang), jnp.sin(ang)
        zeros = jnp.zeros((seq, LANES), F32)
        a = jnp.where(lane < live, 1.0, 0.0)[None, :] + zeros
        bm, cm = zeros, zeros
        for lo in first_lo:
            a = a.at[:, lo:lo + half].set(cos).at[:, lo + half:lo + 2 * half].set(cos)
            bm = bm.at[:, lo:lo + half].set(-sin)
            cm = cm.at[:, lo + half:lo + 2 * half].set(sin)
        ident = jnp.where(lane < live, 1.0, 0.0)[None, :] + jnp.zeros((tm, LANES), F32)
        z = jnp.zeros((tm, LANES), F32)
        return [jnp.concatenate([a, ident]), jnp.concatenate([bm, z]), jnp.concatenate([cm, z])]

    ang_m = angles(MLA_ROPE)
    ang_d = angles(DIFF_HEAD_DIM)
    rtab = jnp.stack(build(ang_m, (0,), _MLA_HALF, MLA_ROPE)
                     + build(ang_d, (0, DIFF_HEAD_DIM), _DIFF_HALF, LANES))

    def chan(ang, fn, fill):
        return jnp.concatenate([fn(ang).T, jnp.full((ang.shape[1], tm), fill, F32)], axis=1)

    ttab = jnp.concatenate([chan(ang_m, jnp.cos, 1.0), chan(ang_m, jnp.sin, 0.0),
                            chan(ang_d, jnp.cos, 1.0), chan(ang_d, jnp.sin, 0.0)], axis=0)
    return rtab, ttab


def _softmax_pv_t(qt, keys, vals_t):
    s = [_bdot(k, qt) for k in keys]
    m = s[0].max(axis=0, keepdims=True)
    for si in s[1:]:
        m = jnp.maximum(m, si.max(axis=0, keepdims=True))
    num, den = None, None
    for si, vt in zip(s, vals_t):
        p = jnp.exp2(si - m)
        d = p.sum(axis=0, keepdims=True)
        o = _bdot(vt, p.astype(BF16))
        num = o if num is None else num + o
        den = d if den is None else den + d
    return num / den


def _fold_rows(v, op):
    parts = [v[r:r + 8] for r in range(0, v.shape[0], 8)]
    while len(parts) > 1:
        parts = [op(parts[i], parts[i + 1]) for i in range(0, len(parts) - 1, 2)] + (
            [parts[-1]] if len(parts) % 2 else [])
    return parts[0]


def _attn_pipelined(qi, n_q, q_cur, q_nxt, k_lanes, kx_ref, kc_ref, vx_ref, vc_ref, s_refs, m_ref, acc_ref, o_ref,
                    combine):
    n_maps = len(q_cur)
    tq = q_cur[0].shape[1]
    chunks = ([(kc_ref, vc_ref, c) for c in range(kc_ref.shape[0])]
              + [(kx_ref, vx_ref, c) for c in range(kx_ref.shape[0])])
    neg = jnp.full((8, tq), -jnp.inf, F32)
    zero = jnp.zeros((8, tq), F32)

    def score_chunk(dst_ref, ci, q_list, mrun):
        k_ref, _, c = chunks[ci]
        out = []
        for j in range(n_maps):
            s = _bdot(k_ref[c, :, k_lanes[j]], q_list[j])
            dst_ref[j, ci] = s
            out.append(jnp.maximum(mrun[j], _fold_rows(s, jnp.maximum)))
        return out

    def exp_chunk(src_ref, ci, m_cur, lrun):
        _, v_ref, c = chunks[ci]
        vt = v_ref[c]
        out = []
        for j in range(n_maps):
            p = jnp.exp2(src_ref[j, ci] - m_cur[j])
            out.append(lrun[j] + _fold_rows(p, jnp.add))
            o = _bdot(vt, p.astype(BF16))
            if ci == 0:
                acc_ref[j] = o
            else:
                acc_ref[j] += o
        return out

    @pl.when(qi == 0)
    def _():
        mrun = [neg] * n_maps
        for ci in range(len(chunks)):
            mrun = score_chunk(s_refs[0], ci, q_cur, mrun)
        for j in range(n_maps):
            m_ref[0, j] = mrun[j]

    def step(slot, with_next):
        cur_ref, nxt_ref = s_refs[slot], s_refs[1 - slot]
        m_cur = [jnp.max(m_ref[slot, j], axis=0, keepdims=True) for j in range(n_maps)]
        lrun, mrun = [zero] * n_maps, [neg] * n_maps
        for ci in range(len(chunks)):
            if with_next:
                mrun = score_chunk(nxt_ref, ci, q_nxt, mrun)
            lrun = exp_chunk(cur_ref, ci, m_cur, lrun)
        if with_next:
            for j in range(n_maps):
                m_ref[1 - slot, j] = mrun[j]
        outs = [acc_ref[j] / jnp.sum(lrun[j], axis=0, keepdims=True) for j in range(n_maps)]
        o_ref[...] = combine(outs).astype(o_ref.dtype)

    for slot in (0, 1):
        pl.when(jnp.logical_and(qi % 2 == slot, qi < n_q - 1))(functools.partial(step, slot, True))
    pl.when(qi == n_q - 1)(functools.partial(step, (n_q - 1) % 2, False))


def _mla_maps(q_ref):
    return [q_ref[:HEAD_BLOCK, :], q_ref[HEAD_BLOCK:, :]]


_MLA_K_LANES = (slice(0, HEAD_BLOCK), slice(HEAD_BLOCK, 2 * HEAD_BLOCK))
_DIFF_K_LANES = (slice(0, HEAD_BLOCK), slice(0, HEAD_BLOCK))


def _mla_combine(outs):
    row = lax.broadcasted_iota(jnp.int32, outs[0].shape, 0)
    return jnp.where(row < MLA_V, outs[0], outs[1]).T


def _diff_maps(q_ref):
    q = q_ref[...]
    row = lax.broadcasted_iota(jnp.int32, q.shape, 0)
    zero = jnp.zeros_like(q)
    return [jnp.where(row < DIFF_HEAD_DIM, q, zero), jnp.where(row >= DIFF_HEAD_DIM, q, zero)]


def _diff_lambda(lam_ref, lam_init):
    lv = lam_ref[...]
    return (jnp.exp(jnp.sum(lv[0:1] * lv[1:2], axis=-1, keepdims=True))
            - jnp.exp(jnp.sum(lv[2:3] * lv[3:4], axis=-1, keepdims=True)) + lam_init)


def _diff_combine(outs, lam, lam_init, g):
    return _rms_rows((outs[0] - lam * outs[1]).T, g) * (1.0 - lam_init)


def _mla_lat_kernel(qc_ref, qn_ref, kx_ref, kc_ref, vx_ref, vc_ref, o_ref, s0_ref, s1_ref, m_ref, acc_ref, *, n_q):
    _attn_pipelined(pl.program_id(2), n_q, _mla_maps(qc_ref), _mla_maps(qn_ref), _MLA_K_LANES, kx_ref, kc_ref,
                    vx_ref, vc_ref, (s0_ref, s1_ref), m_ref, acc_ref, o_ref, _mla_combine)


def _diff_lat_kernel(lam_ref, g_ref, qc_ref, qn_ref, kx_ref, kc_ref, vx_ref, vc_ref, o_ref, s0_ref, s1_ref, m_ref,
                     acc_ref, *, n_q, lam_init):
    combine = functools.partial(_diff_combine, lam=_diff_lambda(lam_ref, lam_init), lam_init=lam_init, g=g_ref[...])
    _attn_pipelined(pl.program_id(2), n_q, _diff_maps(qc_ref), _diff_maps(qn_ref), _DIFF_K_LANES, kx_ref, kc_ref,
                    vx_ref, vc_ref, (s0_ref, s1_ref), m_ref, acc_ref, o_ref, combine)


def _ctx_attention(q_maps, k_lanes, kc_ref, vc_ref):
    n_cc = kc_ref.shape[0]
    vals = [vc_ref[cc] for cc in range(n_cc)]
    return [_softmax_pv_t(q, [kc_ref[cc, :, lanes] for cc in range(n_cc)], vals)
            for q, lanes in zip(q_maps, k_lanes)]


def _mla_ctx_kernel(q_ref, kc_ref, vc_ref, o_ref):
    o_ref[...] = _mla_combine(_ctx_attention(_mla_maps(q_ref), _MLA_K_LANES, kc_ref, vc_ref)).astype(o_ref.dtype)


def _diff_ctx_kernel(lam_ref, g_ref, q_ref, kc_ref, vc_ref, o_ref, *, lam_init):
    outs = _ctx_attention(_diff_maps(q_ref), _DIFF_K_LANES, kc_ref, vc_ref)
    o_ref[...] = _diff_combine(outs, _diff_lambda(lam_ref, lam_init), lam_init, g_ref[...]).astype(o_ref.dtype)


def _attn_lat_call(geom, kern, name, small, qt, k3, vt3, qk_w, n_groups, tq):
    n_q = geom.seq // tq
    n_lc = geom.seq // ATT_KC
    n_cc = geom.ctx // ATT_KC
    ctx_c0 = geom.n_lat // geom.ctx
    small_specs = [pl.BlockSpec(a.shape, lambda b, p, qi: (0, 0)) for a in small]
    specs = small_specs + [
        pl.BlockSpec((qk_w, tq), lambda b, p, qi: (p, b * n_q + qi)),
        pl.BlockSpec((qk_w, tq), lambda b, p, qi: (p, b * n_q + jnp.minimum(qi + 1, n_q - 1))),
        pl.BlockSpec((n_lc, ATT_KC, qk_w), lambda b, p, qi: (b, 0, p)),
        pl.BlockSpec((n_cc, ATT_KC, qk_w), lambda b, p, qi: (ctx_c0 + b, 0, p)),
        pl.BlockSpec((n_lc, HEAD_BLOCK, ATT_KC), lambda b, p, qi: (b, p, 0)),
        pl.BlockSpec((n_cc, HEAD_BLOCK, ATT_KC), lambda b, p, qi: (ctx_c0 + b, p, 0)),
    ]
    return pl.pallas_call(
        functools.partial(kern, n_q=n_q),
        grid=(geom.batch, n_groups, n_q),
        in_specs=specs,
        out_specs=pl.BlockSpec((tq, HEAD_BLOCK), lambda b, p, qi: (b * n_q + qi, p)),
        out_shape=jax.ShapeDtypeStruct((geom.n_lat, n_groups * HEAD_BLOCK), BF16),
        scratch_shapes=[pltpu.VMEM((2, n_cc + n_lc, ATT_KC, tq), F32),
                        pltpu.VMEM((2, n_cc + n_lc, ATT_KC, tq), F32),
                        pltpu.VMEM((2, 2, 8, tq), F32),
                        pltpu.VMEM((2, HEAD_BLOCK, tq), F32)],
        compiler_params=_cparams(("parallel", "parallel", "arbitrary")),
        name=name,
    )(*small, qt, qt, k3, k3, vt3, vt3)


def _attn_ctx_call(geom, kern, name, small, qt, k3, vt3, qk_w, n_groups):
    n_cc = geom.ctx // ATT_KC
    ctx_c0 = geom.n_lat // geom.ctx
    small_specs = [pl.BlockSpec(a.shape, lambda b, p: (0, 0)) for a in small]
    specs = small_specs + [
        pl.BlockSpec((qk_w, geom.ctx), lambda b, p: (p, ctx_c0 + b)),
        pl.BlockSpec((n_cc, ATT_KC, qk_w), lambda b, p: (ctx_c0 + b, 0, p)),
        pl.BlockSpec((n_cc, HEAD_BLOCK, ATT_KC), lambda b, p: (ctx_c0 + b, p, 0)),
    ]
    return pl.pallas_call(
        kern,
        grid=(geom.batch, n_groups),
        in_specs=specs,
        out_specs=pl.BlockSpec((geom.ctx, HEAD_BLOCK), lambda b, p: (b, p)),
        out_shape=jax.ShapeDtypeStruct((geom.n_ctx, n_groups * HEAD_BLOCK), BF16),
        compiler_params=_cparams(("parallel", "parallel")),
        name=name,
    )(*small, qt, k3, vt3)


def _attention(geom, qa, ka, va, dq, dk, dv, lam_vecs, subln_g, lam_init, with_ctx, tq):
    n_chunks = geom.n_tok // ATT_KC
    ka3 = ka.reshape(n_chunks, ATT_KC, _QA_W)
    dk3 = dk.reshape(n_chunks, ATT_KC, DIFF_WIDTH)
    small = [lam_vecs, subln_g]
    lat = (_attn_lat_call(geom, _mla_lat_kernel, "mla_attn", [], qa, ka3, va, 2 * HEAD_BLOCK, MLA_HEADS // 2, tq),
           _attn_lat_call(geom, functools.partial(_diff_lat_kernel, lam_init=lam_init), "diff_attn", small, dq, dk3,
                          dv, HEAD_BLOCK, DIFF_HEADS, tq))
    if not with_ctx:
        return lat, None
    ctx = (_attn_ctx_call(geom, _mla_ctx_kernel, "mla_attn_ctx", [], qa, ka3, va, 2 * HEAD_BLOCK, MLA_HEADS // 2),
           _attn_ctx_call(geom, functools.partial(_diff_ctx_kernel, lam_init=lam_init), "diff_attn_ctx", small, dq,
                          dk3, dv, HEAD_BLOCK, DIFF_HEADS))
    return lat, ctx


def _attn_out_kernel(*refs, n_lat_tiles, with_ctx):
    if with_ctx:
        oa_ref, od_ref, oac_ref, odc_ref, h_ref, mod_ref, w_ref, g_ref, b_ref, o_ref = refs
        is_lat = pl.program_id(0) < n_lat_tiles
        oa = jnp.where(is_lat, oa_ref[...], oac_ref[...])
        od = jnp.where(is_lat, od_ref[...], odc_ref[...])
    else:
        oa_ref, od_ref, h_ref, mod_ref, w_ref, g_ref, b_ref, o_ref = refs
        oa, od = oa_ref[...], od_ref[...]
    y = _bdot(oa, w_ref[:_VA_W, :]) + _bdot(od, w_ref[_VA_W:, :])
    r = mod_ref[5:6, :] * y
    o_ref[...] = _layer_norm_rows(DEEPNORM_ALPHA * h_ref[...] + r, g_ref[...], b_ref[...])


def _attn_out_call(geom, lat, ctx, h, mods, layer, w_out, ln_g, ln_b, tm):
    n_lat_tiles = geom.n_lat // tm
    with_ctx = ctx is not None
    n_rows = geom.n_tok if with_ctx else geom.n_lat
    head_specs = [pl.BlockSpec((tm, _VA_W), lambda i: (jnp.minimum(i, n_lat_tiles - 1), 0)),
                  pl.BlockSpec((tm, DIFF_WIDTH), lambda i: (jnp.minimum(i, n_lat_tiles - 1), 0))]
    heads = list(lat)
    if with_ctx:
        head_specs += [pl.BlockSpec((tm, _VA_W), lambda i: (jnp.maximum(i - n_lat_tiles, 0), 0)),
                       pl.BlockSpec((tm, DIFF_WIDTH), lambda i: (jnp.maximum(i - n_lat_tiles, 0), 0))]
        heads += list(ctx)
    return pl.pallas_call(
        functools.partial(_attn_out_kernel, n_lat_tiles=n_lat_tiles, with_ctx=with_ctx),
        grid=(n_rows // tm,),
        in_specs=head_specs + [
            _row_spec(tm, D_MODEL),
            geom.mod_spec(layer, tm),
            _full_spec(w_out.shape),
            _full_spec((1, D_MODEL)),
            _full_spec((1, D_MODEL)),
        ],
        out_specs=_row_spec(tm, D_MODEL),
        out_shape=jax.ShapeDtypeStruct((n_rows, D_MODEL), F32),
        compiler_params=_cparams(("parallel",)),
        name="attn_out",
    )(*heads, h, mods, w_out, ln_g.reshape(1, -1), ln_b.reshape(1, -1))


SSD_PROJ_COLS = SSD_INNER + SSD_CONV_DIM + 2 * LANES
_S_XBC = SSD_INNER
_S_DT = SSD_INNER + SSD_CONV_DIM


def _ssd_in_kernel(h_ref, hp_ref, hn_ref, mod_ref, w_ref, cw_ref, cb_ref, z_ref, xbc_ref, dt_ref, pad_ref, *,
                   tm, tiles_per_seq, n_lat_tiles, nc):
    i = pl.program_id(0)
    is_lat = i < n_lat_tiles
    pos = i % tiles_per_seq
    has_prev = jnp.logical_and(is_lat, pos > 0)
    has_next = jnp.logical_and(is_lat, pos < tiles_per_seq - 1)
    scale1 = 1.0 + mod_ref[4:5, :]
    shift = mod_ref[3:4, :]
    u = (h_ref[...] * scale1 + shift).astype(BF16)
    u_ext = jnp.concatenate([(hp_ref[...] * scale1 + shift).astype(BF16), u,
                             (hn_ref[...] * scale1 + shift).astype(BF16)], axis=0)
    for lo in range(0, SSD_INNER, nc):
        z_ref[:, lo:lo + nc] = _bdot(u, w_ref[:, lo:lo + nc]).astype(BF16)
    dt = _bdot(u, w_ref[:, _S_DT:])
    dt_ref[0] = dt[:, :LANES]
    dt_ref[1] = dt[:, LANES:]
    half = SSD_CONV // 2
    for lo in range(0, SSD_CONV_DIM, nc):
        cs = slice(lo, lo + nc)
        r = _bdot(u_ext, w_ref[:, _S_XBC + lo:_S_XBC + lo + nc])
        head, tail = r[:HALO], r[HALO + tm:]
        r = jnp.concatenate([jnp.where(has_prev, head, jnp.zeros_like(head)), r[HALO:HALO + tm],
                             jnp.where(has_next, tail, jnp.zeros_like(tail))], axis=0)
        acc = cb_ref[:, cs] + jnp.zeros((tm, nc), F32)
        for k in range(SSD_CONV):
            sh = r if k == half else pltpu.roll(r, (half - k) % r.shape[0], axis=0)
            acc = acc + sh[HALO:HALO + tm] * cw_ref[k:k + 1, cs]
        xbc_ref[:, cs] = (acc * _sigmoid(acc)).astype(BF16)


def _ssd_in_call(geom, h, mods, layer, w_in, conv_w, conv_b, tm):
    n = geom.n_tok
    nc = 512
    hb = tm // HALO
    last_hb = n // HALO - 1
    kern = functools.partial(_ssd_in_kernel, tm=tm, tiles_per_seq=geom.seq // tm, n_lat_tiles=geom.n_lat // tm, nc=nc)
    return pl.pallas_call(
        kern,
        grid=(n // tm,),
        in_specs=[
            _row_spec(tm, D_MODEL),
            pl.BlockSpec((HALO, D_MODEL), lambda i: (jnp.maximum(i * hb - 1, 0), 0)),
            pl.BlockSpec((HALO, D_MODEL), lambda i: (jnp.minimum((i + 1) * hb, last_hb), 0)),
            geom.mod_spec(layer, tm),
            _full_spec(w_in.shape),
            _full_spec(conv_w.shape),
            _full_spec((1, SSD_CONV_DIM)),
        ],
        out_specs=[_row_spec(tm, SSD_INNER), _row_spec(tm, SSD_CONV_DIM),
                   pl.BlockSpec((2, tm, LANES), lambda i: (0, i, 0))],
        out_shape=[jax.ShapeDtypeStruct((n, SSD_INNER), BF16), jax.ShapeDtypeStruct((n, SSD_CONV_DIM), BF16),
                   jax.ShapeDtypeStruct((2, n, LANES), F32)],
        scratch_shapes=[pltpu.VMEM((tm + 2 * HALO, nc), F32)],
        compiler_params=_cparams(("parallel",)),
        name="ssd_in",
    )(h, h, h, mods, w_in, conv_w, conv_b.reshape(1, -1))


_GROUP_W = SSD_INNER // SSD_GROUPS
_HEADS_PER_GROUP = SSD_HEADS // SSD_GROUPS
_B_COL = SSD_INNER
_C_COL = SSD_INNER + SSD_GROUPS * SSD_STATE


def _scan_kernel(xf_ref, xb_ref, dtf_ref, dtb_ref, bias_ref, alog_ref, expand_ref, yf_ref, yb_ref, state_ref):
    @pl.when(pl.program_id(1) == 0)
    def _():
        state_ref[...] = jnp.zeros_like(state_ref)

    _scan_chunk(0, xf_ref, dtf_ref, bias_ref[0], alog_ref[0], expand_ref, yf_ref, state_ref.at[0])
    _scan_chunk(1, xb_ref, dtb_ref, bias_ref[1], alog_ref[1], expand_ref, yb_ref, state_ref.at[1])


def _scan_chunk(d, xbc_ref, dt_ref, bias, alog, expand_ref, y_ref, state_ref):
    t = SSD_CHUNK
    lane = lax.broadcasted_iota(jnp.int32, (1, LANES), 1)
    a = jnp.where(lane < SSD_HEADS, -jnp.exp(alog), 0.0)
    raw = dt_ref[...] + bias
    e = jnp.exp(-jnp.abs(raw))
    u = 1.0 + e
    um1 = u - 1.0
    dt = jnp.maximum(raw, 0.0) + jnp.where(um1 == 0.0, e, jnp.log(u) * (e / jnp.where(um1 == 0.0, 1.0, um1)))
    da = dt * a
    row = lax.broadcasted_iota(jnp.int32, (t, t), 0)
    col = lax.broadcasted_iota(jnp.int32, (t, t), 1)
    mask = (col <= row) if d == 0 else (col >= row)
    tri = mask.astype(BF16)
    da_hi = da.astype(BF16)
    rem = da - da_hi.astype(F32)
    da_mid = rem.astype(BF16)
    da_lo = (rem - da_mid.astype(F32)).astype(BF16)
    cum = _bdot(tri, da_hi) + _bdot(tri, da_mid) + _bdot(tri, da_lo)
    total = jnp.sum(da, axis=0, keepdims=True)
    src_t = (cum - jnp.log(dt)).T
    ecum = jnp.exp(cum)
    dtdec = dt * jnp.exp(total - cum)
    cdec = jnp.broadcast_to(jnp.exp(total), (8, LANES))

    per_head = jnp.concatenate([dtdec, ecum, cdec], axis=0).astype(BF16)

    lane_t = lax.broadcasted_iota(jnp.int32, (t, LANES), 1)
    mask_bias = jnp.where(mask, 0.0, -jnp.inf)
    zero_x = jnp.zeros((t, LANES), BF16)
    for g in range(SSD_GROUPS):
        gs = slice(g * _GROUP_W, (g + 1) * _GROUP_W)
        ex = _bdot(per_head, expand_ref[:, gs])
        w_state, w_off, w_carry = ex[0:t], ex[t:2 * t], ex[2 * t:2 * t + 1]
        b_g = xbc_ref[:, _B_COL + g * SSD_STATE:_B_COL + (g + 1) * SSD_STATE]
        c_g = xbc_ref[:, _C_COL + g * SSD_STATE:_C_COL + (g + 1) * SSD_STATE]
        cb = _bdot_nt(c_g, b_g)
        st = state_ref[g]
        y_off = _bdot(c_g, st.astype(BF16)) * w_off
        for pr in range(_HEADS_PER_GROUP // 2):
            xs = slice(g * _GROUP_W + pr * LANES, g * _GROUP_W + (pr + 1) * LANES)
            x_pair = xbc_ref[:, xs]
            ms = []
            for e in range(2):
                hd = g * _HEADS_PER_GROUP + 2 * pr + e
                seg = cum[:, hd:hd + 1] - src_t[hd:hd + 1, :]
                ms.append((cb * jnp.exp(seg + mask_bias)).astype(BF16))
            x_blk = jnp.concatenate([jnp.where(lane_t < SSD_HEAD_DIM, x_pair, zero_x),
                                     jnp.where(lane_t >= SSD_HEAD_DIM, x_pair, zero_x)], axis=0)
            y_pair = _bdot(jnp.concatenate(ms, axis=1), x_blk)
            y_ref[:, xs] = (y_pair + y_off[:, pr * LANES:(pr + 1) * LANES]).astype(y_ref.dtype)
        xw = (xbc_ref[:, gs].astype(F32) * w_state).astype(BF16)
        b_t = b_g.astype(F32).T.astype(BF16)
        state_ref[g] = st * w_carry + _bdot(b_t, xw)


def _scan_call(geom, xbc, dt, dt_bias, a_log):
    t = SSD_CHUNK
    ncc = geom.ctx // t
    nlc = geom.seq // t
    ctx0 = geom.n_lat // t

    def row_block(b, d, s):
        jc = s + d * (ncc - 1 - 2 * s)
        sl = s - ncc
        jl = sl + d * (nlc - 1 - 2 * sl)
        return jnp.where(s < ncc, ctx0 + b * ncc + jc, b * nlc + jl)

    def pad_lanes(v):
        return jnp.pad(v.astype(F32), ((0, 0), (0, LANES - v.shape[-1]))).reshape(2, 1, LANES)

    expand = (jnp.arange(SSD_INNER)[None, :] // SSD_HEAD_DIM == jnp.arange(LANES)[:, None]).astype(BF16)

    y_shape = jax.ShapeDtypeStruct((geom.n_tok, SSD_INNER), BF16)
    return pl.pallas_call(
        _scan_kernel,
        grid=(geom.batch, ncc + nlc),
        in_specs=[
            pl.BlockSpec((t, SSD_CONV_DIM), lambda b, s: (row_block(b, 0, s), 0)),
            pl.BlockSpec((t, SSD_CONV_DIM), lambda b, s: (row_block(b, 1, s), 0)),
            pl.BlockSpec((None, t, LANES), lambda b, s: (0, row_block(b, 0, s), 0)),
            pl.BlockSpec((None, t, LANES), lambda b, s: (1, row_block(b, 1, s), 0)),
            pl.BlockSpec((2, 1, LANES), lambda b, s: (0, 0, 0)),
            pl.BlockSpec((2, 1, LANES), lambda b, s: (0, 0, 0)),
            pl.BlockSpec((LANES, SSD_INNER), lambda b, s: (0, 0)),
        ],
        out_specs=[pl.BlockSpec((t, SSD_INNER), lambda b, s: (row_block(b, 0, s), 0)),
                   pl.BlockSpec((t, SSD_INNER), lambda b, s: (row_block(b, 1, s), 0))],
        out_shape=[y_shape, y_shape],
        scratch_shapes=[pltpu.VMEM((2, SSD_GROUPS, SSD_STATE, _GROUP_W), F32)],
        compiler_params=_cparams(("parallel", "arbitrary")),
        name="ssd_scan",
    )(xbc, xbc, dt, dt, pad_lanes(dt_bias), pad_lanes(a_log), expand)


def _ssd_out_kernel(yf_ref, yb_ref, xbc_ref, z_ref, dskip_ref, ng_ref, h_ref, mod_ref, w_ref, g_ref, b_ref, o_ref):
    dsk = dskip_ref[0:1, :] + dskip_ref[1:2, :]
    y = None
    for g in range(SSD_GROUPS):
        gs = slice(g * _GROUP_W, (g + 1) * _GROUP_W)
        z = z_ref[:, gs].astype(F32)
        ysum = yf_ref[:, gs].astype(F32) + yb_ref[:, gs].astype(F32) + dsk[:, gs] * xbc_ref[:, gs].astype(F32)
        gy = ysum * (z * _sigmoid(z))
        gy = _rms_rows(gy, ng_ref[:, gs]).astype(BF16)
        part = _bdot(gy, w_ref[gs, :])
        y = part if y is None else y + part
    r = mod_ref[5:6, :] * y
    o_ref[...] = _layer_norm_rows(DEEPNORM_ALPHA * h_ref[...] + r, g_ref[...], b_ref[...])


def _ssd_out_call(geom, y, xbc, z, dskip, norm_g, h, mods, layer, w_out, ln_g, ln_b, n_rows, tm):
    return pl.pallas_call(
        _ssd_out_kernel,
        grid=(n_rows // tm,),
        in_specs=[
            _row_spec(tm, SSD_INNER),
            _row_spec(tm, SSD_INNER),
            _row_spec(tm, SSD_INNER),
            _row_spec(tm, SSD_INNER),
            _full_spec(dskip.shape),
            _full_spec((1, SSD_INNER)),
            _row_spec(tm, D_MODEL),
            geom.mod_spec(layer, tm),
            _full_spec(w_out.shape),
            _full_spec((1, D_MODEL)),
            _full_spec((1, D_MODEL)),
        ],
        out_specs=_row_spec(tm, D_MODEL),
        out_shape=jax.ShapeDtypeStruct((n_rows, D_MODEL), F32),
        compiler_params=_cparams(("parallel",)),
        name="ssd_out",
    )(y[0], y[1], xbc, z, dskip, norm_g.reshape(1, -1), h, mods, w_out, ln_g.reshape(1, -1), ln_b.reshape(1, -1))


def _attn_weights(w_in, w_uq, w_ukv):
    d = w_in.shape[0]
    kr_end = MLA_Q_RANK + MLA_KV_RANK + MLA_ROPE
    dk_lo = kr_end + DIFF_WIDTH
    dv_lo = dk_lo + DIFF_WIDTH
    w_row = jnp.concatenate([w_in[:, :kr_end], jnp.zeros((d, HEAD_BLOCK - MLA_ROPE), w_in.dtype),
                             w_in[:, dk_lo:dv_lo]], axis=1).astype(BF16)
    w_col = jnp.concatenate([w_in[:, kr_end:dk_lo], w_in[:, dv_lo:]], axis=1).T.astype(BF16)
    pad_q = HEAD_BLOCK - MLA_QK_DIM
    w_uq_p = jnp.pad(w_uq.reshape(MLA_Q_RANK, MLA_HEADS, MLA_QK_DIM), ((0, 0), (0, 0), (0, pad_q)))
    w_uqt = w_uq_p.reshape(MLA_Q_RANK, _QA_W).T.astype(BF16)
    kv = w_ukv.reshape(MLA_KV_RANK, MLA_HEADS, MLA_NOPE + MLA_V)
    w_uk = jnp.pad(kv[:, :, :MLA_NOPE], ((0, 0), (0, 0), (0, HEAD_BLOCK - MLA_NOPE)))
    w_uk = w_uk.reshape(MLA_KV_RANK, _QA_W).astype(BF16)
    w_uvt = kv[:, :, MLA_NOPE:].reshape(MLA_KV_RANK, _VA_W).T.astype(BF16)
    src = jnp.arange(HEAD_BLOCK)[:, None]
    dst = jnp.arange(_QA_W)[None, :]
    place = ((dst % HEAD_BLOCK == src + MLA_NOPE) & (src < MLA_ROPE)).astype(BF16)
    return w_row, w_col, w_uqt, w_uk, w_uvt, place


def _ssd_weights(w_in):
    d = w_in.shape[0]
    z = jnp.zeros((d, LANES - SSD_HEADS), w_in.dtype)
    return jnp.concatenate([w_in[:, :_S_DT], w_in[:, _S_DT:_S_DT + SSD_HEADS], z, w_in[:, _S_DT + SSD_HEADS:], z],
                           axis=1).astype(BF16)


def _lambda_init_for(layer):
    return 0.8 - 0.6 * math.exp(-0.3 * layer)


def _pick_tile(seq, ctx, n_ctx, want):
    tm = want
    while seq % tm or n_ctx % tm:
        tm //= 2
    return tm


def kernel(x, c, ctx, c_ctx, ada_w, ada_b, ln_g, ln_b, ffn1_w_gu, ffn1_w_down, ffn2_w_gu, ffn2_w_down, attn_w_in,
           mla_q_norm_g, mla_w_uq, mla_kv_norm_g, mla_w_ukv, diff_lam_q1, diff_lam_k1, diff_lam_q2, diff_lam_k2,
           diff_subln_g, attn_w_out, ssd_w_in, ssd_conv_w, ssd_conv_b, ssd_a_log, ssd_dt_bias, ssd_d, ssd_norm_g,
           ssd_w_out):
    batch, seq, d = x.shape
    n_ctx_tok = ctx.shape[1]
    geom = _Geom(batch, seq, n_ctx_tok)
    assert d == D_MODEL and batch + 1 <= MOD_ROWS
    assert seq % GRID_W == 0 and seq % SSD_CHUNK == 0 and n_ctx_tok % SSD_CHUNK == 0
    tq = n_ctx_tok
    assert seq % tq == 0
    tm_ffn = _pick_tile(seq, n_ctx_tok, geom.n_ctx, 512)
    tm_proj = _pick_tile(seq, n_ctx_tok, geom.n_ctx, 256)
    assert n_ctx_tok % tm_proj == 0
    fc = 256

    cond = jnp.concatenate([c, c_ctx[None, :], jnp.zeros((MOD_ROWS - batch - 1, d), F32)], axis=0)
    mods = _ada_call(cond, ada_w, ada_b).reshape(DEPTH, MOD_ROWS, N_MOD, d)
    rtab, ttab = _rope_tables(seq, tm_proj)

    h = jnp.concatenate([x.reshape(batch * seq, d), ctx.reshape(batch * n_ctx_tok, d)], axis=0)
    for l in range(DEPTH):
        last = l == DEPTH - 1
        h = _ffn_call(geom, h, mods, l, 0, ffn1_w_gu[l].astype(BF16), ffn1_w_down[l].astype(BF16),
                      ln_g[l, 0], ln_b[l, 0], geom.n_tok, tm_ffn, fc)
        n_out = geom.n_lat if last else geom.n_tok
        if l % 2 == 0:
            a = l // 2
            wts = _attn_weights(attn_w_in[a], mla_w_uq[a], mla_w_ukv[a])
            qa, ka, va, dq, dk, dv = _attn_in_call(geom, h, mods, l, rtab, ttab, wts, mla_q_norm_g[a].reshape(1, -1),
                                                   mla_kv_norm_g[a].reshape(1, -1), tm_proj)
            lam_vecs = jnp.stack([diff_lam_q1[a], diff_lam_k1[a], diff_lam_q2[a], diff_lam_k2[a]])
            lat, ctx_heads = _attention(geom, qa, ka, va, dq, dk, dv, lam_vecs, diff_subln_g[a].reshape(1, -1),
                                        _lambda_init_for(l), not last, tq)
            h = _attn_out_call(geom, lat, ctx_heads, h, mods, l, attn_w_out[a].astype(BF16), ln_g[l, 1],
                               ln_b[l, 1], tm_ffn)
        else:
            s = l // 2
            z, xbc, dt = _ssd_in_call(geom, h, mods, l, _ssd_weights(ssd_w_in[s]), ssd_conv_w[s], ssd_conv_b[s],
                                      tm_proj)
            y = _scan_call(geom, xbc, dt, ssd_dt_bias[s], ssd_a_log[s])
            dskip = jnp.repeat(ssd_d[s], SSD_HEAD_DIM, axis=-1)
            h = _ssd_out_call(geom, y, xbc, z, dskip, ssd_norm_g[s], h, mods, l, ssd_w_out[s].astype(BF16),
                              ln_g[l, 1], ln_b[l, 1], n_out, tm_proj)
        h = _ffn_call(geom, h, mods, l, 6, ffn2_w_gu[l].astype(BF16), ffn2_w_down[l].astype(BF16),
                      ln_g[l, 2], ln_b[l, 2], n_out, tm_ffn, fc)
    return h[:geom.n_lat].reshape(batch, seq, d)
```

```python
import functools
import math

import jax
import jax.numpy as jnp
from jax import lax
from jax.experimental import pallas as pl
from jax.experimental.pallas import tpu as pltpu

F32 = jnp.float32
BF16 = jnp.bfloat16

D_MODEL = 1024
DEPTH = 4
GRID_W = 64
N_MOD = 9
FFN_DIM = 2816
MACARON_WEIGHT = 0.5
MLA_HEADS = 8
MLA_Q_RANK = 384
MLA_KV_RANK = 256
MLA_NOPE = 64
MLA_ROPE = 32
MLA_V = 64
MLA_QK_DIM = MLA_NOPE + MLA_ROPE
MLA_SCALE = MLA_QK_DIM ** -0.5
DIFF_HEADS = 4
DIFF_HEAD_DIM = 64
DIFF_WIDTH = DIFF_HEADS * 2 * DIFF_HEAD_DIM
DIFF_SCALE = DIFF_HEAD_DIM ** -0.5
ROPE_BASE = 10000.0
SSD_INNER = 2 * D_MODEL
SSD_HEAD_DIM = 64
SSD_HEADS = SSD_INNER // SSD_HEAD_DIM
SSD_GROUPS = 4
SSD_STATE = 128
SSD_CONV = 5
SSD_CHUNK = 128
SSD_CONV_DIM = SSD_INNER + 2 * SSD_GROUPS * SSD_STATE
DEEPNORM_ALPHA = (2.0 * DEPTH) ** 0.25
LN_EPS = 1e-6
RMS_EPS = 1e-6

LANES = 128
V7X_VMEM_LIMIT = 56 * 1024 * 1024
HALO = 16

HEAD_BLOCK = LANES
MOD_ROWS = 24


def _cparams(sem):
    return pltpu.CompilerParams(dimension_semantics=sem, vmem_limit_bytes=V7X_VMEM_LIMIT)


def _sigmoid(v):
    return 1.0 / (1.0 + jnp.exp(-v))


def _layer_norm_rows(v, g, b):
    mu = jnp.mean(v, axis=-1, keepdims=True)
    c = v - mu
    var = jnp.mean(c * c, axis=-1, keepdims=True)
    return c * lax.rsqrt(var + LN_EPS) * g + b


def _rms_rows(v, g):
    return v * lax.rsqrt(jnp.mean(v * v, axis=-1, keepdims=True) + RMS_EPS) * g


def _bdot(a, b):
    return jnp.dot(a, b, preferred_element_type=F32)


def _bdot_nt(a, b):
    return lax.dot_general(a, b, (((1,), (1,)), ((), ())), preferred_element_type=F32)


def _ada_kernel(c_ref, w_ref, b_ref, o_ref):
    c = c_ref[...]
    s = (c * _sigmoid(c)).astype(BF16)
    o_ref[...] = _bdot(s, w_ref[...].astype(BF16)) + b_ref[...]


def _ada_call(cond, ada_w, ada_b):
    depth, d, n = ada_w.shape
    tn = n // 8
    return pl.pallas_call(
        _ada_kernel,
        grid=(depth, n // tn),
        in_specs=[
            pl.BlockSpec((MOD_ROWS, d), lambda l, j: (0, 0)),
            pl.BlockSpec((None, d, tn), lambda l, j: (l, 0, j)),
            pl.BlockSpec((None, 1, tn), lambda l, j: (l, 0, j)),
        ],
        out_specs=pl.BlockSpec((None, MOD_ROWS, tn), lambda l, j: (l, 0, j)),
        out_shape=jax.ShapeDtypeStruct((depth, MOD_ROWS, n), F32),
        compiler_params=_cparams(("parallel", "parallel")),
        name="ada_mod",
    )(cond, ada_w, ada_b.reshape(depth, 1, n))


class _Geom:
    def __init__(self, batch, seq, ctx):
        self.batch, self.seq, self.ctx = batch, seq, ctx
        self.n_lat = batch * seq
        self.n_ctx = batch * ctx
        self.n_tok = self.n_lat + self.n_ctx

    def mod_spec(self, layer, tm):
        n_lat_tiles = self.n_lat // tm
        tiles_per_batch = self.seq // tm
        batch = self.batch

        def index(i):
            return (layer, jnp.where(i < n_lat_tiles, i // tiles_per_batch, batch), 0, 0)

        return pl.BlockSpec((None, None, N_MOD, D_MODEL), index)


def _row_spec(tm, width):
    return pl.BlockSpec((tm, width), lambda i: (i, 0))


def _full_spec(shape):
    nd = len(shape)
    return pl.BlockSpec(shape, lambda i: (0,) * nd)


def _ffn_kernel(*refs, k0, fc, n_lat_tiles):
    if n_lat_tiles is None:
        h_ref, mod_ref, wgu_ref, wd_ref, g_ref, b_ref, o_ref, acc_ref = refs
        h = h_ref[...]
    else:
        hx_ref, hc_ref, mod_ref, wgu_ref, wd_ref, g_ref, b_ref, o_ref, acc_ref = refs
        h = jnp.where(pl.program_id(0) < n_lat_tiles, hx_ref[...], hc_ref[...])
    shift = mod_ref[k0:k0 + 1, :]
    scale = mod_ref[k0 + 1:k0 + 2, :]
    gate = mod_ref[k0 + 2:k0 + 3, :]
    t = (h * (1.0 + scale) + shift).astype(BF16)
    for j in range(FFN_DIM // fc):
        gj = _bdot(t, wgu_ref[:, j * fc:(j + 1) * fc])
        uj = _bdot(t, wgu_ref[:, FFN_DIM + j * fc:FFN_DIM + (j + 1) * fc])
        a = (gj * _sigmoid(gj) * uj).astype(BF16)
        y = _bdot(a, wd_ref[j * fc:(j + 1) * fc, :])
        if j == 0:
            acc_ref[...] = y
        else:
            acc_ref[...] += y
    r = (MACARON_WEIGHT * gate) * acc_ref[...]
    o_ref[...] = _layer_norm_rows(DEEPNORM_ALPHA * h + r, g_ref[...], b_ref[...])


def _ffn_call(geom, h, mods, layer, k0, w_gu, w_down, ln_g, ln_b, n_rows, tm, fc):
    if isinstance(h, tuple):
        n_lat_tiles = geom.n_lat // tm
        h_args = h
        h_specs = [pl.BlockSpec((tm, D_MODEL), lambda i: (jnp.minimum(i, n_lat_tiles - 1), 0)),
                   pl.BlockSpec((tm, D_MODEL), lambda i: (jnp.maximum(i - n_lat_tiles, 0), 0))]
    else:
        n_lat_tiles = None
        h_args = (h,)
        h_specs = [_row_spec(tm, D_MODEL)]
    kern = functools.partial(_ffn_kernel, k0=k0, fc=fc, n_lat_tiles=n_lat_tiles)
    return pl.pallas_call(
        kern,
        grid=(n_rows // tm,),
        in_specs=h_specs + [
            geom.mod_spec(layer, tm),
            _full_spec(w_gu.shape),
            _full_spec(w_down.shape),
            _full_spec((1, D_MODEL)),
            _full_spec((1, D_MODEL)),
        ],
        out_specs=_row_spec(tm, D_MODEL),
        out_shape=jax.ShapeDtypeStruct((n_rows, D_MODEL), F32),
        scratch_shapes=[pltpu.VMEM((tm, D_MODEL), F32)],
        compiler_params=_cparams(("parallel",)),
        name="ffn",
    )(*h_args, mods, w_gu, w_down, ln_g.reshape(1, -1), ln_b.reshape(1, -1))


_QA_W = MLA_HEADS * HEAD_BLOCK
_VA_W = MLA_HEADS * MLA_V
_R_CKV = MLA_Q_RANK
_R_KR = MLA_Q_RANK + MLA_KV_RANK
_R_DK = _R_KR + HEAD_BLOCK
ATTN_ROW_COLS = _R_DK + DIFF_WIDTH
_MLA_HALF = MLA_ROPE // 2
_DIFF_HALF = DIFF_HEAD_DIM // 2
_T_SIN_M = _MLA_HALF
_T_COS_D = 2 * _MLA_HALF
_T_SIN_D = _T_COS_D + _DIFF_HALF
ROPE_T_ROWS = _T_SIN_D + _DIFF_HALF
ATT_KC = 256
LOG2E = math.log2(math.e)
MLA_QSCALE = MLA_SCALE * LOG2E
DIFF_QSCALE = DIFF_SCALE * LOG2E


def _rope_block(v, tab_ref, t0, shift):
    left = pltpu.roll(v, LANES - shift, axis=1)
    right = pltpu.roll(v, shift, axis=1)
    return v * tab_ref[t0] + left * tab_ref[t0 + 1] + right * tab_ref[t0 + 2]


def _rope_rows(dst_ref, src, r0, half, cos, sin, scale):
    t1 = src[r0:r0 + half]
    t2 = src[r0 + half:r0 + 2 * half]
    dst_ref[r0:r0 + half, :] = ((t1 * cos - t2 * sin) * scale).astype(dst_ref.dtype)
    dst_ref[r0 + half:r0 + 2 * half, :] = ((t1 * sin + t2 * cos) * scale).astype(dst_ref.dtype)


def _attn_in_kernel(h_ref, mod_ref, rtab_ref, ttab_ref, wrow_ref, wcol_ref, qg_ref, wuqt_ref, kvg_ref, wuk_ref,
                    wuvt_ref, place_ref, qat_ref, ka_ref, vat_ref, dqt_ref, dk_ref, dvt_ref):
    h = h_ref[...]
    u = (h * (1.0 + mod_ref[4:5, :]) + mod_ref[3:4, :]).astype(BF16)
    proj = _bdot(u, wrow_ref[...])
    projt = _bdot_nt(wcol_ref[...], u)

    cqn = _rms_rows(proj[:, :_R_CKV], qg_ref[...]).astype(BF16)
    qt = _bdot_nt(wuqt_ref[...], cqn)
    cos_m = ttab_ref[0:_T_SIN_M, :]
    sin_m = ttab_ref[_T_SIN_M:_T_COS_D, :]
    for hd in range(MLA_HEADS):
        r0 = hd * HEAD_BLOCK
        qat_ref[r0:r0 + MLA_NOPE, :] = (qt[r0:r0 + MLA_NOPE] * MLA_QSCALE).astype(BF16)
        _rope_rows(qat_ref, qt, r0 + MLA_NOPE, _MLA_HALF, cos_m, sin_m, MLA_QSCALE)
        qat_ref[r0 + MLA_QK_DIM:r0 + HEAD_BLOCK, :] = jnp.zeros((HEAD_BLOCK - MLA_QK_DIM, qt.shape[1]), BF16)

    ckvn = _rms_rows(proj[:, _R_CKV:_R_KR], kvg_ref[...]).astype(BF16)
    kr = _rope_block(proj[:, _R_KR:_R_DK], rtab_ref, 0, _MLA_HALF).astype(BF16)
    ka_ref[...] = (_bdot(ckvn, wuk_ref[...]) + _bdot(kr, place_ref[...])).astype(BF16)
    vat_ref[...] = _bdot_nt(wuvt_ref[...], ckvn).astype(BF16)

    cos_d = ttab_ref[_T_COS_D:_T_SIN_D, :]
    sin_d = ttab_ref[_T_SIN_D:ROPE_T_ROWS, :]
    for sub in range(2 * DIFF_HEADS):
        _rope_rows(dqt_ref, projt, sub * DIFF_HEAD_DIM, _DIFF_HALF, cos_d, sin_d, DIFF_QSCALE)
    dvt_ref[...] = projt[DIFF_WIDTH:].astype(BF16)
    for hd in range(DIFF_HEADS):
        sk = slice(_R_DK + hd * HEAD_BLOCK, _R_DK + (hd + 1) * HEAD_BLOCK)
        so = slice(hd * HEAD_BLOCK, (hd + 1) * HEAD_BLOCK)
        dk_ref[:, so] = _rope_block(proj[:, sk], rtab_ref, 3, _DIFF_HALF).astype(BF16)


def _attn_in_call(geom, h, mods, layer, rtab, ttab, wts, q_g, kv_g, tm):
    n_lat_tiles = geom.n_lat // tm
    tiles_per_seq = geom.seq // tm

    def pos_block(i):
        return jnp.where(i < n_lat_tiles, i % tiles_per_seq, tiles_per_seq)

    n = geom.n_tok
    w_row, w_col, w_uqt, w_uk, w_uvt, place = wts

    def tok_major(width):
        return _row_spec(tm, width), jax.ShapeDtypeStruct((n, width), BF16)

    def chan_major(rows):
        return pl.BlockSpec((rows, tm), lambda i: (0, i)), jax.ShapeDtypeStruct((rows, n), BF16)

    def chan_major_chunked(rows):
        return (pl.BlockSpec((None, rows, tm), lambda i: (i, 0, 0)),
                jax.ShapeDtypeStruct((n // tm, rows, tm), BF16))

    assert tm == ATT_KC
    outs = [chan_major(_QA_W), tok_major(_QA_W), chan_major_chunked(_VA_W),
            chan_major(DIFF_WIDTH), tok_major(DIFF_WIDTH), chan_major_chunked(DIFF_WIDTH)]
    return pl.pallas_call(
        _attn_in_kernel,
        grid=(n // tm,),
        in_specs=[
            _row_spec(tm, D_MODEL),
            geom.mod_spec(layer, tm),
            pl.BlockSpec((6, tm, LANES), lambda i: (0, pos_block(i), 0)),
            pl.BlockSpec((ROPE_T_ROWS, tm), lambda i: (0, pos_block(i))),
            _full_spec(w_row.shape),
            _full_spec(w_col.shape),
            _full_spec(q_g.shape),
            _full_spec(w_uqt.shape),
            _full_spec(kv_g.shape),
            _full_spec(w_uk.shape),
            _full_spec(w_uvt.shape),
            _full_spec(place.shape),
        ],
        out_specs=[o[0] for o in outs],
        out_shape=[o[1] for o in outs],
        compiler_params=_cparams(("parallel",)),
        name="attn_in",
    )(h, mods, rtab, ttab, w_row, w_col, q_g, w_uqt, kv_g, w_uk, w_uvt, place)


def _rope_tables(seq, tm):
    rows = seq // GRID_W
    row = jnp.repeat(jnp.arange(rows, dtype=F32), GRID_W)
    col = jnp.tile(jnp.arange(GRID_W, dtype=F32), rows)

    def angles(rot_dim):
        n_freq = rot_dim // 4
        inv = ROPE_BASE ** (-jnp.arange(n_freq, dtype=F32) / n_freq)
        return jnp.concatenate([row[:, None] * inv, col[:, None] * inv], axis=-1)

    def build(ang, first_lo, half, live):
        lane = jnp.arange(LANES)
        cos, sin = jnp.cos(ang), jnp.sin(ang)
        zeros = jnp.zeros((seq, LANES), F32)
        a = jnp.where(lane < live, 1.0, 0.0)[None, :] + zeros
        bm, cm = zeros, zeros
        for lo in first_lo:
            a = a.at[:, lo:lo + half].set(cos).at[:, lo + half:lo + 2 * half].set(cos)
            bm = bm.at[:, lo:lo + half].set(-sin)
            cm = cm.at[:, lo + half:lo + 2 * half].set(sin)
        ident = jnp.where(lane < live, 1.0, 0.0)[None, :] + jnp.zeros((tm, LANES), F32)
        z = jnp.zeros((tm, LANES), F32)
        return [jnp.concatenate([a, ident]), jnp.concatenate([bm, z]), jnp.concatenate([cm, z])]

    ang_m = angles(MLA_ROPE)
    ang_d = angles(DIFF_HEAD_DIM)
    rtab = jnp.stack(build(ang_m, (0,), _MLA_HALF, MLA_ROPE)
                     + build(ang_d, (0, DIFF_HEAD_DIM), _DIFF_HALF, LANES))

    def chan(ang, fn, fill):
        return jnp.concatenate([fn(ang).T, jnp.full((ang.shape[1], tm), fill, F32)], axis=1)

    ttab = jnp.concatenate([chan(ang_m, jnp.cos, 1.0), chan(ang_m, jnp.sin, 0.0),
                            chan(ang_d, jnp.cos, 1.0), chan(ang_d, jnp.sin, 0.0)], axis=0)
    return rtab, ttab


def _softmax_pv_t(qt, keys, vals_t):
    s = [_bdot(k, qt) for k in keys]
    m = s[0].max(axis=0, keepdims=True)
    for si in s[1:]:
        m = jnp.maximum(m, si.max(axis=0, keepdims=True))
    num, den = None, None
    for si, vt in zip(s, vals_t):
        p = jnp.exp2(si - m)
        d = p.sum(axis=0, keepdims=True)
        o = _bdot(vt, p.astype(BF16))
        num = o if num is None else num + o
        den = d if den is None else den + d
    return num / den


def _fold_rows(v, op):
    parts = [v[r:r + 8] for r in range(0, v.shape[0], 8)]
    while len(parts) > 1:
        parts = [op(parts[i], parts[i + 1]) for i in range(0, len(parts) - 1, 2)] + (
            [parts[-1]] if len(parts) % 2 else [])
    return parts[0]


def _attn_pipelined(qi, n_q, q_cur, q_nxt, k_lanes, kx_ref, kc_ref, vx_ref, vc_ref, s_refs, m_ref, acc_ref, o_ref,
                    combine):
    n_maps = len(q_cur)
    tq = q_cur[0].shape[1]
    chunks = ([(kc_ref, vc_ref, c) for c in range(kc_ref.shape[0])]
              + [(kx_ref, vx_ref, c) for c in range(kx_ref.shape[0])])
    neg = jnp.full((8, tq), -jnp.inf, F32)
    zero = jnp.zeros((8, tq), F32)

    def score_chunk(dst_ref, ci, q_list, mrun):
        k_ref, _, c = chunks[ci]
        out = []
        for j in range(n_maps):
            s = _bdot(k_ref[c, :, k_lanes[j]], q_list[j])
            dst_ref[j, ci] = s
            out.append(jnp.maximum(mrun[j], _fold_rows(s, jnp.maximum)))
        return out

    def exp_chunk(src_ref, ci, m_cur, lrun):
        _, v_ref, c = chunks[ci]
        vt = v_ref[c]
        out = []
        for j in range(n_maps):
            p = jnp.exp2(src_ref[j, ci] - m_cur[j])
            out.append(lrun[j] + _fold_rows(p, jnp.add))
            o = _bdot(vt, p.astype(BF16))
            if ci == 0:
                acc_ref[j] = o
            else:
                acc_ref[j] += o
        return out

    @pl.when(qi == 0)
    def _():
        mrun = [neg] * n_maps
        for ci in range(len(chunks)):
            mrun = score_chunk(s_refs[0], ci, q_cur, mrun)
        for j in range(n_maps):
            m_ref[0, j] = mrun[j]

    def step(slot, with_next):
        cur_ref, nxt_ref = s_refs[slot], s_refs[1 - slot]
        m_cur = [jnp.max(m_ref[slot, j], axis=0, keepdims=True) for j in range(n_maps)]
        lrun, mrun = [zero] * n_maps, [neg] * n_maps
        for ci in range(len(chunks)):
            if with_next:
                mrun = score_chunk(nxt_ref, ci, q_nxt, mrun)
            lrun = exp_chunk(cur_ref, ci, m_cur, lrun)
        if with_next:
            for j in range(n_maps):
                m_ref[1 - slot, j] = mrun[j]
        outs = [acc_ref[j] / jnp.sum(lrun[j], axis=0, keepdims=True) for j in range(n_maps)]
        o_ref[...] = combine(outs).astype(o_ref.dtype)

    for slot in (0, 1):
        pl.when(jnp.logical_and(qi % 2 == slot, qi < n_q - 1))(functools.partial(step, slot, True))
    pl.when(qi == n_q - 1)(functools.partial(step, (n_q - 1) % 2, False))


def _mla_maps(q_ref):
    return [q_ref[:HEAD_BLOCK, :], q_ref[HEAD_BLOCK:, :]]


_MLA_K_LANES = (slice(0, HEAD_BLOCK), slice(HEAD_BLOCK, 2 * HEAD_BLOCK))
_DIFF_K_LANES = (slice(0, HEAD_BLOCK), slice(0, HEAD_BLOCK))


def _mla_combine(outs):
    row = lax.broadcasted_iota(jnp.int32, outs[0].shape, 0)
    return jnp.where(row < MLA_V, outs[0], outs[1]).T


def _diff_maps(q_ref):
    q = q_ref[...]
    row = lax.broadcasted_iota(jnp.int32, q.shape, 0)
    zero = jnp.zeros_like(q)
    return [jnp.where(row < DIFF_HEAD_DIM, q, zero), jnp.where(row >= DIFF_HEAD_DIM, q, zero)]


def _diff_lambda(lam_ref, lam_init):
    lv = lam_ref[...]
    return (jnp.exp(jnp.sum(lv[0:1] * lv[1:2], axis=-1, keepdims=True))
            - jnp.exp(jnp.sum(lv[2:3] * lv[3:4], axis=-1, keepdims=True)) + lam_init)


def _diff_combine(outs, lam, lam_init, g):
    return _rms_rows((outs[0] - lam * outs[1]).T, g) * (1.0 - lam_init)


def _mla_lat_kernel(qc_ref, qn_ref, kx_ref, kc_ref, vx_ref, vc_ref, o_ref, s0_ref, s1_ref, m_ref, acc_ref, *, n_q):
    _attn_pipelined(pl.program_id(2), n_q, _mla_maps(qc_ref), _mla_maps(qn_ref), _MLA_K_LANES, kx_ref, kc_ref,
                    vx_ref, vc_ref, (s0_ref, s1_ref), m_ref, acc_ref, o_ref, _mla_combine)


def _diff_lat_kernel(lam_ref, g_ref, qc_ref, qn_ref, kx_ref, kc_ref, vx_ref, vc_ref, o_ref, s0_ref, s1_ref, m_ref,
                     acc_ref, *, n_q, lam_init):
    combine = functools.partial(_diff_combine, lam=_diff_lambda(lam_ref, lam_init), lam_init=lam_init, g=g_ref[...])
    _attn_pipelined(pl.program_id(2), n_q, _diff_maps(qc_ref), _diff_maps(qn_ref), _DIFF_K_LANES, kx_ref, kc_ref,
                    vx_ref, vc_ref, (s0_ref, s1_ref), m_ref, acc_ref, o_ref, combine)


def _ctx_attention(q_maps, k_lanes, kc_ref, vc_ref):
    n_cc = kc_ref.shape[0]
    vals = [vc_ref[cc] for cc in range(n_cc)]
    return [_softmax_pv_t(q, [kc_ref[cc, :, lanes] for cc in range(n_cc)], vals)
            for q, lanes in zip(q_maps, k_lanes)]


def _mla_ctx_kernel(q_ref, kc_ref, vc_ref, o_ref):
    o_ref[...] = _mla_combine(_ctx_attention(_mla_maps(q_ref), _MLA_K_LANES, kc_ref, vc_ref)).astype(o_ref.dtype)


def _diff_ctx_kernel(lam_ref, g_ref, q_ref, kc_ref, vc_ref, o_ref, *, lam_init):
    outs = _ctx_attention(_diff_maps(q_ref), _DIFF_K_LANES, kc_ref, vc_ref)
    o_ref[...] = _diff_combine(outs, _diff_lambda(lam_ref, lam_init), lam_init, g_ref[...]).astype(o_ref.dtype)


def _attn_lat_call(geom, kern, name, small, qt, k3, vt3, qk_w, n_groups, tq):
    n_q = geom.seq // tq
    n_lc = geom.seq // ATT_KC
    n_cc = geom.ctx // ATT_KC
    ctx_c0 = geom.n_lat // geom.ctx
    small_specs = [pl.BlockSpec(a.shape, lambda b, p, qi: (0, 0)) for a in small]
    specs = small_specs + [
        pl.BlockSpec((qk_w, tq), lambda b, p, qi: (p, b * n_q + qi)),
        pl.BlockSpec((qk_w, tq), lambda b, p, qi: (p, b * n_q + jnp.minimum(qi + 1, n_q - 1))),
        pl.BlockSpec((n_lc, ATT_KC, qk_w), lambda b, p, qi: (b, 0, p)),
        pl.BlockSpec((n_cc, ATT_KC, qk_w), lambda b, p, qi: (ctx_c0 + b, 0, p)),
        pl.BlockSpec((n_lc, HEAD_BLOCK, ATT_KC), lambda b, p, qi: (b, p, 0)),
        pl.BlockSpec((n_cc, HEAD_BLOCK, ATT_KC), lambda b, p, qi: (ctx_c0 + b, p, 0)),
    ]
    return pl.pallas_call(
        functools.partial(kern, n_q=n_q),
        grid=(geom.batch, n_groups, n_q),
        in_specs=specs,
        out_specs=pl.BlockSpec((tq, HEAD_BLOCK), lambda b, p, qi: (b * n_q + qi, p)),
        out_shape=jax.ShapeDtypeStruct((geom.n_lat, n_groups * HEAD_BLOCK), BF16),
        scratch_shapes=[pltpu.VMEM((2, n_cc + n_lc, ATT_KC, tq), F32),
                        pltpu.VMEM((2, n_cc + n_lc, ATT_KC, tq), F32),
                        pltpu.VMEM((2, 2, 8, tq), F32),
                        pltpu.VMEM((2, HEAD_BLOCK, tq), F32)],
        compiler_params=_cparams(("parallel", "parallel", "arbitrary")),
        name=name,
    )(*small, qt, qt, k3, k3, vt3, vt3)


def _attn_ctx_call(geom, kern, name, small, qt, k3, vt3, qk_w, n_groups):
    n_cc = geom.ctx // ATT_KC
    ctx_c0 = geom.n_lat // geom.ctx
    small_specs = [pl.BlockSpec(a.shape, lambda b, p: (0, 0)) for a in small]
    specs = small_specs + [
        pl.BlockSpec((qk_w, geom.ctx), lambda b, p: (p, ctx_c0 + b)),
        pl.BlockSpec((n_cc, ATT_KC, qk_w), lambda b, p: (ctx_c0 + b, 0, p)),
        pl.BlockSpec((n_cc, HEAD_BLOCK, ATT_KC), lambda b, p: (ctx_c0 + b, p, 0)),
    ]
    return pl.pallas_call(
        kern,
        grid=(geom.batch, n_groups),
        in_specs=specs,
        out_specs=pl.BlockSpec((geom.ctx, HEAD_BLOCK), lambda b, p: (b, p)),
        out_shape=jax.ShapeDtypeStruct((geom.n_ctx, n_groups * HEAD_BLOCK), BF16),
        compiler_params=_cparams(("parallel", "parallel")),
        name=name,
    )(*small, qt, k3, vt3)


def _attention(geom, qa, ka, va, dq, dk, dv, lam_vecs, subln_g, lam_init, with_ctx, tq):
    n_chunks = geom.n_tok // ATT_KC
    ka3 = ka.reshape(n_chunks, ATT_KC, _QA_W)
    dk3 = dk.reshape(n_chunks, ATT_KC, DIFF_WIDTH)
    small = [lam_vecs, subln_g]
    lat = (_attn_lat_call(geom, _mla_lat_kernel, "mla_attn", [], qa, ka3, va, 2 * HEAD_BLOCK, MLA_HEADS // 2, tq),
           _attn_lat_call(geom, functools.partial(_diff_lat_kernel, lam_init=lam_init), "diff_attn", small, dq, dk3,
                          dv, HEAD_BLOCK, DIFF_HEADS, tq))
    if not with_ctx:
        return lat, None
    ctx = (_attn_ctx_call(geom, _mla_ctx_kernel, "mla_attn_ctx", [], qa, ka3, va, 2 * HEAD_BLOCK, MLA_HEADS // 2),
           _attn_ctx_call(geom, functools.partial(_diff_ctx_kernel, lam_init=lam_init), "diff_attn_ctx", small, dq,
                          dk3, dv, HEAD_BLOCK, DIFF_HEADS))
    return lat, ctx


def _attn_out_kernel(*refs, n_lat_tiles, with_ctx):
    if with_ctx:
        oa_ref, od_ref, oac_ref, odc_ref, h_ref, mod_ref, w_ref, g_ref, b_ref, o_ref = refs
        is_lat = pl.program_id(0) < n_lat_tiles
        oa = jnp.where(is_lat, oa_ref[...], oac_ref[...])
        od = jnp.where(is_lat, od_ref[...], odc_ref[...])
    else:
        oa_ref, od_ref, h_ref, mod_ref, w_ref, g_ref, b_ref, o_ref = refs
        oa, od = oa_ref[...], od_ref[...]
    y = _bdot(oa, w_ref[:_VA_W, :]) + _bdot(od, w_ref[_VA_W:, :])
    r = mod_ref[5:6, :] * y
    o_ref[...] = _layer_norm_rows(DEEPNORM_ALPHA * h_ref[...] + r, g_ref[...], b_ref[...])


def _attn_out_call(geom, lat, ctx, h, mods, layer, w_out, ln_g, ln_b, tm):
    n_lat_tiles = geom.n_lat // tm
    with_ctx = ctx is not None
    n_rows = geom.n_tok if with_ctx else geom.n_lat
    head_specs = [pl.BlockSpec((tm, _VA_W), lambda i: (jnp.minimum(i, n_lat_tiles - 1), 0)),
                  pl.BlockSpec((tm, DIFF_WIDTH), lambda i: (jnp.minimum(i, n_lat_tiles - 1), 0))]
    heads = list(lat)
    if with_ctx:
        head_specs += [pl.BlockSpec((tm, _VA_W), lambda i: (jnp.maximum(i - n_lat_tiles, 0), 0)),
                       pl.BlockSpec((tm, DIFF_WIDTH), lambda i: (jnp.maximum(i - n_lat_tiles, 0), 0))]
        heads += list(ctx)
    return pl.pallas_call(
        functools.partial(_attn_out_kernel, n_lat_tiles=n_lat_tiles, with_ctx=with_ctx),
        grid=(n_rows // tm,),
        in_specs=head_specs + [
            _row_spec(tm, D_MODEL),
            geom.mod_spec(layer, tm),
            _full_spec(w_out.shape),
            _full_spec((1, D_MODEL)),
            _full_spec((1, D_MODEL)),
        ],
        out_specs=_row_spec(tm, D_MODEL),
        out_shape=jax.ShapeDtypeStruct((n_rows, D_MODEL), F32),
        compiler_params=_cparams(("parallel",)),
        name="attn_out",
    )(*heads, h, mods, w_out, ln_g.reshape(1, -1), ln_b.reshape(1, -1))


SSD_PROJ_COLS = SSD_INNER + SSD_CONV_DIM + 2 * LANES
_S_XBC = SSD_INNER
_S_DT = SSD_INNER + SSD_CONV_DIM


def _ssd_in_kernel(h_ref, hp_ref, hn_ref, mod_ref, w_ref, cw_ref, cb_ref, z_ref, xbc_ref, dt_ref, pad_ref, *,
                   tm, tiles_per_seq, n_lat_tiles, nc):
    i = pl.program_id(0)
    is_lat = i < n_lat_tiles
    pos = i % tiles_per_seq
    has_prev = jnp.logical_and(is_lat, pos > 0)
    has_next = jnp.logical_and(is_lat, pos < tiles_per_seq - 1)
    scale1 = 1.0 + mod_ref[4:5, :]
    shift = mod_ref[3:4, :]
    u = (h_ref[...] * scale1 + shift).astype(BF16)
    u_ext = jnp.concatenate([(hp_ref[...] * scale1 + shift).astype(BF16), u,
                             (hn_ref[...] * scale1 + shift).astype(BF16)], axis=0)
    for lo in range(0, SSD_INNER, nc):
        z_ref[:, lo:lo + nc] = _bdot(u, w_ref[:, lo:lo + nc]).astype(BF16)
    dt = _bdot(u, w_ref[:, _S_DT:])
    dt_ref[0] = dt[:, :LANES]
    dt_ref[1] = dt[:, LANES:]
    half = SSD_CONV // 2
    for lo in range(0, SSD_CONV_DIM, nc):
        cs = slice(lo, lo + nc)
        r = _bdot(u_ext, w_ref[:, _S_XBC + lo:_S_XBC + lo + nc])
        head, tail = r[:HALO], r[HALO + tm:]
        r = jnp.concatenate([jnp.where(has_prev, head, jnp.zeros_like(head)), r[HALO:HALO + tm],
                             jnp.where(has_next, tail, jnp.zeros_like(tail))], axis=0)
        acc = cb_ref[:, cs] + jnp.zeros((tm, nc), F32)
        for k in range(SSD_CONV):
            sh = r if k == half else pltpu.roll(r, (half - k) % r.shape[0], axis=0)
            acc = acc + sh[HALO:HALO + tm] * cw_ref[k:k + 1, cs]
        xbc_ref[:, cs] = (acc * _sigmoid(acc)).astype(BF16)


def _ssd_in_call(geom, h, mods, layer, w_in, conv_w, conv_b, tm):
    n = geom.n_tok
    nc = 512
    hb = tm // HALO
    last_hb = n // HALO - 1
    kern = functools.partial(_ssd_in_kernel, tm=tm, tiles_per_seq=geom.seq // tm, n_lat_tiles=geom.n_lat // tm, nc=nc)
    return pl.pallas_call(
        kern,
        grid=(n // tm,),
        in_specs=[
            _row_spec(tm, D_MODEL),
            pl.BlockSpec((HALO, D_MODEL), lambda i: (jnp.maximum(i * hb - 1, 0), 0)),
            pl.BlockSpec((HALO, D_MODEL), lambda i: (jnp.minimum((i + 1) * hb, last_hb), 0)),
            geom.mod_spec(layer, tm),
            _full_spec(w_in.shape),
            _full_spec(conv_w.shape),
            _full_spec((1, SSD_CONV_DIM)),
        ],
        out_specs=[_row_spec(tm, SSD_INNER), _row_spec(tm, SSD_CONV_DIM),
                   pl.BlockSpec((2, tm, LANES), lambda i: (0, i, 0))],
        out_shape=[jax.ShapeDtypeStruct((n, SSD_INNER), BF16), jax.ShapeDtypeStruct((n, SSD_CONV_DIM), BF16),
                   jax.ShapeDtypeStruct((2, n, LANES), F32)],
        scratch_shapes=[pltpu.VMEM((tm + 2 * HALO, nc), F32)],
        compiler_params=_cparams(("parallel",)),
        name="ssd_in",
    )(h, h, h, mods, w_in, conv_w, conv_b.reshape(1, -1))


_GROUP_W = SSD_INNER // SSD_GROUPS
_HEADS_PER_GROUP = SSD_HEADS // SSD_GROUPS
_B_COL = SSD_INNER
_C_COL = SSD_INNER + SSD_GROUPS * SSD_STATE
SCAN_CPS = 2


def _scan_kernel(xf_ref, xb_ref, dtf_ref, dtb_ref, bias_ref, alog_ref, expand_ref, yf_ref, yb_ref, state_ref):
    @pl.when(pl.program_id(1) == 0)
    def _():
        state_ref[...] = jnp.zeros_like(state_ref)

    for k in range(SCAN_CPS):
        rf = pl.ds(k * SSD_CHUNK, SSD_CHUNK)
        rb = pl.ds((SCAN_CPS - 1 - k) * SSD_CHUNK, SSD_CHUNK)
        _scan_chunk(0, xf_ref.at[rf], dtf_ref.at[rf], bias_ref[0], alog_ref[0], expand_ref, yf_ref.at[rf],
                    state_ref.at[0])
        _scan_chunk(1, xb_ref.at[rb], dtb_ref.at[rb], bias_ref[1], alog_ref[1], expand_ref, yb_ref.at[rb],
                    state_ref.at[1])


def _scan_chunk(d, xbc_ref, dt_ref, bias, alog, expand_ref, y_ref, state_ref):
    t = SSD_CHUNK
    lane = lax.broadcasted_iota(jnp.int32, (1, LANES), 1)
    a = jnp.where(lane < SSD_HEADS, -jnp.exp(alog), 0.0)
    raw = dt_ref[...] + bias
    e = jnp.exp(-jnp.abs(raw))
    u = 1.0 + e
    um1 = u - 1.0
    dt = jnp.maximum(raw, 0.0) + jnp.where(um1 == 0.0, e, jnp.log(u) * (e / jnp.where(um1 == 0.0, 1.0, um1)))
    da = dt * a
    row = lax.broadcasted_iota(jnp.int32, (t, t), 0)
    col = lax.broadcasted_iota(jnp.int32, (t, t), 1)
    mask = (col <= row) if d == 0 else (col >= row)
    tri = mask.astype(BF16)
    da_hi = da.astype(BF16)
    rem = da - da_hi.astype(F32)
    da_mid = rem.astype(BF16)
    da_lo = (rem - da_mid.astype(F32)).astype(BF16)
    cum = _bdot(tri, da_hi) + _bdot(tri, da_mid) + _bdot(tri, da_lo)
    total = jnp.sum(da, axis=0, keepdims=True)
    src_t = (cum - jnp.log(dt)).T
    ecum = jnp.exp(cum)
    dtdec = dt * jnp.exp(total - cum)
    cdec = jnp.broadcast_to(jnp.exp(total), (8, LANES))

    per_head = jnp.concatenate([dtdec, ecum, cdec], axis=0).astype(BF16)

    lane_t = lax.broadcasted_iota(jnp.int32, (t, LANES), 1)
    mask_bias = jnp.where(mask, 0.0, -jnp.inf)
    zero_x = jnp.zeros((t, LANES), BF16)
    for g in range(SSD_GROUPS):
        gs = slice(g * _GROUP_W, (g + 1) * _GROUP_W)
        ex = _bdot(per_head, expand_ref[:, gs])
        w_state, w_off, w_carry = ex[0:t], ex[t:2 * t], ex[2 * t:2 * t + 1]
        b_g = xbc_ref[:, _B_COL + g * SSD_STATE:_B_COL + (g + 1) * SSD_STATE]
        c_g = xbc_ref[:, _C_COL + g * SSD_STATE:_C_COL + (g + 1) * SSD_STATE]
        cb = _bdot_nt(c_g, b_g)
        st = state_ref[g]
        y_off = _bdot(c_g, st.astype(BF16)) * w_off
        for pr in range(_HEADS_PER_GROUP // 2):
            xs = slice(g * _GROUP_W + pr * LANES, g * _GROUP_W + (pr + 1) * LANES)
            x_pair = xbc_ref[:, xs]
            ms = []
            for e in range(2):
                hd = g * _HEADS_PER_GROUP + 2 * pr + e
                seg = cum[:, hd:hd + 1] - src_t[hd:hd + 1, :]
                ms.append((cb * jnp.exp(seg + mask_bias)).astype(BF16))
            x_blk = jnp.concatenate([jnp.where(lane_t < SSD_HEAD_DIM, x_pair, zero_x),
                                     jnp.where(lane_t >= SSD_HEAD_DIM, x_pair, zero_x)], axis=0)
            y_pair = _bdot(jnp.concatenate(ms, axis=1), x_blk)
            y_ref[:, xs] = (y_pair + y_off[:, pr * LANES:(pr + 1) * LANES]).astype(y_ref.dtype)
        xw = (xbc_ref[:, gs].astype(F32) * w_state).astype(BF16)
        b_t = b_g.astype(F32).T.astype(BF16)
        state_ref[g] = st * w_carry + _bdot(b_t, xw)


def _scan_call(geom, xbc, dt, dt_bias, a_log):
    t = SCAN_CPS * SSD_CHUNK
    assert geom.ctx % t == 0 and geom.seq % t == 0
    ncc = geom.ctx // t
    nlc = geom.seq // t
    ctx0 = geom.n_lat // t

    def row_block(b, d, s):
        jc = s + d * (ncc - 1 - 2 * s)
        sl = s - ncc
        jl = sl + d * (nlc - 1 - 2 * sl)
        return jnp.where(s < ncc, ctx0 + b * ncc + jc, b * nlc + jl)

    def pad_lanes(v):
        return jnp.pad(v.astype(F32), ((0, 0), (0, LANES - v.shape[-1]))).reshape(2, 1, LANES)

    expand = (jnp.arange(SSD_INNER)[None, :] // SSD_HEAD_DIM == jnp.arange(LANES)[:, None]).astype(BF16)

    y_shape = jax.ShapeDtypeStruct((geom.n_tok, SSD_INNER), BF16)
    return pl.pallas_call(
        _scan_kernel,
        grid=(geom.batch, ncc + nlc),
        in_specs=[
            pl.BlockSpec((t, SSD_CONV_DIM), lambda b, s: (row_block(b, 0, s), 0)),
            pl.BlockSpec((t, SSD_CONV_DIM), lambda b, s: (row_block(b, 1, s), 0)),
            pl.BlockSpec((None, t, LANES), lambda b, s: (0, row_block(b, 0, s), 0)),
            pl.BlockSpec((None, t, LANES), lambda b, s: (1, row_block(b, 1, s), 0)),
            pl.BlockSpec((2, 1, LANES), lambda b, s: (0, 0, 0)),
            pl.BlockSpec((2, 1, LANES), lambda b, s: (0, 0, 0)),
            pl.BlockSpec((LANES, SSD_INNER), lambda b, s: (0, 0)),
        ],
        out_specs=[pl.BlockSpec((t, SSD_INNER), lambda b, s: (row_block(b, 0, s), 0)),
                   pl.BlockSpec((t, SSD_INNER), lambda b, s: (row_block(b, 1, s), 0))],
        out_shape=[y_shape, y_shape],
        scratch_shapes=[pltpu.VMEM((2, SSD_GROUPS, SSD_STATE, _GROUP_W), F32)],
        compiler_params=_cparams(("parallel", "arbitrary")),
        name="ssd_scan",
    )(xbc, xbc, dt, dt, pad_lanes(dt_bias), pad_lanes(a_log), expand)


def _ssd_out_kernel(yf_ref, yb_ref, xbc_ref, z_ref, dskip_ref, ng_ref, h_ref, mod_ref, w_ref, g_ref, b_ref, o_ref):
    dsk = dskip_ref[0:1, :] + dskip_ref[1:2, :]
    y = None
    for g in range(SSD_GROUPS):
        gs = slice(g * _GROUP_W, (g + 1) * _GROUP_W)
        z = z_ref[:, gs].astype(F32)
        ysum = yf_ref[:, gs].astype(F32) + yb_ref[:, gs].astype(F32) + dsk[:, gs] * xbc_ref[:, gs].astype(F32)
        gy = ysum * (z * _sigmoid(z))
        gy = _rms_rows(gy, ng_ref[:, gs]).astype(BF16)
        part = _bdot(gy, w_ref[gs, :])
        y = part if y is None else y + part
    r = mod_ref[5:6, :] * y
    o_ref[...] = _layer_norm_rows(DEEPNORM_ALPHA * h_ref[...] + r, g_ref[...], b_ref[...])


def _ssd_out_call(geom, y, xbc, z, dskip, norm_g, h, mods, layer, w_out, ln_g, ln_b, n_rows, tm):
    return pl.pallas_call(
        _ssd_out_kernel,
        grid=(n_rows // tm,),
        in_specs=[
            _row_spec(tm, SSD_INNER),
            _row_spec(tm, SSD_INNER),
            _row_spec(tm, SSD_INNER),
            _row_spec(tm, SSD_INNER),
            _full_spec(dskip.shape),
            _full_spec((1, SSD_INNER)),
            _row_spec(tm, D_MODEL),
            geom.mod_spec(layer, tm),
            _full_spec(w_out.shape),
            _full_spec((1, D_MODEL)),
            _full_spec((1, D_MODEL)),
        ],
        out_specs=_row_spec(tm, D_MODEL),
        out_shape=jax.ShapeDtypeStruct((n_rows, D_MODEL), F32),
        compiler_params=_cparams(("parallel",)),
        name="ssd_out",
    )(y[0], y[1], xbc, z, dskip, norm_g.reshape(1, -1), h, mods, w_out, ln_g.reshape(1, -1), ln_b.reshape(1, -1))


def _attn_weights(w_in, w_uq, w_ukv):
    d = w_in.shape[0]
    kr_end = MLA_Q_RANK + MLA_KV_RANK + MLA_ROPE
    dk_lo = kr_end + DIFF_WIDTH
    dv_lo = dk_lo + DIFF_WIDTH
    w_row = jnp.concatenate([w_in[:, :kr_end], jnp.zeros((d, HEAD_BLOCK - MLA_ROPE), w_in.dtype),
                             w_in[:, dk_lo:dv_lo]], axis=1).astype(BF16)
    w_col = jnp.concatenate([w_in[:, kr_end:dk_lo], w_in[:, dv_lo:]], axis=1).T.astype(BF16)
    pad_q = HEAD_BLOCK - MLA_QK_DIM
    w_uq_p = jnp.pad(w_uq.reshape(MLA_Q_RANK, MLA_HEADS, MLA_QK_DIM), ((0, 0), (0, 0), (0, pad_q)))
    w_uqt = w_uq_p.reshape(MLA_Q_RANK, _QA_W).T.astype(BF16)
    kv = w_ukv.reshape(MLA_KV_RANK, MLA_HEADS, MLA_NOPE + MLA_V)
    w_uk = jnp.pad(kv[:, :, :MLA_NOPE], ((0, 0), (0, 0), (0, HEAD_BLOCK - MLA_NOPE)))
    w_uk = w_uk.reshape(MLA_KV_RANK, _QA_W).astype(BF16)
    w_uvt = kv[:, :, MLA_NOPE:].reshape(MLA_KV_RANK, _VA_W).T.astype(BF16)
    src = jnp.arange(HEAD_BLOCK)[:, None]
    dst = jnp.arange(_QA_W)[None, :]
    place = ((dst % HEAD_BLOCK == src + MLA_NOPE) & (src < MLA_ROPE)).astype(BF16)
    return w_row, w_col, w_uqt, w_uk, w_uvt, place


def _ssd_weights(w_in):
    d = w_in.shape[0]
    z = jnp.zeros((d, LANES - SSD_HEADS), w_in.dtype)
    return jnp.concatenate([w_in[:, :_S_DT], w_in[:, _S_DT:_S_DT + SSD_HEADS], z, w_in[:, _S_DT + SSD_HEADS:], z],
                           axis=1).astype(BF16)


def _lambda_init_for(layer):
    return 0.8 - 0.6 * math.exp(-0.3 * layer)


def _pick_tile(seq, ctx, n_ctx, want):
    tm = want
    while seq % tm or n_ctx % tm:
        tm //= 2
    return tm


def kernel(x, c, ctx, c_ctx, ada_w, ada_b, ln_g, ln_b, ffn1_w_gu, ffn1_w_down, ffn2_w_gu, ffn2_w_down, attn_w_in,
           mla_q_norm_g, mla_w_uq, mla_kv_norm_g, mla_w_ukv, diff_lam_q1, diff_lam_k1, diff_lam_q2, diff_lam_k2,
           diff_subln_g, attn_w_out, ssd_w_in, ssd_conv_w, ssd_conv_b, ssd_a_log, ssd_dt_bias, ssd_d, ssd_norm_g,
           ssd_w_out):
    batch, seq, d = x.shape
    n_ctx_tok = ctx.shape[1]
    geom = _Geom(batch, seq, n_ctx_tok)
    assert d == D_MODEL and batch + 1 <= MOD_ROWS
    assert seq % GRID_W == 0 and seq % SSD_CHUNK == 0 and n_ctx_tok % SSD_CHUNK == 0
    tq = n_ctx_tok
    assert seq % tq == 0
    tm_ffn = _pick_tile(seq, n_ctx_tok, geom.n_ctx, 512)
    tm_proj = _pick_tile(seq, n_ctx_tok, geom.n_ctx, 256)
    assert n_ctx_tok % tm_proj == 0
    fc = 256

    cond = jnp.concatenate([c, c_ctx[None, :], jnp.zeros((MOD_ROWS - batch - 1, d), F32)], axis=0)
    mods = _ada_call(cond, ada_w, ada_b).reshape(DEPTH, MOD_ROWS, N_MOD, d)
    rtab, ttab = _rope_tables(seq, tm_proj)

    h = (x.reshape(batch * seq, d), ctx.reshape(batch * n_ctx_tok, d))
    for l in range(DEPTH):
        last = l == DEPTH - 1
        h = _ffn_call(geom, h, mods, l, 0, ffn1_w_gu[l].astype(BF16), ffn1_w_down[l].astype(BF16),
                      ln_g[l, 0], ln_b[l, 0], geom.n_tok, tm_ffn, fc)
        n_out = geom.n_lat if last else geom.n_tok
        if l % 2 == 0:
            a = l // 2
            wts = _attn_weights(attn_w_in[a], mla_w_uq[a], mla_w_ukv[a])
            qa, ka, va, dq, dk, dv = _attn_in_call(geom, h, mods, l, rtab, ttab, wts, mla_q_norm_g[a].reshape(1, -1),
                                                   mla_kv_norm_g[a].reshape(1, -1), tm_proj)
            lam_vecs = jnp.stack([diff_lam_q1[a], diff_lam_k1[a], diff_lam_q2[a], diff_lam_k2[a]])
            lat, ctx_heads = _attention(geom, qa, ka, va, dq, dk, dv, lam_vecs, diff_subln_g[a].reshape(1, -1),
                                        _lambda_init_for(l), not last, tq)
            h = _attn_out_call(geom, lat, ctx_heads, h, mods, l, attn_w_out[a].astype(BF16), ln_g[l, 1],
                               ln_b[l, 1], tm_ffn)
        else:
            s = l // 2
            z, xbc, dt = _ssd_in_call(geom, h, mods, l, _ssd_weights(ssd_w_in[s]), ssd_conv_w[s], ssd_conv_b[s],
                                      tm_proj)
            y = _scan_call(geom, xbc, dt, ssd_dt_bias[s], ssd_a_log[s])
            dskip = jnp.repeat(ssd_d[s], SSD_HEAD_DIM, axis=-1)
            h = _ssd_out_call(geom, y, xbc, z, dskip, ssd_norm_g[s], h, mods, l, ssd_w_out[s].astype(BF16),
                              ln_g[l, 1], ln_b[l, 1], n_out, tm_ffn)
        h = _ffn_call(geom, h, mods, l, 6, ffn2_w_gu[l].astype(BF16), ffn2_w_down[l].astype(BF16),
                      ln_g[l, 2], ln_b[l, 2], n_out, tm_ffn, fc)
    return h[:geom.n_lat].reshape(batch, seq, d)
```

```python
import functools
import math

import jax
import jax.numpy as jnp
from jax import lax
from jax.experimental import pallas as pl
from jax.experimental.pallas import tpu as pltpu

F32 = jnp.float32
BF16 = jnp.bfloat16

D_MODEL = 1024
DEPTH = 4
GRID_W = 64
N_MOD = 9
FFN_DIM = 2816
MACARON_WEIGHT = 0.5
MLA_HEADS = 8
MLA_Q_RANK = 384
MLA_KV_RANK = 256
MLA_NOPE = 64
MLA_ROPE = 32
MLA_V = 64
MLA_QK_DIM = MLA_NOPE + MLA_ROPE
MLA_SCALE = MLA_QK_DIM ** -0.5
DIFF_HEADS = 4
DIFF_HEAD_DIM = 64
DIFF_WIDTH = DIFF_HEADS * 2 * DIFF_HEAD_DIM
DIFF_SCALE = DIFF_HEAD_DIM ** -0.5
ROPE_BASE = 10000.0
SSD_INNER = 2 * D_MODEL
SSD_HEAD_DIM = 64
SSD_HEADS = SSD_INNER // SSD_HEAD_DIM
SSD_GROUPS = 4
SSD_STATE = 128
SSD_CONV = 5
SSD_CHUNK = 128
SSD_CONV_DIM = SSD_INNER + 2 * SSD_GROUPS * SSD_STATE
DEEPNORM_ALPHA = (2.0 * DEPTH) ** 0.25
LN_EPS = 1e-6
RMS_EPS = 1e-6

LANES = 128
V7X_VMEM_LIMIT = 56 * 1024 * 1024
HALO = 16

HEAD_BLOCK = LANES
MOD_ROWS = 24


def _cparams(sem):
    return pltpu.CompilerParams(dimension_semantics=sem, vmem_limit_bytes=V7X_VMEM_LIMIT)


def _sigmoid(v):
    return 0.5 * jnp.tanh(0.5 * v) + 0.5


def _layer_norm_rows(v, g, b):
    mu = jnp.mean(v, axis=-1, keepdims=True)
    c = v - mu
    var = jnp.mean(c * c, axis=-1, keepdims=True)
    return c * lax.rsqrt(var + LN_EPS) * g + b


def _rms_rows(v, g):
    return v * lax.rsqrt(jnp.mean(v * v, axis=-1, keepdims=True) + RMS_EPS) * g


def _bdot(a, b):
    return jnp.dot(a, b, preferred_element_type=F32)


def _bdot_nt(a, b):
    return lax.dot_general(a, b, (((1,), (1,)), ((), ())), preferred_element_type=F32)


def _ada_kernel(c_ref, w_ref, b_ref, o_ref):
    c = c_ref[...]
    s = (c * _sigmoid(c)).astype(BF16)
    o_ref[...] = _bdot(s, w_ref[...].astype(BF16)) + b_ref[...]


def _ada_call(cond, ada_w, ada_b):
    depth, d, n = ada_w.shape
    tn = n // 8
    return pl.pallas_call(
        _ada_kernel,
        grid=(depth, n // tn),
        in_specs=[
            pl.BlockSpec((MOD_ROWS, d), lambda l, j: (0, 0)),
            pl.BlockSpec((None, d, tn), lambda l, j: (l, 0, j)),
            pl.BlockSpec((None, 1, tn), lambda l, j: (l, 0, j)),
        ],
        out_specs=pl.BlockSpec((None, MOD_ROWS, tn), lambda l, j: (l, 0, j)),
        out_shape=jax.ShapeDtypeStruct((depth, MOD_ROWS, n), F32),
        compiler_params=_cparams(("parallel", "parallel")),
        name="ada_mod",
    )(cond, ada_w, ada_b.reshape(depth, 1, n))


class _Geom:
    def __init__(self, batch, seq, ctx):
        self.batch, self.seq, self.ctx = batch, seq, ctx
        self.n_lat = batch * seq
        self.n_ctx = batch * ctx
        self.n_tok = self.n_lat + self.n_ctx

    def mod_spec(self, layer, tm):
        n_lat_tiles = self.n_lat // tm
        tiles_per_batch = self.seq // tm
        batch = self.batch

        def index(i):
            return (layer, jnp.where(i < n_lat_tiles, i // tiles_per_batch, batch), 0, 0)

        return pl.BlockSpec((None, None, N_MOD, D_MODEL), index)


def _row_spec(tm, width):
    return pl.BlockSpec((tm, width), lambda i: (i, 0))


def _full_spec(shape):
    nd = len(shape)
    return pl.BlockSpec(shape, lambda i: (0,) * nd)


def _ffn_kernel(*refs, k0, fc, n_lat_tiles):
    if n_lat_tiles is None:
        h_ref, mod_ref, wgu_ref, wd_ref, g_ref, b_ref, o_ref, acc_ref = refs
        h = h_ref[...]
    else:
        hx_ref, hc_ref, mod_ref, wgu_ref, wd_ref, g_ref, b_ref, o_ref, acc_ref = refs
        h = jnp.where(pl.program_id(0) < n_lat_tiles, hx_ref[...], hc_ref[...])
    shift = mod_ref[k0:k0 + 1, :]
    scale = mod_ref[k0 + 1:k0 + 2, :]
    gate = mod_ref[k0 + 2:k0 + 3, :]
    t = (h * (1.0 + scale) + shift).astype(BF16)
    for j in range(FFN_DIM // fc):
        gj = _bdot(t, wgu_ref[:, j * fc:(j + 1) * fc])
        uj = _bdot(t, wgu_ref[:, FFN_DIM + j * fc:FFN_DIM + (j + 1) * fc])
        a = (gj * _sigmoid(gj) * uj).astype(BF16)
        y = _bdot(a, wd_ref[j * fc:(j + 1) * fc, :])
        if j == 0:
            acc_ref[...] = y
        else:
            acc_ref[...] += y
    r = (MACARON_WEIGHT * gate) * acc_ref[...]
    o_ref[...] = _layer_norm_rows(DEEPNORM_ALPHA * h + r, g_ref[...], b_ref[...])


def _ffn_call(geom, h, mods, layer, k0, w_gu, w_down, ln_g, ln_b, n_rows, tm, fc):
    if isinstance(h, tuple):
        n_lat_tiles = geom.n_lat // tm
        h_args = h
        h_specs = [pl.BlockSpec((tm, D_MODEL), lambda i: (jnp.minimum(i, n_lat_tiles - 1), 0)),
                   pl.BlockSpec((tm, D_MODEL), lambda i: (jnp.maximum(i - n_lat_tiles, 0), 0))]
    else:
        n_lat_tiles = None
        h_args = (h,)
        h_specs = [_row_spec(tm, D_MODEL)]
    kern = functools.partial(_ffn_kernel, k0=k0, fc=fc, n_lat_tiles=n_lat_tiles)
    return pl.pallas_call(
        kern,
        grid=(n_rows // tm,),
        in_specs=h_specs + [
            geom.mod_spec(layer, tm),
            _full_spec(w_gu.shape),
            _full_spec(w_down.shape),
            _full_spec((1, D_MODEL)),
            _full_spec((1, D_MODEL)),
        ],
        out_specs=_row_spec(tm, D_MODEL),
        out_shape=jax.ShapeDtypeStruct((n_rows, D_MODEL), F32),
        scratch_shapes=[pltpu.VMEM((tm, D_MODEL), F32)],
        compiler_params=_cparams(("parallel",)),
        name="ffn",
    )(*h_args, mods, w_gu, w_down, ln_g.reshape(1, -1), ln_b.reshape(1, -1))


_QA_W = MLA_HEADS * HEAD_BLOCK
_VA_W = MLA_HEADS * MLA_V
_R_CKV = MLA_Q_RANK
_R_KR = MLA_Q_RANK + MLA_KV_RANK
_R_DK = _R_KR + HEAD_BLOCK
ATTN_ROW_COLS = _R_DK + DIFF_WIDTH
_MLA_HALF = MLA_ROPE // 2
_DIFF_HALF = DIFF_HEAD_DIM // 2
_T_SIN_M = _MLA_HALF
_T_COS_D = 2 * _MLA_HALF
_T_SIN_D = _T_COS_D + _DIFF_HALF
ROPE_T_ROWS = _T_SIN_D + _DIFF_HALF
ATT_KC = 256
LOG2E = math.log2(math.e)
MLA_QSCALE = MLA_SCALE * LOG2E
DIFF_QSCALE = DIFF_SCALE * LOG2E


def _rope_block(v, tab_ref, t0, shift):
    left = pltpu.roll(v, LANES - shift, axis=1)
    right = pltpu.roll(v, shift, axis=1)
    return v * tab_ref[t0] + left * tab_ref[t0 + 1] + right * tab_ref[t0 + 2]


def _rope_rows(dst_ref, src, r0, half, cos, sin, scale):
    t1 = src[r0:r0 + half]
    t2 = src[r0 + half:r0 + 2 * half]
    dst_ref[r0:r0 + half, :] = ((t1 * cos - t2 * sin) * scale).astype(dst_ref.dtype)
    dst_ref[r0 + half:r0 + 2 * half, :] = ((t1 * sin + t2 * cos) * scale).astype(dst_ref.dtype)


def _attn_in_kernel(h_ref, mod_ref, rtab_ref, ttab_ref, wrow_ref, wcol_ref, qg_ref, wuqt_ref, kvg_ref, wuk_ref,
                    wuvt_ref, place_ref, qat_ref, ka_ref, vat_ref, dqt_ref, dk_ref, dvt_ref):
    h = h_ref[...]
    u = (h * (1.0 + mod_ref[4:5, :]) + mod_ref[3:4, :]).astype(BF16)
    proj = _bdot(u, wrow_ref[...])
    projt = _bdot_nt(wcol_ref[...], u)

    cqn = _rms_rows(proj[:, :_R_CKV], qg_ref[...]).astype(BF16)
    qt = _bdot_nt(wuqt_ref[...], cqn)
    cos_m = ttab_ref[0:_T_SIN_M, :]
    sin_m = ttab_ref[_T_SIN_M:_T_COS_D, :]
    for hd in range(MLA_HEADS):
        r0 = hd * HEAD_BLOCK
        qat_ref[r0:r0 + MLA_NOPE, :] = (qt[r0:r0 + MLA_NOPE] * MLA_QSCALE).astype(BF16)
        _rope_rows(qat_ref, qt, r0 + MLA_NOPE, _MLA_HALF, cos_m, sin_m, MLA_QSCALE)
        qat_ref[r0 + MLA_QK_DIM:r0 + HEAD_BLOCK, :] = jnp.zeros((HEAD_BLOCK - MLA_QK_DIM, qt.shape[1]), BF16)

    ckvn = _rms_rows(proj[:, _R_CKV:_R_KR], kvg_ref[...]).astype(BF16)
    kr = _rope_block(proj[:, _R_KR:_R_DK], rtab_ref, 0, _MLA_HALF).astype(BF16)
    ka_ref[...] = (_bdot(ckvn, wuk_ref[...]) + _bdot(kr, place_ref[...])).astype(BF16)
    vat_ref[...] = _bdot_nt(wuvt_ref[...], ckvn).astype(BF16)

    cos_d = ttab_ref[_T_COS_D:_T_SIN_D, :]
    sin_d = ttab_ref[_T_SIN_D:ROPE_T_ROWS, :]
    for sub in range(2 * DIFF_HEADS):
        _rope_rows(dqt_ref, projt, sub * DIFF_HEAD_DIM, _DIFF_HALF, cos_d, sin_d, DIFF_QSCALE)
    dvt_ref[...] = projt[DIFF_WIDTH:].astype(BF16)
    for hd in range(DIFF_HEADS):
        sk = slice(_R_DK + hd * HEAD_BLOCK, _R_DK + (hd + 1) * HEAD_BLOCK)
        so = slice(hd * HEAD_BLOCK, (hd + 1) * HEAD_BLOCK)
        dk_ref[:, so] = _rope_block(proj[:, sk], rtab_ref, 3, _DIFF_HALF).astype(BF16)


def _attn_in_call(geom, h, mods, layer, rtab, ttab, wts, q_g, kv_g, tm):
    n_lat_tiles = geom.n_lat // tm
    tiles_per_seq = geom.seq // tm

    def pos_block(i):
        return jnp.where(i < n_lat_tiles, i % tiles_per_seq, tiles_per_seq)

    n = geom.n_tok
    w_row, w_col, w_uqt, w_uk, w_uvt, place = wts

    def tok_major(width):
        return _row_spec(tm, width), jax.ShapeDtypeStruct((n, width), BF16)

    def chan_major(rows):
        return pl.BlockSpec((rows, tm), lambda i: (0, i)), jax.ShapeDtypeStruct((rows, n), BF16)

    def chan_major_chunked(rows):
        return (pl.BlockSpec((None, rows, tm), lambda i: (i, 0, 0)),
                jax.ShapeDtypeStruct((n // tm, rows, tm), BF16))

    assert tm == ATT_KC
    outs = [chan_major(_QA_W), tok_major(_QA_W), chan_major_chunked(_VA_W),
            chan_major(DIFF_WIDTH), tok_major(DIFF_WIDTH), chan_major_chunked(DIFF_WIDTH)]
    return pl.pallas_call(
        _attn_in_kernel,
        grid=(n // tm,),
        in_specs=[
            _row_spec(tm, D_MODEL),
            geom.mod_spec(layer, tm),
            pl.BlockSpec((6, tm, LANES), lambda i: (0, pos_block(i), 0)),
            pl.BlockSpec((ROPE_T_ROWS, tm), lambda i: (0, pos_block(i))),
            _full_spec(w_row.shape),
            _full_spec(w_col.shape),
            _full_spec(q_g.shape),
            _full_spec(w_uqt.shape),
            _full_spec(kv_g.shape),
            _full_spec(w_uk.shape),
            _full_spec(w_uvt.shape),
            _full_spec(place.shape),
        ],
        out_specs=[o[0] for o in outs],
        out_shape=[o[1] for o in outs],
        compiler_params=_cparams(("parallel",)),
        name="attn_in",
    )(h, mods, rtab, ttab, w_row, w_col, q_g, w_uqt, kv_g, w_uk, w_uvt, place)


def _rope_tables(seq, tm):
    rows = seq // GRID_W
    row = jnp.repeat(jnp.arange(rows, dtype=F32), GRID_W)
    col = jnp.tile(jnp.arange(GRID_W, dtype=F32), rows)

    def angles(rot_dim):
        n_freq = rot_dim // 4
        inv = ROPE_BASE ** (-jnp.arange(n_freq, dtype=F32) / n_freq)
        return jnp.concatenate([row[:, None] * inv, col[:, None] * inv], axis=-1)

    def build(ang, first_lo, half, live):
        lane = jnp.arange(LANES)
        cos, sin = jnp.cos(ang), jnp.sin(ang)
        zeros = jnp.zeros((seq, LANES), F32)
        a = jnp.where(lane < live, 1.0, 0.0)[None, :] + zeros
        bm, cm = zeros, zeros
        for lo in first_lo:
            a = a.at[:, lo:lo + half].set(cos).at[:, lo + half:lo + 2 * half].set(cos)
            bm = bm.at[:, lo:lo + half].set(-sin)
            cm = cm.at[:, lo + half:lo + 2 * half].set(sin)
        ident = jnp.where(lane < live, 1.0, 0.0)[None, :] + jnp.zeros((tm, LANES), F32)
        z = jnp.zeros((tm, LANES), F32)
        return [jnp.concatenate([a, ident]), jnp.concatenate([bm, z]), jnp.concatenate([cm, z])]

    ang_m = angles(MLA_ROPE)
    ang_d = angles(DIFF_HEAD_DIM)
    rtab = jnp.stack(build(ang_m, (0,), _MLA_HALF, MLA_ROPE)
                     + build(ang_d, (0, DIFF_HEAD_DIM), _DIFF_HALF, LANES))

    def chan(ang, fn, fill):
        return jnp.concatenate([fn(ang).T, jnp.full((ang.shape[1], tm), fill, F32)], axis=1)

    ttab = jnp.concatenate([chan(ang_m, jnp.cos, 1.0), chan(ang_m, jnp.sin, 0.0),
                            chan(ang_d, jnp.cos, 1.0), chan(ang_d, jnp.sin, 0.0)], axis=0)
    return rtab, ttab


def _softmax_pv_t(qt, keys, vals_t):
    s = [_bdot(k, qt) for k in keys]
    m = s[0].max(axis=0, keepdims=True)
    for si in s[1:]:
        m = jnp.maximum(m, si.max(axis=0, keepdims=True))
    num, den = None, None
    for si, vt in zip(s, vals_t):
        p = jnp.exp2(si - m)
        d = p.sum(axis=0, keepdims=True)
        o = _bdot(vt, p.astype(BF16))
        num = o if num is None else num + o
        den = d if den is None else den + d
    return num / den


def _fold_rows(v, op):
    parts = [v[r:r + 8] for r in range(0, v.shape[0], 8)]
    while len(parts) > 1:
        parts = [op(parts[i], parts[i + 1]) for i in range(0, len(parts) - 1, 2)] + (
            [parts[-1]] if len(parts) % 2 else [])
    return parts[0]


def _attn_pipelined(qi, n_q, q_cur, q_nxt, k_lanes, kx_ref, kc_ref, vx_ref, vc_ref, s_refs, m_ref, acc_ref, o_ref,
                    combine):
    n_maps = len(q_cur)
    tq = q_cur[0].shape[1]
    chunks = ([(kc_ref, vc_ref, c) for c in range(kc_ref.shape[0])]
              + [(kx_ref, vx_ref, c) for c in range(kx_ref.shape[0])])
    neg = jnp.full((8, tq), -jnp.inf, F32)
    zero = jnp.zeros((8, tq), F32)

    def score_chunk(dst_ref, ci, q_list, mrun):
        k_ref, _, c = chunks[ci]
        out = []
        for j in range(n_maps):
            s = _bdot(k_ref[c, :, k_lanes[j]], q_list[j])
            dst_ref[j, ci] = s
            out.append(jnp.maximum(mrun[j], _fold_rows(s, jnp.maximum)))
        return out

    def exp_chunk(src_ref, ci, m_cur, lrun):
        _, v_ref, c = chunks[ci]
        vt = v_ref[c]
        out = []
        for j in range(n_maps):
            p = jnp.exp2(src_ref[j, ci] - m_cur[j])
            out.append(lrun[j] + _fold_rows(p, jnp.add))
            o = _bdot(vt, p.astype(BF16))
            if ci == 0:
                acc_ref[j] = o
            else:
                acc_ref[j] += o
        return out

    @pl.when(qi == 0)
    def _():
        mrun = [neg] * n_maps
        for ci in range(len(chunks)):
            mrun = score_chunk(s_refs[0], ci, q_cur, mrun)
        for j in range(n_maps):
            m_ref[0, j] = mrun[j]

    def step(slot, with_next):
        cur_ref, nxt_ref = s_refs[slot], s_refs[1 - slot]
        m_cur = [jnp.max(m_ref[slot, j], axis=0, keepdims=True) for j in range(n_maps)]
        lrun, mrun = [zero] * n_maps, [neg] * n_maps
        for ci in range(len(chunks)):
            lrun = exp_chunk(cur_ref, ci, m_cur, lrun)
            if with_next:
                mrun = score_chunk(nxt_ref, ci, q_nxt, mrun)
        if with_next:
            for j in range(n_maps):
                m_ref[1 - slot, j] = mrun[j]
        outs = [acc_ref[j] / jnp.sum(lrun[j], axis=0, keepdims=True) for j in range(n_maps)]
        o_ref[...] = combine(outs).astype(o_ref.dtype)

    for slot in (0, 1):
        pl.when(jnp.logical_and(qi % 2 == slot, qi < n_q - 1))(functools.partial(step, slot, True))
    pl.when(qi == n_q - 1)(functools.partial(step, (n_q - 1) % 2, False))


def _mla_maps(q_ref):
    return [q_ref[:HEAD_BLOCK, :], q_ref[HEAD_BLOCK:, :]]


_MLA_K_LANES = (slice(0, HEAD_BLOCK), slice(HEAD_BLOCK, 2 * HEAD_BLOCK))
_DIFF_K_LANES = (slice(0, HEAD_BLOCK), slice(0, HEAD_BLOCK))


def _mla_combine(outs):
    row = lax.broadcasted_iota(jnp.int32, outs[0].shape, 0)
    return jnp.where(row < MLA_V, outs[0], outs[1]).T


def _diff_maps(q_ref):
    q = q_ref[...]
    row = lax.broadcasted_iota(jnp.int32, q.shape, 0)
    zero = jnp.zeros_like(q)
    return [jnp.where(row < DIFF_HEAD_DIM, q, zero), jnp.where(row >= DIFF_HEAD_DIM, q, zero)]


def _diff_lambda(lam_ref, lam_init):
    lv = lam_ref[...]
    return (jnp.exp(jnp.sum(lv[0:1] * lv[1:2], axis=-1, keepdims=True))
            - jnp.exp(jnp.sum(lv[2:3] * lv[3:4], axis=-1, keepdims=True)) + lam_init)


def _diff_combine(outs, lam, lam_init, g):
    return _rms_rows((outs[0] - lam * outs[1]).T, g) * (1.0 - lam_init)


def _mla_lat_kernel(qc_ref, qn_ref, kx_ref, kc_ref, vx_ref, vc_ref, o_ref, s0_ref, s1_ref, m_ref, acc_ref, *, n_q):
    _attn_pipelined(pl.program_id(2), n_q, _mla_maps(qc_ref), _mla_maps(qn_ref), _MLA_K_LANES, kx_ref, kc_ref,
                    vx_ref, vc_ref, (s0_ref, s1_ref), m_ref, acc_ref, o_ref, _mla_combine)


def _diff_lat_kernel(lam_ref, g_ref, qc_ref, qn_ref, kx_ref, kc_ref, vx_ref, vc_ref, o_ref, s0_ref, s1_ref, m_ref,
                     acc_ref, *, n_q, lam_init):
    combine = functools.partial(_diff_combine, lam=_diff_lambda(lam_ref, lam_init), lam_init=lam_init, g=g_ref[...])
    _attn_pipelined(pl.program_id(2), n_q, _diff_maps(qc_ref), _diff_maps(qn_ref), _DIFF_K_LANES, kx_ref, kc_ref,
                    vx_ref, vc_ref, (s0_ref, s1_ref), m_ref, acc_ref, o_ref, combine)


def _ctx_attention(q_maps, k_lanes, kc_ref, vc_ref, grp, qk_w):
    n_cc = kc_ref.shape[0]
    vals = [vc_ref[cc, grp * HEAD_BLOCK:(grp + 1) * HEAD_BLOCK, :] for cc in range(n_cc)]
    outs = []
    for q, lanes in zip(q_maps, k_lanes):
        cols = slice(grp * qk_w + lanes.start, grp * qk_w + lanes.stop)
        outs.append(_softmax_pv_t(q, [kc_ref[cc, :, cols] for cc in range(n_cc)], vals))
    return outs


def _group_rows(q_ref, grp, qk_w):
    return q_ref.at[pl.ds(grp * qk_w, qk_w)]


def _mla_ctx_kernel(q_ref, kc_ref, vc_ref, o_ref):
    qk_w = 2 * HEAD_BLOCK
    for grp in range(MLA_HEADS // 2):
        outs = _ctx_attention(_mla_maps(_group_rows(q_ref, grp, qk_w)), _MLA_K_LANES, kc_ref, vc_ref, grp, qk_w)
        o_ref[:, grp * HEAD_BLOCK:(grp + 1) * HEAD_BLOCK] = _mla_combine(outs).astype(o_ref.dtype)


def _diff_ctx_kernel(lam_ref, g_ref, q_ref, kc_ref, vc_ref, o_ref, *, lam_init):
    lam = _diff_lambda(lam_ref, lam_init)
    for grp in range(DIFF_HEADS):
        outs = _ctx_attention(_diff_maps(_group_rows(q_ref, grp, HEAD_BLOCK)), _DIFF_K_LANES, kc_ref, vc_ref, grp,
                              HEAD_BLOCK)
        o_ref[:, grp * HEAD_BLOCK:(grp + 1) * HEAD_BLOCK] = _diff_combine(outs, lam, lam_init,
                                                                          g_ref[...]).astype(o_ref.dtype)


def _attn_lat_call(geom, kern, name, small, qt, k3, vt3, qk_w, n_groups, tq):
    n_q = geom.seq // tq
    n_lc = geom.seq // ATT_KC
    n_cc = geom.ctx // ATT_KC
    ctx_c0 = geom.n_lat // geom.ctx
    small_specs = [pl.BlockSpec(a.shape, lambda b, p, qi: (0, 0)) for a in small]
    specs = small_specs + [
        pl.BlockSpec((qk_w, tq), lambda b, p, qi: (p, b * n_q + qi)),
        pl.BlockSpec((qk_w, tq), lambda b, p, qi: (p, b * n_q + jnp.minimum(qi + 1, n_q - 1))),
        pl.BlockSpec((n_lc, ATT_KC, qk_w), lambda b, p, qi: (b, 0, p)),
        pl.BlockSpec((n_cc, ATT_KC, qk_w), lambda b, p, qi: (ctx_c0 + b, 0, p)),
        pl.BlockSpec((n_lc, HEAD_BLOCK, ATT_KC), lambda b, p, qi: (b, p, 0)),
        pl.BlockSpec((n_cc, HEAD_BLOCK, ATT_KC), lambda b, p, qi: (ctx_c0 + b, p, 0)),
    ]
    return pl.pallas_call(
        functools.partial(kern, n_q=n_q),
        grid=(geom.batch, n_groups, n_q),
        in_specs=specs,
        out_specs=pl.BlockSpec((tq, HEAD_BLOCK), lambda b, p, qi: (b * n_q + qi, p)),
        out_shape=jax.ShapeDtypeStruct((geom.n_lat, n_groups * HEAD_BLOCK), BF16),
        scratch_shapes=[pltpu.VMEM((2, n_cc + n_lc, ATT_KC, tq), F32),
                        pltpu.VMEM((2, n_cc + n_lc, ATT_KC, tq), F32),
                        pltpu.VMEM((2, 2, 8, tq), F32),
                        pltpu.VMEM((2, HEAD_BLOCK, tq), F32)],
        compiler_params=_cparams(("parallel", "parallel", "arbitrary")),
        name=name,
    )(*small, qt, qt, k3, k3, vt3, vt3)


def _attn_ctx_call(geom, kern, name, small, qt, k3, vt3, qk_w, n_groups):
    n_cc = geom.ctx // ATT_KC
    ctx_c0 = geom.n_lat // geom.ctx
    small_specs = [pl.BlockSpec(a.shape, lambda b: (0, 0)) for a in small]
    specs = small_specs + [
        pl.BlockSpec((n_groups * qk_w, geom.ctx), lambda b: (0, ctx_c0 + b)),
        pl.BlockSpec((n_cc, ATT_KC, n_groups * qk_w), lambda b: (ctx_c0 + b, 0, 0)),
        pl.BlockSpec((n_cc, n_groups * HEAD_BLOCK, ATT_KC), lambda b: (ctx_c0 + b, 0, 0)),
    ]
    return pl.pallas_call(
        kern,
        grid=(geom.batch,),
        in_specs=specs,
        out_specs=pl.BlockSpec((geom.ctx, n_groups * HEAD_BLOCK), lambda b: (b, 0)),
        out_shape=jax.ShapeDtypeStruct((geom.n_ctx, n_groups * HEAD_BLOCK), BF16),
        compiler_params=_cparams(("parallel",)),
        name=name,
    )(*small, qt, k3, vt3)


def _attention(geom, qa, ka, va, dq, dk, dv, lam_vecs, subln_g, lam_init, with_ctx, tq):
    n_chunks = geom.n_tok // ATT_KC
    ka3 = ka.reshape(n_chunks, ATT_KC, _QA_W)
    dk3 = dk.reshape(n_chunks, ATT_KC, DIFF_WIDTH)
    small = [lam_vecs, subln_g]
    lat = (_attn_lat_call(geom, _mla_lat_kernel, "mla_attn", [], qa, ka3, va, 2 * HEAD_BLOCK, MLA_HEADS // 2, tq),
           _attn_lat_call(geom, functools.partial(_diff_lat_kernel, lam_init=lam_init), "diff_attn", small, dq, dk3,
                          dv, HEAD_BLOCK, DIFF_HEADS, tq))
    if not with_ctx:
        return lat, None
    ctx = (_attn_ctx_call(geom, _mla_ctx_kernel, "mla_attn_ctx", [], qa, ka3, va, 2 * HEAD_BLOCK, MLA_HEADS // 2),
           _attn_ctx_call(geom, functools.partial(_diff_ctx_kernel, lam_init=lam_init), "diff_attn_ctx", small, dq,
                          dk3, dv, HEAD_BLOCK, DIFF_HEADS))
    return lat, ctx


def _attn_out_kernel(*refs, n_lat_tiles, with_ctx):
    if with_ctx:
        oa_ref, od_ref, oac_ref, odc_ref, h_ref, mod_ref, w_ref, g_ref, b_ref, o_ref = refs
        is_lat = pl.program_id(0) < n_lat_tiles
        oa = jnp.where(is_lat, oa_ref[...], oac_ref[...])
        od = jnp.where(is_lat, od_ref[...], odc_ref[...])
    else:
        oa_ref, od_ref, h_ref, mod_ref, w_ref, g_ref, b_ref, o_ref = refs
        oa, od = oa_ref[...], od_ref[...]
    y = _bdot(oa, w_ref[:_VA_W, :]) + _bdot(od, w_ref[_VA_W:, :])
    r = mod_ref[5:6, :] * y
    o_ref[...] = _layer_norm_rows(DEEPNORM_ALPHA * h_ref[...] + r, g_ref[...], b_ref[...])


def _attn_out_call(geom, lat, ctx, h, mods, layer, w_out, ln_g, ln_b, tm):
    n_lat_tiles = geom.n_lat // tm
    with_ctx = ctx is not None
    n_rows = geom.n_tok if with_ctx else geom.n_lat
    head_specs = [pl.BlockSpec((tm, _VA_W), lambda i: (jnp.minimum(i, n_lat_tiles - 1), 0)),
                  pl.BlockSpec((tm, DIFF_WIDTH), lambda i: (jnp.minimum(i, n_lat_tiles - 1), 0))]
    heads = list(lat)
    if with_ctx:
        head_specs += [pl.BlockSpec((tm, _VA_W), lambda i: (jnp.maximum(i - n_lat_tiles, 0), 0)),
                       pl.BlockSpec((tm, DIFF_WIDTH), lambda i: (jnp.maximum(i - n_lat_tiles, 0), 0))]
        heads += list(ctx)
    return pl.pallas_call(
        functools.partial(_attn_out_kernel, n_lat_tiles=n_lat_tiles, with_ctx=with_ctx),
        grid=(n_rows // tm,),
        in_specs=head_specs + [
            _row_spec(tm, D_MODEL),
            geom.mod_spec(layer, tm),
            _full_spec(w_out.shape),
            _full_spec((1, D_MODEL)),
            _full_spec((1, D_MODEL)),
        ],
        out_specs=_row_spec(tm, D_MODEL),
        out_shape=jax.ShapeDtypeStruct((n_rows, D_MODEL), F32),
        compiler_params=_cparams(("parallel",)),
        name="attn_out",
    )(*heads, h, mods, w_out, ln_g.reshape(1, -1), ln_b.reshape(1, -1))


SSD_PROJ_COLS = SSD_INNER + SSD_CONV_DIM + 2 * LANES
_S_XBC = SSD_INNER
_S_DT = SSD_INNER + SSD_CONV_DIM


def _ssd_in_kernel(h_ref, hp_ref, hn_ref, mod_ref, w_ref, cw_ref, cb_ref, z_ref, xbc_ref, dt_ref, pad_ref, *,
                   tm, tiles_per_seq, n_lat_tiles, nc):
    i = pl.program_id(0)
    is_lat = i < n_lat_tiles
    pos = i % tiles_per_seq
    has_prev = jnp.logical_and(is_lat, pos > 0)
    has_next = jnp.logical_and(is_lat, pos < tiles_per_seq - 1)
    scale1 = 1.0 + mod_ref[4:5, :]
    shift = mod_ref[3:4, :]
    u = (h_ref[...] * scale1 + shift).astype(BF16)
    u_ext = jnp.concatenate([(hp_ref[...] * scale1 + shift).astype(BF16), u,
                             (hn_ref[...] * scale1 + shift).astype(BF16)], axis=0)
    for lo in range(0, SSD_INNER, nc):
        z_ref[:, lo:lo + nc] = _bdot(u, w_ref[:, lo:lo + nc]).astype(BF16)
    dt = _bdot(u, w_ref[:, _S_DT:])
    dt_ref[0] = dt[:, :LANES]
    dt_ref[1] = dt[:, LANES:]
    half = SSD_CONV // 2
    for lo in range(0, SSD_CONV_DIM, nc):
        cs = slice(lo, lo + nc)
        r = _bdot(u_ext, w_ref[:, _S_XBC + lo:_S_XBC + lo + nc])
        head, tail = r[:HALO], r[HALO + tm:]
        r = jnp.concatenate([jnp.where(has_prev, head, jnp.zeros_like(head)), r[HALO:HALO + tm],
                             jnp.where(has_next, tail, jnp.zeros_like(tail))], axis=0)
        acc = cb_ref[:, cs] + jnp.zeros((tm, nc), F32)
        for k in range(SSD_CONV):
            sh = r if k == half else pltpu.roll(r, (half - k) % r.shape[0], axis=0)
            acc = acc + sh[HALO:HALO + tm] * cw_ref[k:k + 1, cs]
        xbc_ref[:, cs] = (acc * _sigmoid(acc)).astype(BF16)


def _ssd_in_call(geom, h, mods, layer, w_in, conv_w, conv_b, tm):
    n = geom.n_tok
    nc = 512
    hb = tm // HALO
    last_hb = n // HALO - 1
    kern = functools.partial(_ssd_in_kernel, tm=tm, tiles_per_seq=geom.seq // tm, n_lat_tiles=geom.n_lat // tm, nc=nc)
    return pl.pallas_call(
        kern,
        grid=(n // tm,),
        in_specs=[
            _row_spec(tm, D_MODEL),
            pl.BlockSpec((HALO, D_MODEL), lambda i: (jnp.maximum(i * hb - 1, 0), 0)),
            pl.BlockSpec((HALO, D_MODEL), lambda i: (jnp.minimum((i + 1) * hb, last_hb), 0)),
            geom.mod_spec(layer, tm),
            _full_spec(w_in.shape),
            _full_spec(conv_w.shape),
            _full_spec((1, SSD_CONV_DIM)),
        ],
        out_specs=[_row_spec(tm, SSD_INNER), _row_spec(tm, SSD_CONV_DIM),
                   pl.BlockSpec((2, tm, LANES), lambda i: (0, i, 0))],
        out_shape=[jax.ShapeDtypeStruct((n, SSD_INNER), BF16), jax.ShapeDtypeStruct((n, SSD_CONV_DIM), BF16),
                   jax.ShapeDtypeStruct((2, n, LANES), F32)],
        scratch_shapes=[pltpu.VMEM((tm + 2 * HALO, nc), F32)],
        compiler_params=_cparams(("parallel",)),
        name="ssd_in",
    )(h, h, h, mods, w_in, conv_w, conv_b.reshape(1, -1))


_GROUP_W = SSD_INNER // SSD_GROUPS
_HEADS_PER_GROUP = SSD_HEADS // SSD_GROUPS
_B_COL = SSD_INNER
_C_COL = SSD_INNER + SSD_GROUPS * SSD_STATE
SCAN_CPS = 2


def _scan_kernel(xf_ref, xb_ref, dtf_ref, dtb_ref, bias_ref, alog_ref, expand_ref, yf_ref, yb_ref, state_ref):
    @pl.when(pl.program_id(1) == 0)
    def _():
        state_ref[...] = jnp.zeros_like(state_ref)

    for k in range(SCAN_CPS):
        rf = pl.ds(k * SSD_CHUNK, SSD_CHUNK)
        rb = pl.ds((SCAN_CPS - 1 - k) * SSD_CHUNK, SSD_CHUNK)
        _scan_chunk(0, xf_ref.at[rf], dtf_ref.at[rf], bias_ref[0], alog_ref[0], expand_ref, yf_ref.at[rf],
                    state_ref.at[0])
        _scan_chunk(1, xb_ref.at[rb], dtb_ref.at[rb], bias_ref[1], alog_ref[1], expand_ref, yb_ref.at[rb],
                    state_ref.at[1])


def _scan_chunk(d, xbc_ref, dt_ref, bias, alog, expand_ref, y_ref, state_ref):
    t = SSD_CHUNK
    lane = lax.broadcasted_iota(jnp.int32, (1, LANES), 1)
    a = jnp.where(lane < SSD_HEADS, -jnp.exp(alog), 0.0)
    raw = dt_ref[...] + bias
    e = jnp.exp(-jnp.abs(raw))
    u = 1.0 + e
    um1 = u - 1.0
    dt = jnp.maximum(raw, 0.0) + jnp.where(um1 == 0.0, e, jnp.log(u) * (e / jnp.where(um1 == 0.0, 1.0, um1)))
    da = dt * a
    row = lax.broadcasted_iota(jnp.int32, (t, t), 0)
    col = lax.broadcasted_iota(jnp.int32, (t, t), 1)
    mask = (col <= row) if d == 0 else (col >= row)
    tri = mask.astype(BF16)
    da_hi = da.astype(BF16)
    rem = da - da_hi.astype(F32)
    da_mid = rem.astype(BF16)
    da_lo = (rem - da_mid.astype(F32)).astype(BF16)
    cum = _bdot(tri, da_hi) + _bdot(tri, da_mid) + _bdot(tri, da_lo)
    total = jnp.sum(da, axis=0, keepdims=True)
    src_t = (cum - jnp.log(dt)).T
    ecum = jnp.exp(cum)
    dtdec = dt * jnp.exp(total - cum)
    cdec = jnp.broadcast_to(jnp.exp(total), (8, LANES))

    per_head = jnp.concatenate([dtdec, ecum, cdec], axis=0).astype(BF16)

    lane_t = lax.broadcasted_iota(jnp.int32, (t, LANES), 1)
    mask_bias = jnp.where(mask, 0.0, -jnp.inf)
    zero_x = jnp.zeros((t, LANES), BF16)
    for g in range(SSD_GROUPS):
        gs = slice(g * _GROUP_W, (g + 1) * _GROUP_W)
        ex = _bdot(per_head, expand_ref[:, gs])
        w_state, w_off, w_carry = ex[0:t], ex[t:2 * t], ex[2 * t:2 * t + 1]
        b_g = xbc_ref[:, _B_COL + g * SSD_STATE:_B_COL + (g + 1) * SSD_STATE]
        c_g = xbc_ref[:, _C_COL + g * SSD_STATE:_C_COL + (g + 1) * SSD_STATE]
        cb = _bdot_nt(c_g, b_g)
        st = state_ref[g]
        y_off = _bdot(c_g, st.astype(BF16)) * w_off
        for pr in range(_HEADS_PER_GROUP // 2):
            xs = slice(g * _GROUP_W + pr * LANES, g * _GROUP_W + (pr + 1) * LANES)
            x_pair = xbc_ref[:, xs]
            ms = []
            for e in range(2):
                hd = g * _HEADS_PER_GROUP + 2 * pr + e
                seg = cum[:, hd:hd + 1] - src_t[hd:hd + 1, :]
                ms.append((cb * jnp.exp(seg + mask_bias)).astype(BF16))
            x_blk = jnp.concatenate([jnp.where(lane_t < SSD_HEAD_DIM, x_pair, zero_x),
                                     jnp.where(lane_t >= SSD_HEAD_DIM, x_pair, zero_x)], axis=0)
            y_pair = _bdot(jnp.concatenate(ms, axis=1), x_blk)
            y_ref[:, xs] = (y_pair + y_off[:, pr * LANES:(pr + 1) * LANES]).astype(y_ref.dtype)
        xw = (xbc_ref[:, gs].astype(F32) * w_state).astype(BF16)
        b_t = b_g.astype(F32).T.astype(BF16)
        state_ref[g] = st * w_carry + _bdot(b_t, xw)


def _scan_call(geom, xbc, dt, dt_bias, a_log):
    t = SCAN_CPS * SSD_CHUNK
    assert geom.ctx % t == 0 and geom.seq % t == 0
    ncc = geom.ctx // t
    nlc = geom.seq // t
    ctx0 = geom.n_lat // t

    def row_block(b, d, s):
        jc = s + d * (ncc - 1 - 2 * s)
        sl = s - ncc
        jl = sl + d * (nlc - 1 - 2 * sl)
        return jnp.where(s < ncc, ctx0 + b * ncc + jc, b * nlc + jl)

    def pad_lanes(v):
        return jnp.pad(v.astype(F32), ((0, 0), (0, LANES - v.shape[-1]))).reshape(2, 1, LANES)

    expand = (jnp.arange(SSD_INNER)[None, :] // SSD_HEAD_DIM == jnp.arange(LANES)[:, None]).astype(BF16)

    y_shape = jax.ShapeDtypeStruct((geom.n_tok, SSD_INNER), BF16)
    return pl.pallas_call(
        _scan_kernel,
        grid=(geom.batch, ncc + nlc),
        in_specs=[
            pl.BlockSpec((t, SSD_CONV_DIM), lambda b, s: (row_block(b, 0, s), 0)),
            pl.BlockSpec((t, SSD_CONV_DIM), lambda b, s: (row_block(b, 1, s), 0)),
            pl.BlockSpec((None, t, LANES), lambda b, s: (0, row_block(b, 0, s), 0)),
            pl.BlockSpec((None, t, LANES), lambda b, s: (1, row_block(b, 1, s), 0)),
            pl.BlockSpec((2, 1, LANES), lambda b, s: (0, 0, 0)),
            pl.BlockSpec((2, 1, LANES), lambda b, s: (0, 0, 0)),
            pl.BlockSpec((LANES, SSD_INNER), lambda b, s: (0, 0)),
        ],
        out_specs=[pl.BlockSpec((t, SSD_INNER), lambda b, s: (row_block(b, 0, s), 0)),
                   pl.BlockSpec((t, SSD_INNER), lambda b, s: (row_block(b, 1, s), 0))],
        out_shape=[y_shape, y_shape],
        scratch_shapes=[pltpu.VMEM((2, SSD_GROUPS, SSD_STATE, _GROUP_W), F32)],
        compiler_params=_cparams(("parallel", "arbitrary")),
        name="ssd_scan",
    )(xbc, xbc, dt, dt, pad_lanes(dt_bias), pad_lanes(a_log), expand)


def _ssd_out_kernel(yf_ref, yb_ref, xbc_ref, z_ref, dskip_ref, ng_ref, h_ref, mod_ref, w_ref, g_ref, b_ref, o_ref):
    dsk = dskip_ref[0:1, :] + dskip_ref[1:2, :]
    y = None
    for g in range(SSD_GROUPS):
        gs = slice(g * _GROUP_W, (g + 1) * _GROUP_W)
        z = z_ref[:, gs].astype(F32)
        ysum = yf_ref[:, gs].astype(F32) + yb_ref[:, gs].astype(F32) + dsk[:, gs] * xbc_ref[:, gs].astype(F32)
        gy = ysum * (z * _sigmoid(z))
        gy = _rms_rows(gy, ng_ref[:, gs]).astype(BF16)
        part = _bdot(gy, w_ref[gs, :])
        y = part if y is None else y + part
    r = mod_ref[5:6, :] * y
    o_ref[...] = _layer_norm_rows(DEEPNORM_ALPHA * h_ref[...] + r, g_ref[...], b_ref[...])


def _ssd_out_call(geom, y, xbc, z, dskip, norm_g, h, mods, layer, w_out, ln_g, ln_b, n_rows, tm):
    return pl.pallas_call(
        _ssd_out_kernel,
        grid=(n_rows // tm,),
        in_specs=[
            _row_spec(tm, SSD_INNER),
            _row_spec(tm, SSD_INNER),
            _row_spec(tm, SSD_INNER),
            _row_spec(tm, SSD_INNER),
            _full_spec(dskip.shape),
            _full_spec((1, SSD_INNER)),
            _row_spec(tm, D_MODEL),
            geom.mod_spec(layer, tm),
            _full_spec(w_out.shape),
            _full_spec((1, D_MODEL)),
            _full_spec((1, D_MODEL)),
        ],
        out_specs=_row_spec(tm, D_MODEL),
        out_shape=jax.ShapeDtypeStruct((n_rows, D_MODEL), F32),
        compiler_params=_cparams(("parallel",)),
        name="ssd_out",
    )(y[0], y[1], xbc, z, dskip, norm_g.reshape(1, -1), h, mods, w_out, ln_g.reshape(1, -1), ln_b.reshape(1, -1))


def _attn_weights(w_in, w_uq, w_ukv):
    d = w_in.shape[0]
    kr_end = MLA_Q_RANK + MLA_KV_RANK + MLA_ROPE
    dk_lo = kr_end + DIFF_WIDTH
    dv_lo = dk_lo + DIFF_WIDTH
    w_row = jnp.concatenate([w_in[:, :kr_end], jnp.zeros((d, HEAD_BLOCK - MLA_ROPE), w_in.dtype),
                             w_in[:, dk_lo:dv_lo]], axis=1).astype(BF16)
    w_col = jnp.concatenate([w_in[:, kr_end:dk_lo], w_in[:, dv_lo:]], axis=1).T.astype(BF16)
    pad_q = HEAD_BLOCK - MLA_QK_DIM
    w_uq_p = jnp.pad(w_uq.reshape(MLA_Q_RANK, MLA_HEADS, MLA_QK_DIM), ((0, 0), (0, 0), (0, pad_q)))
    w_uqt = w_uq_p.reshape(MLA_Q_RANK, _QA_W).T.astype(BF16)
    kv = w_ukv.reshape(MLA_KV_RANK, MLA_HEADS, MLA_NOPE + MLA_V)
    w_uk = jnp.pad(kv[:, :, :MLA_NOPE], ((0, 0), (0, 0), (0, HEAD_BLOCK - MLA_NOPE)))
    w_uk = w_uk.reshape(MLA_KV_RANK, _QA_W).astype(BF16)
    w_uvt = kv[:, :, MLA_NOPE:].reshape(MLA_KV_RANK, _VA_W).T.astype(BF16)
    src = jnp.arange(HEAD_BLOCK)[:, None]
    dst = jnp.arange(_QA_W)[None, :]
    place = ((dst % HEAD_BLOCK == src + MLA_NOPE) & (src < MLA_ROPE)).astype(BF16)
    return w_row, w_col, w_uqt, w_uk, w_uvt, place


def _ssd_weights(w_in):
    d = w_in.shape[0]
    z = jnp.zeros((d, LANES - SSD_HEADS), w_in.dtype)
    return jnp.concatenate([w_in[:, :_S_DT], w_in[:, _S_DT:_S_DT + SSD_HEADS], z, w_in[:, _S_DT + SSD_HEADS:], z],
                           axis=1).astype(BF16)


def _lambda_init_for(layer):
    return 0.8 - 0.6 * math.exp(-0.3 * layer)


def _pick_tile(seq, ctx, n_ctx, want):
    tm = want
    while seq % tm or n_ctx % tm:
        tm //= 2
    return tm


def kernel(x, c, ctx, c_ctx, ada_w, ada_b, ln_g, ln_b, ffn1_w_gu, ffn1_w_down, ffn2_w_gu, ffn2_w_down, attn_w_in,
           mla_q_norm_g, mla_w_uq, mla_kv_norm_g, mla_w_ukv, diff_lam_q1, diff_lam_k1, diff_lam_q2, diff_lam_k2,
           diff_subln_g, attn_w_out, ssd_w_in, ssd_conv_w, ssd_conv_b, ssd_a_log, ssd_dt_bias, ssd_d, ssd_norm_g,
           ssd_w_out):
    batch, seq, d = x.shape
    n_ctx_tok = ctx.shape[1]
    geom = _Geom(batch, seq, n_ctx_tok)
    assert d == D_MODEL and batch + 1 <= MOD_ROWS
    assert seq % GRID_W == 0 and seq % SSD_CHUNK == 0 and n_ctx_tok % SSD_CHUNK == 0
    tq = n_ctx_tok
    assert seq % tq == 0
    tm_ffn = _pick_tile(seq, n_ctx_tok, geom.n_ctx, 512)
    tm_proj = _pick_tile(seq, n_ctx_tok, geom.n_ctx, 256)
    assert n_ctx_tok % tm_proj == 0
    fc = 256

    cond = jnp.concatenate([c, c_ctx[None, :], jnp.zeros((MOD_ROWS - batch - 1, d), F32)], axis=0)
    mods = _ada_call(cond, ada_w, ada_b).reshape(DEPTH, MOD_ROWS, N_MOD, d)
    rtab, ttab = _rope_tables(seq, tm_proj)

    h = (x.reshape(batch * seq, d), ctx.reshape(batch * n_ctx_tok, d))
    for l in range(DEPTH):
        last = l == DEPTH - 1
        h = _ffn_call(geom, h, mods, l, 0, ffn1_w_gu[l].astype(BF16), ffn1_w_down[l].astype(BF16),
                      ln_g[l, 0], ln_b[l, 0], geom.n_tok, tm_ffn, fc)
        n_out = geom.n_lat if last else geom.n_tok
        if l % 2 == 0:
            a = l // 2
            wts = _attn_weights(attn_w_in[a], mla_w_uq[a], mla_w_ukv[a])
            qa, ka, va, dq, dk, dv = _attn_in_call(geom, h, mods, l, rtab, ttab, wts, mla_q_norm_g[a].reshape(1, -1),
                                                   mla_kv_norm_g[a].reshape(1, -1), tm_proj)
            lam_vecs = jnp.stack([diff_lam_q1[a], diff_lam_k1[a], diff_lam_q2[a], diff_lam_k2[a]])
            lat, ctx_heads = _attention(geom, qa, ka, va, dq, dk, dv, lam_vecs, diff_subln_g[a].reshape(1, -1),
                                        _lambda_init_for(l), not last, tq)
            h = _attn_out_call(geom, lat, ctx_heads, h, mods, l, attn_w_out[a].astype(BF16), ln_g[l, 1],
                               ln_b[l, 1], tm_ffn)
        else:
            s = l // 2
            z, xbc, dt = _ssd_in_call(geom, h, mods, l, _ssd_weights(ssd_w_in[s]), ssd_conv_w[s], ssd_conv_b[s],
                                      tm_proj)
            y = _scan_call(geom, xbc, dt, ssd_dt_bias[s], ssd_a_log[s])
            dskip = jnp.repeat(ssd_d[s], SSD_HEAD_DIM, axis=-1)
            h = _ssd_out_call(geom, y, xbc, z, dskip, ssd_norm_g[s], h, mods, l, ssd_w_out[s].astype(BF16),
                              ln_g[l, 1], ln_b[l, 1], n_out, tm_ffn)
        h = _ffn_call(geom, h, mods, l, 6, ffn2_w_gu[l].astype(BF16), ffn2_w_down[l].astype(BF16),
                      ln_g[l, 2], ln_b[l, 2], n_out, tm_ffn, fc)
    return h[:geom.n_lat].reshape(batch, seq, d)
```

```python
import functools
import math

import jax
import jax.numpy as jnp
from jax import lax
from jax.experimental import pallas as pl
from jax.experimental.pallas import tpu as pltpu

F32 = jnp.float32
BF16 = jnp.bfloat16

D_MODEL = 1024
DEPTH = 4
GRID_W = 64
N_MOD = 9
FFN_DIM = 2816
MACARON_WEIGHT = 0.5
MLA_HEADS = 8
MLA_Q_RANK = 384
MLA_KV_RANK = 256
MLA_NOPE = 64
MLA_ROPE = 32
MLA_V = 64
MLA_QK_DIM = MLA_NOPE + MLA_ROPE
MLA_SCALE = MLA_QK_DIM ** -0.5
DIFF_HEADS = 4
DIFF_HEAD_DIM = 64
DIFF_WIDTH = DIFF_HEADS * 2 * DIFF_HEAD_DIM
DIFF_SCALE = DIFF_HEAD_DIM ** -0.5
ROPE_BASE = 10000.0
SSD_INNER = 2 * D_MODEL
SSD_HEAD_DIM = 64
SSD_HEADS = SSD_INNER // SSD_HEAD_DIM
SSD_GROUPS = 4
SSD_STATE = 128
SSD_CONV = 5
SSD_CHUNK = 128
SSD_CONV_DIM = SSD_INNER + 2 * SSD_GROUPS * SSD_STATE
DEEPNORM_ALPHA = (2.0 * DEPTH) ** 0.25
LN_EPS = 1e-6
RMS_EPS = 1e-6

LANES = 128
V7X_VMEM_LIMIT = 56 * 1024 * 1024
HALO = 16

HEAD_BLOCK = LANES
MOD_ROWS = 24


def _cparams(sem):
    return pltpu.CompilerParams(dimension_semantics=sem, vmem_limit_bytes=V7X_VMEM_LIMIT)


def _sigmoid(v):
    return 0.5 * jnp.tanh(0.5 * v) + 0.5


def _layer_norm_rows(v, g, b):
    mu = jnp.mean(v, axis=-1, keepdims=True)
    c = v - mu
    var = jnp.mean(c * c, axis=-1, keepdims=True)
    return c * lax.rsqrt(var + LN_EPS) * g + b


def _rms_rows(v, g):
    return v * lax.rsqrt(jnp.mean(v * v, axis=-1, keepdims=True) + RMS_EPS) * g


def _bdot(a, b):
    return jnp.dot(a, b, preferred_element_type=F32)


def _bdot_nt(a, b):
    return lax.dot_general(a, b, (((1,), (1,)), ((), ())), preferred_element_type=F32)


def _ada_kernel(c_ref, w_ref, b_ref, o_ref):
    c = c_ref[...]
    s = (c * _sigmoid(c)).astype(BF16)
    o_ref[...] = _bdot(s, w_ref[...].astype(BF16)) + b_ref[...]


def _ada_call(cond, ada_w, ada_b):
    depth, d, n = ada_w.shape
    tn = n // 8
    return pl.pallas_call(
        _ada_kernel,
        grid=(depth, n // tn),
        in_specs=[
            pl.BlockSpec((MOD_ROWS, d), lambda l, j: (0, 0)),
            pl.BlockSpec((None, d, tn), lambda l, j: (l, 0, j)),
            pl.BlockSpec((None, 1, tn), lambda l, j: (l, 0, j)),
        ],
        out_specs=pl.BlockSpec((None, MOD_ROWS, tn), lambda l, j: (l, 0, j)),
        out_shape=jax.ShapeDtypeStruct((depth, MOD_ROWS, n), F32),
        compiler_params=_cparams(("parallel", "parallel")),
        name="ada_mod",
    )(cond, ada_w, ada_b.reshape(depth, 1, n))


class _Geom:
    def __init__(self, batch, seq, ctx):
        self.batch, self.seq, self.ctx = batch, seq, ctx
        self.n_lat = batch * seq
        self.n_ctx = batch * ctx
        self.n_tok = self.n_lat + self.n_ctx

    def mod_spec(self, layer, tm):
        n_lat_tiles = self.n_lat // tm
        tiles_per_batch = self.seq // tm
        batch = self.batch

        def index(i):
            return (layer, jnp.where(i < n_lat_tiles, i // tiles_per_batch, batch), 0, 0)

        return pl.BlockSpec((None, None, N_MOD, D_MODEL), index)


def _row_spec(tm, width):
    return pl.BlockSpec((tm, width), lambda i: (i, 0))


def _full_spec(shape):
    nd = len(shape)
    return pl.BlockSpec(shape, lambda i: (0,) * nd)


def _ffn_kernel(*refs, k0, fc, n_lat_tiles):
    if n_lat_tiles is None:
        h_ref, mod_ref, wgu_ref, wd_ref, g_ref, b_ref, o_ref, acc_ref = refs
        h = h_ref[...]
    else:
        hx_ref, hc_ref, mod_ref, wgu_ref, wd_ref, g_ref, b_ref, o_ref, acc_ref = refs
        h = jnp.where(pl.program_id(0) < n_lat_tiles, hx_ref[...], hc_ref[...])
    shift = mod_ref[k0:k0 + 1, :]
    scale = mod_ref[k0 + 1:k0 + 2, :]
    gate = mod_ref[k0 + 2:k0 + 3, :]
    t = (h * (1.0 + scale) + shift).astype(BF16)
    for j in range(FFN_DIM // fc):
        gj = _bdot(t, wgu_ref[:, j * fc:(j + 1) * fc])
        uj = _bdot(t, wgu_ref[:, FFN_DIM + j * fc:FFN_DIM + (j + 1) * fc])
        a = (gj * _sigmoid(gj) * uj).astype(BF16)
        y = _bdot(a, wd_ref[j * fc:(j + 1) * fc, :])
        if j == 0:
            acc_ref[...] = y
        else:
            acc_ref[...] += y
    r = (MACARON_WEIGHT * gate) * acc_ref[...]
    o_ref[...] = _layer_norm_rows(DEEPNORM_ALPHA * h + r, g_ref[...], b_ref[...])


def _ffn_call(geom, h, mods, layer, k0, w_gu, w_down, ln_g, ln_b, n_rows, tm, fc):
    if isinstance(h, tuple):
        n_lat_tiles = geom.n_lat // tm
        h_args = h
        h_specs = [pl.BlockSpec((tm, D_MODEL), lambda i: (jnp.minimum(i, n_lat_tiles - 1), 0)),
                   pl.BlockSpec((tm, D_MODEL), lambda i: (jnp.maximum(i - n_lat_tiles, 0), 0))]
    else:
        n_lat_tiles = None
        h_args = (h,)
        h_specs = [_row_spec(tm, D_MODEL)]
    kern = functools.partial(_ffn_kernel, k0=k0, fc=fc, n_lat_tiles=n_lat_tiles)
    return pl.pallas_call(
        kern,
        grid=(n_rows // tm,),
        in_specs=h_specs + [
            geom.mod_spec(layer, tm),
            _full_spec(w_gu.shape),
            _full_spec(w_down.shape),
            _full_spec((1, D_MODEL)),
            _full_spec((1, D_MODEL)),
        ],
        out_specs=_row_spec(tm, D_MODEL),
        out_shape=jax.ShapeDtypeStruct((n_rows, D_MODEL), F32),
        scratch_shapes=[pltpu.VMEM((tm, D_MODEL), F32)],
        compiler_params=_cparams(("parallel",)),
        name="ffn",
    )(*h_args, mods, w_gu, w_down, ln_g.reshape(1, -1), ln_b.reshape(1, -1))


_QA_W = MLA_HEADS * HEAD_BLOCK
_VA_W = MLA_HEADS * MLA_V
_R_CKV = MLA_Q_RANK
_R_KR = MLA_Q_RANK + MLA_KV_RANK
_R_DK = _R_KR + HEAD_BLOCK
ATTN_ROW_COLS = _R_DK + DIFF_WIDTH
_MLA_HALF = MLA_ROPE // 2
_DIFF_HALF = DIFF_HEAD_DIM // 2
_T_SIN_M = _MLA_HALF
_T_COS_D = 2 * _MLA_HALF
_T_SIN_D = _T_COS_D + _DIFF_HALF
ROPE_T_ROWS = _T_SIN_D + _DIFF_HALF
ATT_KC = 256
LOG2E = math.log2(math.e)
MLA_QSCALE = MLA_SCALE * LOG2E
DIFF_QSCALE = DIFF_SCALE * LOG2E


def _rope_block(v, tab_ref, t0, shift):
    left = pltpu.roll(v, LANES - shift, axis=1)
    right = pltpu.roll(v, shift, axis=1)
    return v * tab_ref[t0] + left * tab_ref[t0 + 1] + right * tab_ref[t0 + 2]


def _rope_rows(dst_ref, src, r0, half, cos, sin, scale):
    t1 = src[r0:r0 + half]
    t2 = src[r0 + half:r0 + 2 * half]
    dst_ref[r0:r0 + half, :] = ((t1 * cos - t2 * sin) * scale).astype(dst_ref.dtype)
    dst_ref[r0 + half:r0 + 2 * half, :] = ((t1 * sin + t2 * cos) * scale).astype(dst_ref.dtype)


def _attn_in_kernel(h_ref, mod_ref, rtab_ref, ttab_ref, wrow_ref, wcol_ref, qg_ref, wuqt_ref, kvg_ref, wuk_ref,
                    wuvt_ref, place_ref, qat_ref, ka_ref, vat_ref, dqt_ref, dk_ref, dvt_ref):
    h = h_ref[...]
    u = (h * (1.0 + mod_ref[4:5, :]) + mod_ref[3:4, :]).astype(BF16)
    proj = _bdot(u, wrow_ref[...])
    projt = _bdot_nt(wcol_ref[...], u)

    cqn = _rms_rows(proj[:, :_R_CKV], qg_ref[...]).astype(BF16)
    qt = _bdot_nt(wuqt_ref[...], cqn)
    cos_m = ttab_ref[0:_T_SIN_M, :]
    sin_m = ttab_ref[_T_SIN_M:_T_COS_D, :]
    for hd in range(MLA_HEADS):
        r0 = hd * HEAD_BLOCK
        qat_ref[r0:r0 + MLA_NOPE, :] = (qt[r0:r0 + MLA_NOPE] * MLA_QSCALE).astype(BF16)
        _rope_rows(qat_ref, qt, r0 + MLA_NOPE, _MLA_HALF, cos_m, sin_m, MLA_QSCALE)
        qat_ref[r0 + MLA_QK_DIM:r0 + HEAD_BLOCK, :] = jnp.zeros((HEAD_BLOCK - MLA_QK_DIM, qt.shape[1]), BF16)

    ckvn = _rms_rows(proj[:, _R_CKV:_R_KR], kvg_ref[...]).astype(BF16)
    kr = _rope_block(proj[:, _R_KR:_R_DK], rtab_ref, 0, _MLA_HALF).astype(BF16)
    ka_ref[...] = (_bdot(ckvn, wuk_ref[...]) + _bdot(kr, place_ref[...])).astype(BF16)
    vat_ref[...] = _bdot_nt(wuvt_ref[...], ckvn).astype(BF16)

    cos_d = ttab_ref[_T_COS_D:_T_SIN_D, :]
    sin_d = ttab_ref[_T_SIN_D:ROPE_T_ROWS, :]
    for sub in range(2 * DIFF_HEADS):
        _rope_rows(dqt_ref, projt, sub * DIFF_HEAD_DIM, _DIFF_HALF, cos_d, sin_d, DIFF_QSCALE)
    dvt_ref[...] = projt[DIFF_WIDTH:].astype(BF16)
    for hd in range(DIFF_HEADS):
        sk = slice(_R_DK + hd * HEAD_BLOCK, _R_DK + (hd + 1) * HEAD_BLOCK)
        so = slice(hd * HEAD_BLOCK, (hd + 1) * HEAD_BLOCK)
        dk_ref[:, so] = _rope_block(proj[:, sk], rtab_ref, 3, _DIFF_HALF).astype(BF16)


def _attn_in_call(geom, h, mods, layer, rtab, ttab, wts, q_g, kv_g, tm):
    n_lat_tiles = geom.n_lat // tm
    tiles_per_seq = geom.seq // tm

    def pos_block(i):
        return jnp.where(i < n_lat_tiles, i % tiles_per_seq, tiles_per_seq)

    n = geom.n_tok
    w_row, w_col, w_uqt, w_uk, w_uvt, place = wts

    def tok_major(width):
        return _row_spec(tm, width), jax.ShapeDtypeStruct((n, width), BF16)

    def chan_major(rows):
        return pl.BlockSpec((rows, tm), lambda i: (0, i)), jax.ShapeDtypeStruct((rows, n), BF16)

    def chan_major_chunked(rows):
        return (pl.BlockSpec((None, rows, tm), lambda i: (i, 0, 0)),
                jax.ShapeDtypeStruct((n // tm, rows, tm), BF16))

    assert tm == ATT_KC
    outs = [chan_major(_QA_W), tok_major(_QA_W), chan_major_chunked(_VA_W),
            chan_major(DIFF_WIDTH), tok_major(DIFF_WIDTH), chan_major_chunked(DIFF_WIDTH)]
    return pl.pallas_call(
        _attn_in_kernel,
        grid=(n // tm,),
        in_specs=[
            _row_spec(tm, D_MODEL),
            geom.mod_spec(layer, tm),
            pl.BlockSpec((6, tm, LANES), lambda i: (0, pos_block(i), 0)),
            pl.BlockSpec((ROPE_T_ROWS, tm), lambda i: (0, pos_block(i))),
            _full_spec(w_row.shape),
            _full_spec(w_col.shape),
            _full_spec(q_g.shape),
            _full_spec(w_uqt.shape),
            _full_spec(kv_g.shape),
            _full_spec(w_uk.shape),
            _full_spec(w_uvt.shape),
            _full_spec(place.shape),
        ],
        out_specs=[o[0] for o in outs],
        out_shape=[o[1] for o in outs],
        compiler_params=_cparams(("parallel",)),
        name="attn_in",
    )(h, mods, rtab, ttab, w_row, w_col, q_g, w_uqt, kv_g, w_uk, w_uvt, place)


def _rope_tables(seq, tm):
    rows = seq // GRID_W
    row = jnp.repeat(jnp.arange(rows, dtype=F32), GRID_W)
    col = jnp.tile(jnp.arange(GRID_W, dtype=F32), rows)

    def angles(rot_dim):
        n_freq = rot_dim // 4
        inv = ROPE_BASE ** (-jnp.arange(n_freq, dtype=F32) / n_freq)
        return jnp.concatenate([row[:, None] * inv, col[:, None] * inv], axis=-1)

    def build(ang, first_lo, half, live):
        lane = jnp.arange(LANES)
        cos, sin = jnp.cos(ang), jnp.sin(ang)
        zeros = jnp.zeros((seq, LANES), F32)
        a = jnp.where(lane < live, 1.0, 0.0)[None, :] + zeros
        bm, cm = zeros, zeros
        for lo in first_lo:
            a = a.at[:, lo:lo + half].set(cos).at[:, lo + half:lo + 2 * half].set(cos)
            bm = bm.at[:, lo:lo + half].set(-sin)
            cm = cm.at[:, lo + half:lo + 2 * half].set(sin)
        ident = jnp.where(lane < live, 1.0, 0.0)[None, :] + jnp.zeros((tm, LANES), F32)
        z = jnp.zeros((tm, LANES), F32)
        return [jnp.concatenate([a, ident]), jnp.concatenate([bm, z]), jnp.concatenate([cm, z])]

    ang_m = angles(MLA_ROPE)
    ang_d = angles(DIFF_HEAD_DIM)
    rtab = jnp.stack(build(ang_m, (0,), _MLA_HALF, MLA_ROPE)
                     + build(ang_d, (0, DIFF_HEAD_DIM), _DIFF_HALF, LANES))

    def chan(ang, fn, fill):
        return jnp.concatenate([fn(ang).T, jnp.full((ang.shape[1], tm), fill, F32)], axis=1)

    ttab = jnp.concatenate([chan(ang_m, jnp.cos, 1.0), chan(ang_m, jnp.sin, 0.0),
                            chan(ang_d, jnp.cos, 1.0), chan(ang_d, jnp.sin, 0.0)], axis=0)
    return rtab, ttab


def _softmax_pv_t(qt, keys, vals_t):
    s = [_bdot(k, qt) for k in keys]
    m = s[0].max(axis=0, keepdims=True)
    for si in s[1:]:
        m = jnp.maximum(m, si.max(axis=0, keepdims=True))
    num, den = None, None
    for si, vt in zip(s, vals_t):
        p = jnp.exp2(si - m)
        d = p.sum(axis=0, keepdims=True)
        o = _bdot(vt, p.astype(BF16))
        num = o if num is None else num + o
        den = d if den is None else den + d
    return num / den


def _fold_rows(v, op):
    parts = [v[r:r + 8] for r in range(0, v.shape[0], 8)]
    while len(parts) > 1:
        parts = [op(parts[i], parts[i + 1]) for i in range(0, len(parts) - 1, 2)] + (
            [parts[-1]] if len(parts) % 2 else [])
    return parts[0]


def _attn_pipelined(t, n_tiles, q_list, k_lanes, kx_ref, kc_ref, vx_ref, vc_ref, s_refs, m_ref, l_ref, acc_ref,
                    o_ref, combine):
    n_maps = len(q_list)
    tq = q_list[0].shape[1]
    chunks = ([(kc_ref, vc_ref, c) for c in range(kc_ref.shape[0])]
              + [(kx_ref, vx_ref, c) for c in range(kx_ref.shape[0])])
    neg = jnp.full((8, tq), -jnp.inf, F32)
    zero = jnp.zeros((8, tq), F32)

    def score_chunk(dst_ref, ci, q_list, mrun):
        k_ref, _, c = chunks[ci]
        out = []
        for j in range(n_maps):
            s = _bdot(k_ref[c, :, k_lanes[j]], q_list[j])
            dst_ref[j, ci] = s
            out.append(jnp.maximum(mrun[j], _fold_rows(s, jnp.maximum)))
        return out

    def exp_chunk(src_ref, par, ci, m_cur, lrun):
        _, v_ref, c = chunks[ci]
        vt = v_ref[c]
        out = []
        for j in range(n_maps):
            p = jnp.exp2(src_ref[j, ci] - m_cur[j])
            out.append(lrun[j] + _fold_rows(p, jnp.add))
            o = _bdot(vt, p.astype(BF16))
            if ci == 0:
                acc_ref[par, j] = o
            else:
                acc_ref[par, j] += o
        return out

    def step(slot, do_scores, do_exp, do_emit):
        if do_emit:
            outs = [acc_ref[slot, j] / jnp.sum(l_ref[slot, j], axis=0, keepdims=True) for j in range(n_maps)]
            o_ref[...] = combine(outs).astype(o_ref.dtype)
        new_ref, old_ref = s_refs[slot], s_refs[1 - slot]
        lrun, mrun = [zero] * n_maps, [neg] * n_maps
        if do_exp:
            m_old = [jnp.max(m_ref[1 - slot, j], axis=0, keepdims=True) for j in range(n_maps)]
        for ci in range(len(chunks)):
            if do_scores:
                mrun = score_chunk(new_ref, ci, q_list, mrun)
            if do_exp:
                lrun = exp_chunk(old_ref, 1 - slot, ci, m_old, lrun)
        for j in range(n_maps):
            if do_scores:
                m_ref[slot, j] = mrun[j]
            if do_exp:
                l_ref[1 - slot, j] = lrun[j]

    pl.when(t == 0)(functools.partial(step, 0, True, False, False))
    pl.when(t == 1)(functools.partial(step, 1, n_tiles > 1, True, False))
    for slot in (0, 1):
        steady = jnp.logical_and(t % 2 == slot, jnp.logical_and(t > 1, t < n_tiles))
        pl.when(steady)(functools.partial(step, slot, True, True, True))
    if n_tiles > 1:
        pl.when(t == n_tiles)(functools.partial(step, n_tiles % 2, False, True, True))
    pl.when(t == n_tiles + 1)(functools.partial(step, (n_tiles + 1) % 2, False, False, True))


def _mla_maps(q_ref):
    return [q_ref[:HEAD_BLOCK, :], q_ref[HEAD_BLOCK:, :]]


_MLA_K_LANES = (slice(0, HEAD_BLOCK), slice(HEAD_BLOCK, 2 * HEAD_BLOCK))
_DIFF_K_LANES = (slice(0, HEAD_BLOCK), slice(0, HEAD_BLOCK))


def _mla_combine(outs):
    row = lax.broadcasted_iota(jnp.int32, outs[0].shape, 0)
    return jnp.where(row < MLA_V, outs[0], outs[1]).T


def _diff_maps(q_ref):
    q = q_ref[...]
    row = lax.broadcasted_iota(jnp.int32, q.shape, 0)
    zero = jnp.zeros_like(q)
    return [jnp.where(row < DIFF_HEAD_DIM, q, zero), jnp.where(row >= DIFF_HEAD_DIM, q, zero)]


def _diff_lambda(lam_ref, lam_init):
    lv = lam_ref[...]
    return (jnp.exp(jnp.sum(lv[0:1] * lv[1:2], axis=-1, keepdims=True))
            - jnp.exp(jnp.sum(lv[2:3] * lv[3:4], axis=-1, keepdims=True)) + lam_init)


def _diff_combine(outs, lam, lam_init, g):
    return _rms_rows((outs[0] - lam * outs[1]).T, g) * (1.0 - lam_init)


def _mla_lat_kernel(q_ref, kx_ref, kc_ref, vx_ref, vc_ref, o_ref, s0_ref, s1_ref, m_ref, l_ref, acc_ref, *,
                    n_tiles):
    _attn_pipelined(pl.program_id(0), n_tiles, _mla_maps(q_ref), _MLA_K_LANES, kx_ref, kc_ref, vx_ref, vc_ref,
                    (s0_ref, s1_ref), m_ref, l_ref, acc_ref, o_ref, _mla_combine)


def _diff_lat_kernel(lam_ref, g_ref, q_ref, kx_ref, kc_ref, vx_ref, vc_ref, o_ref, s0_ref, s1_ref, m_ref, l_ref,
                     acc_ref, *, n_tiles, lam_init):
    combine = functools.partial(_diff_combine, lam=_diff_lambda(lam_ref, lam_init), lam_init=lam_init, g=g_ref[...])
    _attn_pipelined(pl.program_id(0), n_tiles, _diff_maps(q_ref), _DIFF_K_LANES, kx_ref, kc_ref, vx_ref, vc_ref,
                    (s0_ref, s1_ref), m_ref, l_ref, acc_ref, o_ref, combine)


def _ctx_attention(q_maps, k_lanes, kc_ref, vc_ref, grp, qk_w):
    n_cc = kc_ref.shape[0]
    vals = [vc_ref[cc, grp * HEAD_BLOCK:(grp + 1) * HEAD_BLOCK, :] for cc in range(n_cc)]
    outs = []
    for q, lanes in zip(q_maps, k_lanes):
        cols = slice(grp * qk_w + lanes.start, grp * qk_w + lanes.stop)
        outs.append(_softmax_pv_t(q, [kc_ref[cc, :, cols] for cc in range(n_cc)], vals))
    return outs


def _group_rows(q_ref, grp, qk_w):
    return q_ref.at[pl.ds(grp * qk_w, qk_w)]


def _mla_ctx_kernel(q_ref, kc_ref, vc_ref, o_ref):
    qk_w = 2 * HEAD_BLOCK
    for grp in range(MLA_HEADS // 2):
        outs = _ctx_attention(_mla_maps(_group_rows(q_ref, grp, qk_w)), _MLA_K_LANES, kc_ref, vc_ref, grp, qk_w)
        o_ref[:, grp * HEAD_BLOCK:(grp + 1) * HEAD_BLOCK] = _mla_combine(outs).astype(o_ref.dtype)


def _diff_ctx_kernel(lam_ref, g_ref, q_ref, kc_ref, vc_ref, o_ref, *, lam_init):
    lam = _diff_lambda(lam_ref, lam_init)
    for grp in range(DIFF_HEADS):
        outs = _ctx_attention(_diff_maps(_group_rows(q_ref, grp, HEAD_BLOCK)), _DIFF_K_LANES, kc_ref, vc_ref, grp,
                              HEAD_BLOCK)
        o_ref[:, grp * HEAD_BLOCK:(grp + 1) * HEAD_BLOCK] = _diff_combine(outs, lam, lam_init,
                                                                          g_ref[...]).astype(o_ref.dtype)


def _attn_lat_call(geom, kern, name, small, qt, k3, vt3, qk_w, n_groups, tq):
    n_q = geom.seq // tq
    n_lc = geom.seq // ATT_KC
    n_cc = geom.ctx // ATT_KC
    ctx_c0 = geom.n_lat // geom.ctx
    n_tiles = geom.batch * n_groups * n_q

    def item(t):
        t = jnp.clip(t, 0, n_tiles - 1)
        return t // (n_groups * n_q), (t // n_q) % n_groups, t % n_q

    def scored(t):
        return item(t)

    def emitted(t):
        return item(t - 1)

    def written(t):
        return item(t - 2)

    small_specs = [pl.BlockSpec(a.shape, lambda t: (0, 0)) for a in small]
    specs = small_specs + [
        pl.BlockSpec((qk_w, tq), lambda t: (scored(t)[1], scored(t)[0] * n_q + scored(t)[2])),
        pl.BlockSpec((n_lc, ATT_KC, qk_w), lambda t: (scored(t)[0], 0, scored(t)[1])),
        pl.BlockSpec((n_cc, ATT_KC, qk_w), lambda t: (ctx_c0 + scored(t)[0], 0, scored(t)[1])),
        pl.BlockSpec((n_lc, HEAD_BLOCK, ATT_KC), lambda t: (emitted(t)[0], emitted(t)[1], 0)),
        pl.BlockSpec((n_cc, HEAD_BLOCK, ATT_KC), lambda t: (ctx_c0 + emitted(t)[0], emitted(t)[1], 0)),
    ]
    return pl.pallas_call(
        functools.partial(kern, n_tiles=n_tiles),
        grid=(n_tiles + 2,),
        in_specs=specs,
        out_specs=pl.BlockSpec((tq, HEAD_BLOCK), lambda t: (written(t)[0] * n_q + written(t)[2], written(t)[1])),
        out_shape=jax.ShapeDtypeStruct((geom.n_lat, n_groups * HEAD_BLOCK), BF16),
        scratch_shapes=[pltpu.VMEM((2, n_cc + n_lc, ATT_KC, tq), F32),
                        pltpu.VMEM((2, n_cc + n_lc, ATT_KC, tq), F32),
                        pltpu.VMEM((2, 2, 8, tq), F32),
                        pltpu.VMEM((2, 2, 8, tq), F32),
                        pltpu.VMEM((2, 2, HEAD_BLOCK, tq), F32)],
        compiler_params=_cparams(("arbitrary",)),
        name=name,
    )(*small, qt, k3, k3, vt3, vt3)


def _attn_ctx_call(geom, kern, name, small, qt, k3, vt3, qk_w, n_groups):
    n_cc = geom.ctx // ATT_KC
    ctx_c0 = geom.n_lat // geom.ctx
    small_specs = [pl.BlockSpec(a.shape, lambda b: (0, 0)) for a in small]
    specs = small_specs + [
        pl.BlockSpec((n_groups * qk_w, geom.ctx), lambda b: (0, ctx_c0 + b)),
        pl.BlockSpec((n_cc, ATT_KC, n_groups * qk_w), lambda b: (ctx_c0 + b, 0, 0)),
        pl.BlockSpec((n_cc, n_groups * HEAD_BLOCK, ATT_KC), lambda b: (ctx_c0 + b, 0, 0)),
    ]
    return pl.pallas_call(
        kern,
        grid=(geom.batch,),
        in_specs=specs,
        out_specs=pl.BlockSpec((geom.ctx, n_groups * HEAD_BLOCK), lambda b: (b, 0)),
        out_shape=jax.ShapeDtypeStruct((geom.n_ctx, n_groups * HEAD_BLOCK), BF16),
        compiler_params=_cparams(("parallel",)),
        name=name,
    )(*small, qt, k3, vt3)


def _attention(geom, qa, ka, va, dq, dk, dv, lam_vecs, subln_g, lam_init, with_ctx, tq):
    n_chunks = geom.n_tok // ATT_KC
    ka3 = ka.reshape(n_chunks, ATT_KC, _QA_W)
    dk3 = dk.reshape(n_chunks, ATT_KC, DIFF_WIDTH)
    small = [lam_vecs, subln_g]
    lat = (_attn_lat_call(geom, _mla_lat_kernel, "mla_attn", [], qa, ka3, va, 2 * HEAD_BLOCK, MLA_HEADS // 2, tq),
           _attn_lat_call(geom, functools.partial(_diff_lat_kernel, lam_init=lam_init), "diff_attn", small, dq, dk3,
                          dv, HEAD_BLOCK, DIFF_HEADS, tq))
    if not with_ctx:
        return lat, None
    ctx = (_attn_ctx_call(geom, _mla_ctx_kernel, "mla_attn_ctx", [], qa, ka3, va, 2 * HEAD_BLOCK, MLA_HEADS // 2),
           _attn_ctx_call(geom, functools.partial(_diff_ctx_kernel, lam_init=lam_init), "diff_attn_ctx", small, dq,
                          dk3, dv, HEAD_BLOCK, DIFF_HEADS))
    return lat, ctx


def _attn_out_kernel(*refs, n_lat_tiles, with_ctx):
    if with_ctx:
        oa_ref, od_ref, oac_ref, odc_ref, h_ref, mod_ref, w_ref, g_ref, b_ref, o_ref = refs
        is_lat = pl.program_id(0) < n_lat_tiles
        oa = jnp.where(is_lat, oa_ref[...], oac_ref[...])
        od = jnp.where(is_lat, od_ref[...], odc_ref[...])
    else:
        oa_ref, od_ref, h_ref, mod_ref, w_ref, g_ref, b_ref, o_ref = refs
        oa, od = oa_ref[...], od_ref[...]
    y = _bdot(oa, w_ref[:_VA_W, :]) + _bdot(od, w_ref[_VA_W:, :])
    r = mod_ref[5:6, :] * y
    o_ref[...] = _layer_norm_rows(DEEPNORM_ALPHA * h_ref[...] + r, g_ref[...], b_ref[...])


def _attn_out_call(geom, lat, ctx, h, mods, layer, w_out, ln_g, ln_b, tm):
    n_lat_tiles = geom.n_lat // tm
    with_ctx = ctx is not None
    n_rows = geom.n_tok if with_ctx else geom.n_lat
    head_specs = [pl.BlockSpec((tm, _VA_W), lambda i: (jnp.minimum(i, n_lat_tiles - 1), 0)),
                  pl.BlockSpec((tm, DIFF_WIDTH), lambda i: (jnp.minimum(i, n_lat_tiles - 1), 0))]
    heads = list(lat)
    if with_ctx:
        head_specs += [pl.BlockSpec((tm, _VA_W), lambda i: (jnp.maximum(i - n_lat_tiles, 0), 0)),
                       pl.BlockSpec((tm, DIFF_WIDTH), lambda i: (jnp.maximum(i - n_lat_tiles, 0), 0))]
        heads += list(ctx)
    return pl.pallas_call(
        functools.partial(_attn_out_kernel, n_lat_tiles=n_lat_tiles, with_ctx=with_ctx),
        grid=(n_rows // tm,),
        in_specs=head_specs + [
            _row_spec(tm, D_MODEL),
            geom.mod_spec(layer, tm),
            _full_spec(w_out.shape),
            _full_spec((1, D_MODEL)),
            _full_spec((1, D_MODEL)),
        ],
        out_specs=_row_spec(tm, D_MODEL),
        out_shape=jax.ShapeDtypeStruct((n_rows, D_MODEL), F32),
        compiler_params=_cparams(("parallel",)),
        name="attn_out",
    )(*heads, h, mods, w_out, ln_g.reshape(1, -1), ln_b.reshape(1, -1))


SSD_PROJ_COLS = SSD_INNER + SSD_CONV_DIM + 2 * LANES
_S_XBC = SSD_INNER
_S_DT = SSD_INNER + SSD_CONV_DIM


def _ssd_in_kernel(h_ref, hp_ref, hn_ref, mod_ref, w_ref, cw_ref, cb_ref, z_ref, xbc_ref, dt_ref, pad_ref, *,
                   tm, tiles_per_seq, n_lat_tiles, nc):
    i = pl.program_id(0)
    is_lat = i < n_lat_tiles
    pos = i % tiles_per_seq
    has_prev = jnp.logical_and(is_lat, pos > 0)
    has_next = jnp.logical_and(is_lat, pos < tiles_per_seq - 1)
    scale1 = 1.0 + mod_ref[4:5, :]
    shift = mod_ref[3:4, :]
    u = (h_ref[...] * scale1 + shift).astype(BF16)
    u_ext = jnp.concatenate([(hp_ref[...] * scale1 + shift).astype(BF16), u,
                             (hn_ref[...] * scale1 + shift).astype(BF16)], axis=0)
    for lo in range(0, SSD_INNER, nc):
        z_ref[:, lo:lo + nc] = _bdot(u, w_ref[:, lo:lo + nc]).astype(BF16)
    dt = _bdot(u, w_ref[:, _S_DT:])
    dt_ref[0] = dt[:, :LANES]
    dt_ref[1] = dt[:, LANES:]
    half = SSD_CONV // 2
    for lo in range(0, SSD_CONV_DIM, nc):
        cs = slice(lo, lo + nc)
        r = _bdot(u_ext, w_ref[:, _S_XBC + lo:_S_XBC + lo + nc])
        head, tail = r[:HALO], r[HALO + tm:]
        r = jnp.concatenate([jnp.where(has_prev, head, jnp.zeros_like(head)), r[HALO:HALO + tm],
                             jnp.where(has_next, tail, jnp.zeros_like(tail))], axis=0)
        acc = cb_ref[:, cs] + jnp.zeros((tm, nc), F32)
        for k in range(SSD_CONV):
            sh = r if k == half else pltpu.roll(r, (half - k) % r.shape[0], axis=0)
            acc = acc + sh[HALO:HALO + tm] * cw_ref[k:k + 1, cs]
        xbc_ref[:, cs] = (acc * _sigmoid(acc)).astype(BF16)


def _ssd_in_call(geom, h, mods, layer, w_in, conv_w, conv_b, tm):
    n = geom.n_tok
    nc = 512
    hb = tm // HALO
    last_hb = n // HALO - 1
    kern = functools.partial(_ssd_in_kernel, tm=tm, tiles_per_seq=geom.seq // tm, n_lat_tiles=geom.n_lat // tm, nc=nc)
    return pl.pallas_call(
        kern,
        grid=(n // tm,),
        in_specs=[
            _row_spec(tm, D_MODEL),
            pl.BlockSpec((HALO, D_MODEL), lambda i: (jnp.maximum(i * hb - 1, 0), 0)),
            pl.BlockSpec((HALO, D_MODEL), lambda i: (jnp.minimum((i + 1) * hb, last_hb), 0)),
            geom.mod_spec(layer, tm),
            _full_spec(w_in.shape),
            _full_spec(conv_w.shape),
            _full_spec((1, SSD_CONV_DIM)),
        ],
        out_specs=[_row_spec(tm, SSD_INNER), _row_spec(tm, SSD_CONV_DIM),
                   pl.BlockSpec((2, tm, LANES), lambda i: (0, i, 0))],
        out_shape=[jax.ShapeDtypeStruct((n, SSD_INNER), BF16), jax.ShapeDtypeStruct((n, SSD_CONV_DIM), BF16),
                   jax.ShapeDtypeStruct((2, n, LANES), F32)],
        scratch_shapes=[pltpu.VMEM((tm + 2 * HALO, nc), F32)],
        compiler_params=_cparams(("parallel",)),
        name="ssd_in",
    )(h, h, h, mods, w_in, conv_w, conv_b.reshape(1, -1))


_GROUP_W = SSD_INNER // SSD_GROUPS
_HEADS_PER_GROUP = SSD_HEADS // SSD_GROUPS
_B_COL = SSD_INNER
_C_COL = SSD_INNER + SSD_GROUPS * SSD_STATE
SCAN_CPS = 2


def _scan_kernel(xf_ref, xb_ref, dtf_ref, dtb_ref, bias_ref, alog_ref, expand_ref, yf_ref, yb_ref, state_ref):
    @pl.when(pl.program_id(1) == 0)
    def _():
        state_ref[...] = jnp.zeros_like(state_ref)

    for k in range(SCAN_CPS):
        rf = pl.ds(k * SSD_CHUNK, SSD_CHUNK)
        rb = pl.ds((SCAN_CPS - 1 - k) * SSD_CHUNK, SSD_CHUNK)
        _scan_chunk(0, xf_ref.at[rf], dtf_ref.at[rf], bias_ref[0], alog_ref[0], expand_ref, yf_ref.at[rf],
                    state_ref.at[0])
        _scan_chunk(1, xb_ref.at[rb], dtb_ref.at[rb], bias_ref[1], alog_ref[1], expand_ref, yb_ref.at[rb],
                    state_ref.at[1])


def _scan_chunk(d, xbc_ref, dt_ref, bias, alog, expand_ref, y_ref, state_ref):
    t = SSD_CHUNK
    lane = lax.broadcasted_iota(jnp.int32, (1, LANES), 1)
    a = jnp.where(lane < SSD_HEADS, -jnp.exp(alog), 0.0)
    raw = dt_ref[...] + bias
    e = jnp.exp(-jnp.abs(raw))
    u = 1.0 + e
    um1 = u - 1.0
    dt = jnp.maximum(raw, 0.0) + jnp.where(um1 == 0.0, e, jnp.log(u) * (e / jnp.where(um1 == 0.0, 1.0, um1)))
    da = dt * a
    row = lax.broadcasted_iota(jnp.int32, (t, t), 0)
    col = lax.broadcasted_iota(jnp.int32, (t, t), 1)
    mask = (col <= row) if d == 0 else (col >= row)
    tri = mask.astype(BF16)
    da_hi = da.astype(BF16)
    rem = da - da_hi.astype(F32)
    da_mid = rem.astype(BF16)
    da_lo = (rem - da_mid.astype(F32)).astype(BF16)
    cum = _bdot(tri, da_hi) + _bdot(tri, da_mid) + _bdot(tri, da_lo)
    total = jnp.sum(da, axis=0, keepdims=True)
    src_t = (cum - jnp.log(dt)).T
    ecum = jnp.exp(cum)
    dtdec = dt * jnp.exp(total - cum)
    cdec = jnp.broadcast_to(jnp.exp(total), (8, LANES))

    per_head = jnp.concatenate([dtdec, ecum, cdec], axis=0).astype(BF16)

    lane_t = lax.broadcasted_iota(jnp.int32, (t, LANES), 1)
    mask_bias = jnp.where(mask, 0.0, -jnp.inf)
    zero_x = jnp.zeros((t, LANES), BF16)
    for g in range(SSD_GROUPS):
        gs = slice(g * _GROUP_W, (g + 1) * _GROUP_W)
        ex = _bdot(per_head, expand_ref[:, gs])
        w_state, w_off, w_carry = ex[0:t], ex[t:2 * t], ex[2 * t:2 * t + 1]
        b_g = xbc_ref[:, _B_COL + g * SSD_STATE:_B_COL + (g + 1) * SSD_STATE]
        c_g = xbc_ref[:, _C_COL + g * SSD_STATE:_C_COL + (g + 1) * SSD_STATE]
        cb = _bdot_nt(c_g, b_g)
        st = state_ref[g]
        y_off = _bdot(c_g, st.astype(BF16)) * w_off
        for pr in range(_HEADS_PER_GROUP // 2):
            xs = slice(g * _GROUP_W + pr * LANES, g * _GROUP_W + (pr + 1) * LANES)
            x_pair = xbc_ref[:, xs]
            ms = []
            for e in range(2):
                hd = g * _HEADS_PER_GROUP + 2 * pr + e
                seg = cum[:, hd:hd + 1] - src_t[hd:hd + 1, :]
                ms.append((cb * jnp.exp(seg + mask_bias)).astype(BF16))
            x_blk = jnp.concatenate([jnp.where(lane_t < SSD_HEAD_DIM, x_pair, zero_x),
                                     jnp.where(lane_t >= SSD_HEAD_DIM, x_pair, zero_x)], axis=0)
            y_pair = _bdot(jnp.concatenate(ms, axis=1), x_blk)
            y_ref[:, xs] = (y_pair + y_off[:, pr * LANES:(pr + 1) * LANES]).astype(y_ref.dtype)
        xw = (xbc_ref[:, gs].astype(F32) * w_state).astype(BF16)
        b_t = b_g.astype(F32).T.astype(BF16)
        state_ref[g] = st * w_carry + _bdot(b_t, xw)


def _scan_call(geom, xbc, dt, dt_bias, a_log):
    t = SCAN_CPS * SSD_CHUNK
    assert geom.ctx % t == 0 and geom.seq % t == 0
    ncc = geom.ctx // t
    nlc = geom.seq // t
    ctx0 = geom.n_lat // t

    def row_block(b, d, s):
        jc = s + d * (ncc - 1 - 2 * s)
        sl = s - ncc
        jl = sl + d * (nlc - 1 - 2 * sl)
        return jnp.where(s < ncc, ctx0 + b * ncc + jc, b * nlc + jl)

    def pad_lanes(v):
        return jnp.pad(v.astype(F32), ((0, 0), (0, LANES - v.shape[-1]))).reshape(2, 1, LANES)

    expand = (jnp.arange(SSD_INNER)[None, :] // SSD_HEAD_DIM == jnp.arange(LANES)[:, None]).astype(BF16)

    y_shape = jax.ShapeDtypeStruct((geom.n_tok, SSD_INNER), BF16)
    return pl.pallas_call(
        _scan_kernel,
        grid=(geom.batch, ncc + nlc),
        in_specs=[
            pl.BlockSpec((t, SSD_CONV_DIM), lambda b, s: (row_block(b, 0, s), 0)),
            pl.BlockSpec((t, SSD_CONV_DIM), lambda b, s: (row_block(b, 1, s), 0)),
            pl.BlockSpec((None, t, LANES), lambda b, s: (0, row_block(b, 0, s), 0)),
            pl.BlockSpec((None, t, LANES), lambda b, s: (1, row_block(b, 1, s), 0)),
            pl.BlockSpec((2, 1, LANES), lambda b, s: (0, 0, 0)),
            pl.BlockSpec((2, 1, LANES), lambda b, s: (0, 0, 0)),
            pl.BlockSpec((LANES, SSD_INNER), lambda b, s: (0, 0)),
        ],
        out_specs=[pl.BlockSpec((t, SSD_INNER), lambda b, s: (row_block(b, 0, s), 0)),
                   pl.BlockSpec((t, SSD_INNER), lambda b, s: (row_block(b, 1, s), 0))],
        out_shape=[y_shape, y_shape],
        scratch_shapes=[pltpu.VMEM((2, SSD_GROUPS, SSD_STATE, _GROUP_W), F32)],
        compiler_params=_cparams(("parallel", "arbitrary")),
        name="ssd_scan",
    )(xbc, xbc, dt, dt, pad_lanes(dt_bias), pad_lanes(a_log), expand)


def _ssd_out_kernel(yf_ref, yb_ref, xbc_ref, z_ref, dskip_ref, ng_ref, h_ref, mod_ref, w_ref, g_ref, b_ref, o_ref):
    dsk = dskip_ref[0:1, :] + dskip_ref[1:2, :]
    y = None
    for g in range(SSD_GROUPS):
        gs = slice(g * _GROUP_W, (g + 1) * _GROUP_W)
        z = z_ref[:, gs].astype(F32)
        ysum = yf_ref[:, gs].astype(F32) + yb_ref[:, gs].astype(F32) + dsk[:, gs] * xbc_ref[:, gs].astype(F32)
        gy = ysum * (z * _sigmoid(z))
        gy = _rms_rows(gy, ng_ref[:, gs]).astype(BF16)
        part = _bdot(gy, w_ref[gs, :])
        y = part if y is None else y + part
    r = mod_ref[5:6, :] * y
    o_ref[...] = _layer_norm_rows(DEEPNORM_ALPHA * h_ref[...] + r, g_ref[...], b_ref[...])


def _ssd_out_call(geom, y, xbc, z, dskip, norm_g, h, mods, layer, w_out, ln_g, ln_b, n_rows, tm):
    return pl.pallas_call(
        _ssd_out_kernel,
        grid=(n_rows // tm,),
        in_specs=[
            _row_spec(tm, SSD_INNER),
            _row_spec(tm, SSD_INNER),
            _row_spec(tm, SSD_INNER),
            _row_spec(tm, SSD_INNER),
            _full_spec(dskip.shape),
            _full_spec((1, SSD_INNER)),
            _row_spec(tm, D_MODEL),
            geom.mod_spec(layer, tm),
            _full_spec(w_out.shape),
            _full_spec((1, D_MODEL)),
            _full_spec((1, D_MODEL)),
        ],
        out_specs=_row_spec(tm, D_MODEL),
        out_shape=jax.ShapeDtypeStruct((n_rows, D_MODEL), F32),
        compiler_params=_cparams(("parallel",)),
        name="ssd_out",
    )(y[0], y[1], xbc, z, dskip, norm_g.reshape(1, -1), h, mods, w_out, ln_g.reshape(1, -1), ln_b.reshape(1, -1))


def _attn_weights(w_in, w_uq, w_ukv):
    d = w_in.shape[0]
    kr_end = MLA_Q_RANK + MLA_KV_RANK + MLA_ROPE
    dk_lo = kr_end + DIFF_WIDTH
    dv_lo = dk_lo + DIFF_WIDTH
    w_row = jnp.concatenate([w_in[:, :kr_end], jnp.zeros((d, HEAD_BLOCK - MLA_ROPE), w_in.dtype),
                             w_in[:, dk_lo:dv_lo]], axis=1).astype(BF16)
    w_col = jnp.concatenate([w_in[:, kr_end:dk_lo], w_in[:, dv_lo:]], axis=1).T.astype(BF16)
    pad_q = HEAD_BLOCK - MLA_QK_DIM
    w_uq_p = jnp.pad(w_uq.reshape(MLA_Q_RANK, MLA_HEADS, MLA_QK_DIM), ((0, 0), (0, 0), (0, pad_q)))
    w_uqt = w_uq_p.reshape(MLA_Q_RANK, _QA_W).T.astype(BF16)
    kv = w_ukv.reshape(MLA_KV_RANK, MLA_HEADS, MLA_NOPE + MLA_V)
    w_uk = jnp.pad(kv[:, :, :MLA_NOPE], ((0, 0), (0, 0), (0, HEAD_BLOCK - MLA_NOPE)))
    w_uk = w_uk.reshape(MLA_KV_RANK, _QA_W).astype(BF16)
    w_uvt = kv[:, :, MLA_NOPE:].reshape(MLA_KV_RANK, _VA_W).T.astype(BF16)
    src = jnp.arange(HEAD_BLOCK)[:, None]
    dst = jnp.arange(_QA_W)[None, :]
    place = ((dst % HEAD_BLOCK == src + MLA_NOPE) & (src < MLA_ROPE)).astype(BF16)
    return w_row, w_col, w_uqt, w_uk, w_uvt, place


def _ssd_weights(w_in):
    d = w_in.shape[0]
    z = jnp.zeros((d, LANES - SSD_HEADS), w_in.dtype)
    return jnp.concatenate([w_in[:, :_S_DT], w_in[:, _S_DT:_S_DT + SSD_HEADS], z, w_in[:, _S_DT + SSD_HEADS:], z],
                           axis=1).astype(BF16)


def _lambda_init_for(layer):
    return 0.8 - 0.6 * math.exp(-0.3 * layer)


def _pick_tile(seq, ctx, n_ctx, want):
    tm = want
    while seq % tm or n_ctx % tm:
        tm //= 2
    return tm


def kernel(x, c, ctx, c_ctx, ada_w, ada_b, ln_g, ln_b, ffn1_w_gu, ffn1_w_down, ffn2_w_gu, ffn2_w_down, attn_w_in,
           mla_q_norm_g, mla_w_uq, mla_kv_norm_g, mla_w_ukv, diff_lam_q1, diff_lam_k1, diff_lam_q2, diff_lam_k2,
           diff_subln_g, attn_w_out, ssd_w_in, ssd_conv_w, ssd_conv_b, ssd_a_log, ssd_dt_bias, ssd_d, ssd_norm_g,
           ssd_w_out):
    batch, seq, d = x.shape
    n_ctx_tok = ctx.shape[1]
    geom = _Geom(batch, seq, n_ctx_tok)
    assert d == D_MODEL and batch + 1 <= MOD_ROWS
    assert seq % GRID_W == 0 and seq % SSD_CHUNK == 0 and n_ctx_tok % SSD_CHUNK == 0
    tq = n_ctx_tok
    assert seq % tq == 0
    tm_ffn = _pick_tile(seq, n_ctx_tok, geom.n_ctx, 512)
    tm_proj = _pick_tile(seq, n_ctx_tok, geom.n_ctx, 256)
    assert n_ctx_tok % tm_proj == 0
    fc = 256

    cond = jnp.concatenate([c, c_ctx[None, :], jnp.zeros((MOD_ROWS - batch - 1, d), F32)], axis=0)
    mods = _ada_call(cond, ada_w, ada_b).reshape(DEPTH, MOD_ROWS, N_MOD, d)
    rtab, ttab = _rope_tables(seq, tm_proj)

    h = (x.reshape(batch * seq, d), ctx.reshape(batch * n_ctx_tok, d))
    for l in range(DEPTH):
        last = l == DEPTH - 1
        h = _ffn_call(geom, h, mods, l, 0, ffn1_w_gu[l].astype(BF16), ffn1_w_down[l].astype(BF16),
                      ln_g[l, 0], ln_b[l, 0], geom.n_tok, tm_ffn, fc)
        n_out = geom.n_lat if last else geom.n_tok
        if l % 2 == 0:
            a = l // 2
            wts = _attn_weights(attn_w_in[a], mla_w_uq[a], mla_w_ukv[a])
            qa, ka, va, dq, dk, dv = _attn_in_call(geom, h, mods, l, rtab, ttab, wts, mla_q_norm_g[a].reshape(1, -1),
                                                   mla_kv_norm_g[a].reshape(1, -1), tm_proj)
            lam_vecs = jnp.stack([diff_lam_q1[a], diff_lam_k1[a], diff_lam_q2[a], diff_lam_k2[a]])
            lat, ctx_heads = _attention(geom, qa, ka, va, dq, dk, dv, lam_vecs, diff_subln_g[a].reshape(1, -1),
                                        _lambda_init_for(l), not last, tq)
            h = _attn_out_call(geom, lat, ctx_heads, h, mods, l, attn_w_out[a].astype(BF16), ln_g[l, 1],
                               ln_b[l, 1], tm_ffn)
        else:
            s = l // 2
            z, xbc, dt = _ssd_in_call(geom, h, mods, l, _ssd_weights(ssd_w_in[s]), ssd_conv_w[s], ssd_conv_b[s],
                                      tm_proj)
            y = _scan_call(geom, xbc, dt, ssd_dt_bias[s], ssd_a_log[s])
            dskip = jnp.repeat(ssd_d[s], SSD_HEAD_DIM, axis=-1)
            h = _ssd_out_call(geom, y, xbc, z, dskip, ssd_norm_g[s], h, mods, l, ssd_w_out[s].astype(BF16),
                              ln_g[l, 1], ln_b[l, 1], n_out, tm_ffn)
        h = _ffn_call(geom, h, mods, l, 6, ffn2_w_gu[l].astype(BF16), ffn2_w_down[l].astype(BF16),
                      ln_g[l, 2], ln_b[l, 2], n_out, tm_ffn, fc)
    return h[:geom.n_lat].reshape(batch, seq, d)
```

```python
import functools
import math

import jax
import jax.numpy as jnp
from jax import lax
from jax.experimental import pallas as pl
from jax.experimental.pallas import tpu as pltpu

F32 = jnp.float32
BF16 = jnp.bfloat16

D_MODEL = 1024
DEPTH = 4
GRID_W = 64
N_MOD = 9
FFN_DIM = 2816
MACARON_WEIGHT = 0.5
MLA_HEADS = 8
MLA_Q_RANK = 384
MLA_KV_RANK = 256
MLA_NOPE = 64
MLA_ROPE = 32
MLA_V = 64
MLA_QK_DIM = MLA_NOPE + MLA_ROPE
MLA_SCALE = MLA_QK_DIM ** -0.5
DIFF_HEADS = 4
DIFF_HEAD_DIM = 64
DIFF_WIDTH = DIFF_HEADS * 2 * DIFF_HEAD_DIM
DIFF_SCALE = DIFF_HEAD_DIM ** -0.5
ROPE_BASE = 10000.0
SSD_INNER = 2 * D_MODEL
SSD_HEAD_DIM = 64
SSD_HEADS = SSD_INNER // SSD_HEAD_DIM
SSD_GROUPS = 4
SSD_STATE = 128
SSD_CONV = 5
SSD_CHUNK = 128
SSD_CONV_DIM = SSD_INNER + 2 * SSD_GROUPS * SSD_STATE
DEEPNORM_ALPHA = (2.0 * DEPTH) ** 0.25
LN_EPS = 1e-6
RMS_EPS = 1e-6

LANES = 128
V7X_VMEM_LIMIT = 56 * 1024 * 1024
HALO = 16

HEAD_BLOCK = LANES
MOD_ROWS = 24


def _cparams(sem):
    return pltpu.CompilerParams(dimension_semantics=sem, vmem_limit_bytes=V7X_VMEM_LIMIT)


def _sigmoid(v):
    return 0.5 * jnp.tanh(0.5 * v) + 0.5


def _layer_norm_rows(v, g, b):
    mu = jnp.mean(v, axis=-1, keepdims=True)
    c = v - mu
    var = jnp.mean(c * c, axis=-1, keepdims=True)
    return c * lax.rsqrt(var + LN_EPS) * g + b


def _rms_rows(v, g):
    return v * lax.rsqrt(jnp.mean(v * v, axis=-1, keepdims=True) + RMS_EPS) * g


def _bdot(a, b):
    return jnp.dot(a, b, preferred_element_type=F32)


def _bdot_nt(a, b):
    return lax.dot_general(a, b, (((1,), (1,)), ((), ())), preferred_element_type=F32)


def _ada_kernel(c_ref, w_ref, b_ref, o_ref):
    c = c_ref[...]
    s = (c * _sigmoid(c)).astype(BF16)
    o_ref[...] = _bdot(s, w_ref[...].astype(BF16)) + b_ref[...]


def _ada_call(cond, ada_w, ada_b):
    depth, d, n = ada_w.shape
    tn = n // 8
    return pl.pallas_call(
        _ada_kernel,
        grid=(depth, n // tn),
        in_specs=[
            pl.BlockSpec((MOD_ROWS, d), lambda l, j: (0, 0)),
            pl.BlockSpec((None, d, tn), lambda l, j: (l, 0, j)),
            pl.BlockSpec((None, 1, tn), lambda l, j: (l, 0, j)),
        ],
        out_specs=pl.BlockSpec((None, MOD_ROWS, tn), lambda l, j: (l, 0, j)),
        out_shape=jax.ShapeDtypeStruct((depth, MOD_ROWS, n), F32),
        compiler_params=_cparams(("parallel", "parallel")),
        name="ada_mod",
    )(cond, ada_w, ada_b.reshape(depth, 1, n))


class _Geom:
    def __init__(self, batch, seq, ctx):
        self.batch, self.seq, self.ctx = batch, seq, ctx
        self.n_lat = batch * seq
        self.n_ctx = batch * ctx
        self.n_tok = self.n_lat + self.n_ctx

    def mod_spec(self, layer, tm):
        n_lat_tiles = self.n_lat // tm
        tiles_per_batch = self.seq // tm
        batch = self.batch

        def index(i):
            return (layer, jnp.where(i < n_lat_tiles, i // tiles_per_batch, batch), 0, 0)

        return pl.BlockSpec((None, None, N_MOD, D_MODEL), index)


def _row_spec(tm, width):
    return pl.BlockSpec((tm, width), lambda i: (i, 0))


def _full_spec(shape):
    nd = len(shape)
    return pl.BlockSpec(shape, lambda i: (0,) * nd)


def _ffn_kernel(*refs, k0, fc, n_lat_tiles):
    if n_lat_tiles is None:
        h_ref, mod_ref, wgu_ref, wd_ref, g_ref, b_ref, o_ref, acc_ref = refs
        h = h_ref[...]
    else:
        hx_ref, hc_ref, mod_ref, wgu_ref, wd_ref, g_ref, b_ref, o_ref, acc_ref = refs
        h = jnp.where(pl.program_id(0) < n_lat_tiles, hx_ref[...], hc_ref[...])
    shift = mod_ref[k0:k0 + 1, :]
    scale = mod_ref[k0 + 1:k0 + 2, :]
    gate = mod_ref[k0 + 2:k0 + 3, :]
    t = (h * (1.0 + scale) + shift).astype(BF16)
    for j in range(FFN_DIM // fc):
        gj = _bdot(t, wgu_ref[:, j * fc:(j + 1) * fc])
        uj = _bdot(t, wgu_ref[:, FFN_DIM + j * fc:FFN_DIM + (j + 1) * fc])
        a = (gj * _sigmoid(gj) * uj).astype(BF16)
        y = _bdot(a, wd_ref[j * fc:(j + 1) * fc, :])
        if j == 0:
            acc_ref[...] = y
        else:
            acc_ref[...] += y
    r = (MACARON_WEIGHT * gate) * acc_ref[...]
    o_ref[...] = _layer_norm_rows(DEEPNORM_ALPHA * h + r, g_ref[...], b_ref[...])


def _ffn_call(geom, h, mods, layer, k0, w_gu, w_down, ln_g, ln_b, n_rows, tm, fc):
    if isinstance(h, tuple):
        n_lat_tiles = geom.n_lat // tm
        h_args = h
        h_specs = [pl.BlockSpec((tm, D_MODEL), lambda i: (jnp.minimum(i, n_lat_tiles - 1), 0)),
                   pl.BlockSpec((tm, D_MODEL), lambda i: (jnp.maximum(i - n_lat_tiles, 0), 0))]
    else:
        n_lat_tiles = None
        h_args = (h,)
        h_specs = [_row_spec(tm, D_MODEL)]
    kern = functools.partial(_ffn_kernel, k0=k0, fc=fc, n_lat_tiles=n_lat_tiles)
    return pl.pallas_call(
        kern,
        grid=(n_rows // tm,),
        in_specs=h_specs + [
            geom.mod_spec(layer, tm),
            _full_spec(w_gu.shape),
            _full_spec(w_down.shape),
            _full_spec((1, D_MODEL)),
            _full_spec((1, D_MODEL)),
        ],
        out_specs=_row_spec(tm, D_MODEL),
        out_shape=jax.ShapeDtypeStruct((n_rows, D_MODEL), F32),
        scratch_shapes=[pltpu.VMEM((tm, D_MODEL), F32)],
        compiler_params=_cparams(("parallel",)),
        name="ffn",
    )(*h_args, mods, w_gu, w_down, ln_g.reshape(1, -1), ln_b.reshape(1, -1))


_QA_W = MLA_HEADS * HEAD_BLOCK
_VA_W = MLA_HEADS * MLA_V
_R_CKV = MLA_Q_RANK
_R_KR = MLA_Q_RANK + MLA_KV_RANK
_R_DK = _R_KR + HEAD_BLOCK
ATTN_ROW_COLS = _R_DK + DIFF_WIDTH
_MLA_HALF = MLA_ROPE // 2
_DIFF_HALF = DIFF_HEAD_DIM // 2
_T_SIN_M = _MLA_HALF
_T_COS_D = 2 * _MLA_HALF
_T_SIN_D = _T_COS_D + _DIFF_HALF
ROPE_T_ROWS = _T_SIN_D + _DIFF_HALF
ATT_KC = 256
ATT_TQ = 512
LOG2E = math.log2(math.e)
MLA_QSCALE = MLA_SCALE * LOG2E
DIFF_QSCALE = DIFF_SCALE * LOG2E


def _rope_block(v, tab_ref, t0, shift):
    left = pltpu.roll(v, LANES - shift, axis=1)
    right = pltpu.roll(v, shift, axis=1)
    return v * tab_ref[t0] + left * tab_ref[t0 + 1] + right * tab_ref[t0 + 2]


def _rope_rows(dst_ref, src, r0, half, cos, sin, scale):
    t1 = src[r0:r0 + half]
    t2 = src[r0 + half:r0 + 2 * half]
    dst_ref[r0:r0 + half, :] = ((t1 * cos - t2 * sin) * scale).astype(dst_ref.dtype)
    dst_ref[r0 + half:r0 + 2 * half, :] = ((t1 * sin + t2 * cos) * scale).astype(dst_ref.dtype)


def _attn_in_kernel(h_ref, mod_ref, rtab_ref, ttab_ref, wrow_ref, wcol_ref, qg_ref, wuqt_ref, kvg_ref, wuk_ref,
                    wuvt_ref, place_ref, qat_ref, ka_ref, vat_ref, dqt_ref, dk_ref, dvt_ref):
    h = h_ref[...]
    u = (h * (1.0 + mod_ref[4:5, :]) + mod_ref[3:4, :]).astype(BF16)
    proj = _bdot(u, wrow_ref[...])
    projt = _bdot_nt(wcol_ref[...], u)

    cqn = _rms_rows(proj[:, :_R_CKV], qg_ref[...]).astype(BF16)
    qt = _bdot_nt(wuqt_ref[...], cqn)
    cos_m = ttab_ref[0:_T_SIN_M, :]
    sin_m = ttab_ref[_T_SIN_M:_T_COS_D, :]
    for hd in range(MLA_HEADS):
        r0 = hd * HEAD_BLOCK
        qat_ref[r0:r0 + MLA_NOPE, :] = (qt[r0:r0 + MLA_NOPE] * MLA_QSCALE).astype(BF16)
        _rope_rows(qat_ref, qt, r0 + MLA_NOPE, _MLA_HALF, cos_m, sin_m, MLA_QSCALE)
        qat_ref[r0 + MLA_QK_DIM:r0 + HEAD_BLOCK, :] = jnp.zeros((HEAD_BLOCK - MLA_QK_DIM, qt.shape[1]), BF16)

    ckvn = _rms_rows(proj[:, _R_CKV:_R_KR], kvg_ref[...]).astype(BF16)
    kr = _rope_block(proj[:, _R_KR:_R_DK], rtab_ref, 0, _MLA_HALF).astype(BF16)
    ka_ref[...] = (_bdot(ckvn, wuk_ref[...]) + _bdot(kr, place_ref[...])).astype(BF16)
    vat_ref[...] = _bdot_nt(wuvt_ref[...], ckvn).astype(BF16)

    cos_d = ttab_ref[_T_COS_D:_T_SIN_D, :]
    sin_d = ttab_ref[_T_SIN_D:ROPE_T_ROWS, :]
    for sub in range(2 * DIFF_HEADS):
        _rope_rows(dqt_ref, projt, sub * DIFF_HEAD_DIM, _DIFF_HALF, cos_d, sin_d, DIFF_QSCALE)
    dvt_ref[...] = projt[DIFF_WIDTH:].astype(BF16)
    for hd in range(DIFF_HEADS):
        sk = slice(_R_DK + hd * HEAD_BLOCK, _R_DK + (hd + 1) * HEAD_BLOCK)
        so = slice(hd * HEAD_BLOCK, (hd + 1) * HEAD_BLOCK)
        dk_ref[:, so] = _rope_block(proj[:, sk], rtab_ref, 3, _DIFF_HALF).astype(BF16)


def _attn_in_call(geom, h, mods, layer, rtab, ttab, wts, q_g, kv_g, tm):
    n_lat_tiles = geom.n_lat // tm
    tiles_per_seq = geom.seq // tm

    def pos_block(i):
        return jnp.where(i < n_lat_tiles, i % tiles_per_seq, tiles_per_seq)

    n = geom.n_tok
    w_row, w_col, w_uqt, w_uk, w_uvt, place = wts

    def tok_major(width):
        return _row_spec(tm, width), jax.ShapeDtypeStruct((n, width), BF16)

    def chan_major(rows):
        return pl.BlockSpec((rows, tm), lambda i: (0, i)), jax.ShapeDtypeStruct((rows, n), BF16)

    def chan_major_chunked(rows):
        return (pl.BlockSpec((None, rows, tm), lambda i: (i, 0, 0)),
                jax.ShapeDtypeStruct((n // tm, rows, tm), BF16))

    assert tm == ATT_KC
    outs = [chan_major(_QA_W), tok_major(_QA_W), chan_major_chunked(_VA_W),
            chan_major(DIFF_WIDTH), tok_major(DIFF_WIDTH), chan_major_chunked(DIFF_WIDTH)]
    return pl.pallas_call(
        _attn_in_kernel,
        grid=(n // tm,),
        in_specs=[
            _row_spec(tm, D_MODEL),
            geom.mod_spec(layer, tm),
            pl.BlockSpec((6, tm, LANES), lambda i: (0, pos_block(i), 0)),
            pl.BlockSpec((ROPE_T_ROWS, tm), lambda i: (0, pos_block(i))),
            _full_spec(w_row.shape),
            _full_spec(w_col.shape),
            _full_spec(q_g.shape),
            _full_spec(w_uqt.shape),
            _full_spec(kv_g.shape),
            _full_spec(w_uk.shape),
            _full_spec(w_uvt.shape),
            _full_spec(place.shape),
        ],
        out_specs=[o[0] for o in outs],
        out_shape=[o[1] for o in outs],
        compiler_params=_cparams(("parallel",)),
        name="attn_in",
    )(h, mods, rtab, ttab, w_row, w_col, q_g, w_uqt, kv_g, w_uk, w_uvt, place)


def _rope_tables(seq, tm):
    rows = seq // GRID_W
    row = jnp.repeat(jnp.arange(rows, dtype=F32), GRID_W)
    col = jnp.tile(jnp.arange(GRID_W, dtype=F32), rows)

    def angles(rot_dim):
        n_freq = rot_dim // 4
        inv = ROPE_BASE ** (-jnp.arange(n_freq, dtype=F32) / n_freq)
        return jnp.concatenate([row[:, None] * inv, col[:, None] * inv], axis=-1)

    def build(ang, first_lo, half, live):
        lane = jnp.arange(LANES)
        cos, sin = jnp.cos(ang), jnp.sin(ang)
        zeros = jnp.zeros((seq, LANES), F32)
        a = jnp.where(lane < live, 1.0, 0.0)[None, :] + zeros
        bm, cm = zeros, zeros
        for lo in first_lo:
            a = a.at[:, lo:lo + half].set(cos).at[:, lo + half:lo + 2 * half].set(cos)
            bm = bm.at[:, lo:lo + half].set(-sin)
            cm = cm.at[:, lo + half:lo + 2 * half].set(sin)
        ident = jnp.where(lane < live, 1.0, 0.0)[None, :] + jnp.zeros((tm, LANES), F32)
        z = jnp.zeros((tm, LANES), F32)
        return [jnp.concatenate([a, ident]), jnp.concatenate([bm, z]), jnp.concatenate([cm, z])]

    ang_m = angles(MLA_ROPE)
    ang_d = angles(DIFF_HEAD_DIM)
    rtab = jnp.stack(build(ang_m, (0,), _MLA_HALF, MLA_ROPE)
                     + build(ang_d, (0, DIFF_HEAD_DIM), _DIFF_HALF, LANES))

    def chan(ang, fn, fill):
        return jnp.concatenate([fn(ang).T, jnp.full((ang.shape[1], tm), fill, F32)], axis=1)

    ttab = jnp.concatenate([chan(ang_m, jnp.cos, 1.0), chan(ang_m, jnp.sin, 0.0),
                            chan(ang_d, jnp.cos, 1.0), chan(ang_d, jnp.sin, 0.0)], axis=0)
    return rtab, ttab


def _softmax_pv_t(qt, keys, vals_t):
    s = [_bdot(k, qt) for k in keys]
    m = s[0].max(axis=0, keepdims=True)
    for si in s[1:]:
        m = jnp.maximum(m, si.max(axis=0, keepdims=True))
    num, den = None, None
    for si, vt in zip(s, vals_t):
        p = jnp.exp2(si - m)
        d = p.sum(axis=0, keepdims=True)
        o = _bdot(vt, p.astype(BF16))
        num = o if num is None else num + o
        den = d if den is None else den + d
    return num / den


def _fold_rows(v, op):
    parts = [v[r:r + 8] for r in range(0, v.shape[0], 8)]
    while len(parts) > 1:
        parts = [op(parts[i], parts[i + 1]) for i in range(0, len(parts) - 1, 2)] + (
            [parts[-1]] if len(parts) % 2 else [])
    return parts[0]


def _attn_pipelined(t, n_tiles, q_list, k_lanes, kx_ref, kc_ref, vx_ref, vc_ref, s_refs, m_ref, l_ref, acc_ref,
                    o_ref, combine):
    n_maps = len(q_list)
    tq = q_list[0].shape[1]
    chunks = ([(kc_ref, vc_ref, c) for c in range(kc_ref.shape[0])]
              + [(kx_ref, vx_ref, c) for c in range(kx_ref.shape[0])])
    neg = jnp.full((8, tq), -jnp.inf, F32)
    zero = jnp.zeros((8, tq), F32)

    def score_chunk(dst_ref, ci, q_list, mrun):
        k_ref, _, c = chunks[ci]
        out = []
        for j in range(n_maps):
            s = _bdot(k_ref[c, :, k_lanes[j]], q_list[j])
            dst_ref[j, ci] = s
            out.append(jnp.maximum(mrun[j], _fold_rows(s, jnp.maximum)))
        return out

    def exp_chunk(src_ref, par, ci, m_cur, lrun):
        _, v_ref, c = chunks[ci]
        vt = v_ref[c]
        out = []
        for j in range(n_maps):
            p = jnp.exp2(src_ref[j, ci] - m_cur[j])
            out.append(lrun[j] + _fold_rows(p, jnp.add))
            o = _bdot(vt, p.astype(BF16))
            if ci == 0:
                acc_ref[par, j] = o
            else:
                acc_ref[par, j] += o
        return out

    def step(slot, do_scores, do_exp, do_emit):
        if do_emit:
            outs = [acc_ref[slot, j] / jnp.sum(l_ref[slot, j], axis=0, keepdims=True) for j in range(n_maps)]
            o_ref[...] = combine(outs).astype(o_ref.dtype)
        new_ref, old_ref = s_refs[slot], s_refs[1 - slot]
        lrun, mrun = [zero] * n_maps, [neg] * n_maps
        if do_exp:
            m_old = [jnp.max(m_ref[1 - slot, j], axis=0, keepdims=True) for j in range(n_maps)]
        for ci in range(len(chunks)):
            if do_scores:
                mrun = score_chunk(new_ref, ci, q_list, mrun)
            if do_exp:
                lrun = exp_chunk(old_ref, 1 - slot, ci, m_old, lrun)
        for j in range(n_maps):
            if do_scores:
                m_ref[slot, j] = mrun[j]
            if do_exp:
                l_ref[1 - slot, j] = lrun[j]

    pl.when(t == 0)(functools.partial(step, 0, True, False, False))
    pl.when(t == 1)(functools.partial(step, 1, n_tiles > 1, True, False))
    for slot in (0, 1):
        steady = jnp.logical_and(t % 2 == slot, jnp.logical_and(t > 1, t < n_tiles))
        pl.when(steady)(functools.partial(step, slot, True, True, True))
    if n_tiles > 1:
        pl.when(t == n_tiles)(functools.partial(step, n_tiles % 2, False, True, True))
    pl.when(t == n_tiles + 1)(functools.partial(step, (n_tiles + 1) % 2, False, False, True))


def _mla_maps(q_ref):
    return [q_ref[:HEAD_BLOCK, :], q_ref[HEAD_BLOCK:, :]]


_MLA_K_LANES = (slice(0, HEAD_BLOCK), slice(HEAD_BLOCK, 2 * HEAD_BLOCK))
_DIFF_K_LANES = (slice(0, HEAD_BLOCK), slice(0, HEAD_BLOCK))


def _mla_combine(outs):
    row = lax.broadcasted_iota(jnp.int32, outs[0].shape, 0)
    return jnp.where(row < MLA_V, outs[0], outs[1]).T


def _diff_maps(q_ref):
    q = q_ref[...]
    row = lax.broadcasted_iota(jnp.int32, q.shape, 0)
    zero = jnp.zeros_like(q)
    return [jnp.where(row < DIFF_HEAD_DIM, q, zero), jnp.where(row >= DIFF_HEAD_DIM, q, zero)]


def _diff_lambda(lam_ref, lam_init):
    lv = lam_ref[...]
    return (jnp.exp(jnp.sum(lv[0:1] * lv[1:2], axis=-1, keepdims=True))
            - jnp.exp(jnp.sum(lv[2:3] * lv[3:4], axis=-1, keepdims=True)) + lam_init)


def _diff_combine(outs, lam, lam_init, g):
    return _rms_rows((outs[0] - lam * outs[1]).T, g) * (1.0 - lam_init)


def _mla_lat_kernel(q_ref, kx_ref, kc_ref, vx_ref, vc_ref, o_ref, s0_ref, s1_ref, m_ref, l_ref, acc_ref, *,
                    n_tiles):
    _attn_pipelined(pl.program_id(0), n_tiles, _mla_maps(q_ref), _MLA_K_LANES, kx_ref, kc_ref, vx_ref, vc_ref,
                    (s0_ref, s1_ref), m_ref, l_ref, acc_ref, o_ref, _mla_combine)


def _diff_lat_kernel(lam_ref, g_ref, q_ref, kx_ref, kc_ref, vx_ref, vc_ref, o_ref, s0_ref, s1_ref, m_ref, l_ref,
                     acc_ref, *, n_tiles, lam_init):
    combine = functools.partial(_diff_combine, lam=_diff_lambda(lam_ref, lam_init), lam_init=lam_init, g=g_ref[...])
    _attn_pipelined(pl.program_id(0), n_tiles, _diff_maps(q_ref), _DIFF_K_LANES, kx_ref, kc_ref, vx_ref, vc_ref,
                    (s0_ref, s1_ref), m_ref, l_ref, acc_ref, o_ref, combine)


def _ctx_attention(q_maps, k_lanes, kc_ref, vc_ref, grp, qk_w):
    n_cc = kc_ref.shape[0]
    vals = [vc_ref[cc, grp * HEAD_BLOCK:(grp + 1) * HEAD_BLOCK, :] for cc in range(n_cc)]
    outs = []
    for q, lanes in zip(q_maps, k_lanes):
        cols = slice(grp * qk_w + lanes.start, grp * qk_w + lanes.stop)
        outs.append(_softmax_pv_t(q, [kc_ref[cc, :, cols] for cc in range(n_cc)], vals))
    return outs


def _group_rows(q_ref, grp, qk_w):
    return q_ref.at[pl.ds(grp * qk_w, qk_w)]


def _mla_ctx_kernel(q_ref, kc_ref, vc_ref, o_ref):
    qk_w = 2 * HEAD_BLOCK
    for grp in range(MLA_HEADS // 2):
        outs = _ctx_attention(_mla_maps(_group_rows(q_ref, grp, qk_w)), _MLA_K_LANES, kc_ref, vc_ref, grp, qk_w)
        o_ref[:, grp * HEAD_BLOCK:(grp + 1) * HEAD_BLOCK] = _mla_combine(outs).astype(o_ref.dtype)


def _diff_ctx_kernel(lam_ref, g_ref, q_ref, kc_ref, vc_ref, o_ref, *, lam_init):
    lam = _diff_lambda(lam_ref, lam_init)
    for grp in range(DIFF_HEADS):
        outs = _ctx_attention(_diff_maps(_group_rows(q_ref, grp, HEAD_BLOCK)), _DIFF_K_LANES, kc_ref, vc_ref, grp,
                              HEAD_BLOCK)
        o_ref[:, grp * HEAD_BLOCK:(grp + 1) * HEAD_BLOCK] = _diff_combine(outs, lam, lam_init,
                                                                          g_ref[...]).astype(o_ref.dtype)


def _attn_lat_call(geom, kern, name, small, qt, k3, vt3, qk_w, n_groups, tq):
    n_q = geom.seq // tq
    n_lc = geom.seq // ATT_KC
    n_cc = geom.ctx // ATT_KC
    ctx_c0 = geom.n_lat // geom.ctx
    n_tiles = geom.batch * n_groups * n_q

    def item(t):
        t = jnp.clip(t, 0, n_tiles - 1)
        return t // (n_groups * n_q), (t // n_q) % n_groups, t % n_q

    def scored(t):
        return item(t)

    def emitted(t):
        return item(t - 1)

    def written(t):
        return item(t - 2)

    small_specs = [pl.BlockSpec(a.shape, lambda t: (0, 0)) for a in small]
    specs = small_specs + [
        pl.BlockSpec((qk_w, tq), lambda t: (scored(t)[1], scored(t)[0] * n_q + scored(t)[2])),
        pl.BlockSpec((n_lc, ATT_KC, qk_w), lambda t: (scored(t)[0], 0, scored(t)[1])),
        pl.BlockSpec((n_cc, ATT_KC, qk_w), lambda t: (ctx_c0 + scored(t)[0], 0, scored(t)[1])),
        pl.BlockSpec((n_lc, HEAD_BLOCK, ATT_KC), lambda t: (emitted(t)[0], emitted(t)[1], 0)),
        pl.BlockSpec((n_cc, HEAD_BLOCK, ATT_KC), lambda t: (ctx_c0 + emitted(t)[0], emitted(t)[1], 0)),
    ]
    return pl.pallas_call(
        functools.partial(kern, n_tiles=n_tiles),
        grid=(n_tiles + 2,),
        in_specs=specs,
        out_specs=pl.BlockSpec((tq, HEAD_BLOCK), lambda t: (written(t)[0] * n_q + written(t)[2], written(t)[1])),
        out_shape=jax.ShapeDtypeStruct((geom.n_lat, n_groups * HEAD_BLOCK), BF16),
        scratch_shapes=[pltpu.VMEM((2, n_cc + n_lc, ATT_KC, tq), F32),
                        pltpu.VMEM((2, n_cc + n_lc, ATT_KC, tq), F32),
                        pltpu.VMEM((2, 2, 8, tq), F32),
                        pltpu.VMEM((2, 2, 8, tq), F32),
                        pltpu.VMEM((2, 2, HEAD_BLOCK, tq), F32)],
        compiler_params=_cparams(("arbitrary",)),
        name=name,
    )(*small, qt, k3, k3, vt3, vt3)


def _attn_ctx_call(geom, kern, name, small, qt, k3, vt3, qk_w, n_groups):
    n_cc = geom.ctx // ATT_KC
    ctx_c0 = geom.n_lat // geom.ctx
    small_specs = [pl.BlockSpec(a.shape, lambda b: (0, 0)) for a in small]
    specs = small_specs + [
        pl.BlockSpec((n_groups * qk_w, geom.ctx), lambda b: (0, ctx_c0 + b)),
        pl.BlockSpec((n_cc, ATT_KC, n_groups * qk_w), lambda b: (ctx_c0 + b, 0, 0)),
        pl.BlockSpec((n_cc, n_groups * HEAD_BLOCK, ATT_KC), lambda b: (ctx_c0 + b, 0, 0)),
    ]
    return pl.pallas_call(
        kern,
        grid=(geom.batch,),
        in_specs=specs,
        out_specs=pl.BlockSpec((geom.ctx, n_groups * HEAD_BLOCK), lambda b: (b, 0)),
        out_shape=jax.ShapeDtypeStruct((geom.n_ctx, n_groups * HEAD_BLOCK), BF16),
        compiler_params=_cparams(("parallel",)),
        name=name,
    )(*small, qt, k3, vt3)


def _attention(geom, qa, ka, va, dq, dk, dv, lam_vecs, subln_g, lam_init, with_ctx, tq):
    n_chunks = geom.n_tok // ATT_KC
    ka3 = ka.reshape(n_chunks, ATT_KC, _QA_W)
    dk3 = dk.reshape(n_chunks, ATT_KC, DIFF_WIDTH)
    small = [lam_vecs, subln_g]
    lat = (_attn_lat_call(geom, _mla_lat_kernel, "mla_attn", [], qa, ka3, va, 2 * HEAD_BLOCK, MLA_HEADS // 2, tq),
           _attn_lat_call(geom, functools.partial(_diff_lat_kernel, lam_init=lam_init), "diff_attn", small, dq, dk3,
                          dv, HEAD_BLOCK, DIFF_HEADS, tq))
    if not with_ctx:
        return lat, None
    ctx = (_attn_ctx_call(geom, _mla_ctx_kernel, "mla_attn_ctx", [], qa, ka3, va, 2 * HEAD_BLOCK, MLA_HEADS // 2),
           _attn_ctx_call(geom, functools.partial(_diff_ctx_kernel, lam_init=lam_init), "diff_attn_ctx", small, dq,
                          dk3, dv, HEAD_BLOCK, DIFF_HEADS))
    return lat, ctx


def _attn_out_kernel(*refs, n_lat_tiles, with_ctx):
    if with_ctx:
        oa_ref, od_ref, oac_ref, odc_ref, h_ref, mod_ref, w_ref, g_ref, b_ref, o_ref = refs
        is_lat = pl.program_id(0) < n_lat_tiles
        oa = jnp.where(is_lat, oa_ref[...], oac_ref[...])
        od = jnp.where(is_lat, od_ref[...], odc_ref[...])
    else:
        oa_ref, od_ref, h_ref, mod_ref, w_ref, g_ref, b_ref, o_ref = refs
        oa, od = oa_ref[...], od_ref[...]
    y = _bdot(oa, w_ref[:_VA_W, :]) + _bdot(od, w_ref[_VA_W:, :])
    r = mod_ref[5:6, :] * y
    o_ref[...] = _layer_norm_rows(DEEPNORM_ALPHA * h_ref[...] + r, g_ref[...], b_ref[...])


def _attn_out_call(geom, lat, ctx, h, mods, layer, w_out, ln_g, ln_b, tm):
    n_lat_tiles = geom.n_lat // tm
    with_ctx = ctx is not None
    n_rows = geom.n_tok if with_ctx else geom.n_lat
    head_specs = [pl.BlockSpec((tm, _VA_W), lambda i: (jnp.minimum(i, n_lat_tiles - 1), 0)),
                  pl.BlockSpec((tm, DIFF_WIDTH), lambda i: (jnp.minimum(i, n_lat_tiles - 1), 0))]
    heads = list(lat)
    if with_ctx:
        head_specs += [pl.BlockSpec((tm, _VA_W), lambda i: (jnp.maximum(i - n_lat_tiles, 0), 0)),
                       pl.BlockSpec((tm, DIFF_WIDTH), lambda i: (jnp.maximum(i - n_lat_tiles, 0), 0))]
        heads += list(ctx)
    return pl.pallas_call(
        functools.partial(_attn_out_kernel, n_lat_tiles=n_lat_tiles, with_ctx=with_ctx),
        grid=(n_rows // tm,),
        in_specs=head_specs + [
            _row_spec(tm, D_MODEL),
            geom.mod_spec(layer, tm),
            _full_spec(w_out.shape),
            _full_spec((1, D_MODEL)),
            _full_spec((1, D_MODEL)),
        ],
        out_specs=_row_spec(tm, D_MODEL),
        out_shape=jax.ShapeDtypeStruct((n_rows, D_MODEL), F32),
        compiler_params=_cparams(("parallel",)),
        name="attn_out",
    )(*heads, h, mods, w_out, ln_g.reshape(1, -1), ln_b.reshape(1, -1))


SSD_PROJ_COLS = SSD_INNER + SSD_CONV_DIM + 2 * LANES
_S_XBC = SSD_INNER
_S_DT = SSD_INNER + SSD_CONV_DIM


def _ssd_in_kernel(h_ref, hp_ref, hn_ref, mod_ref, w_ref, cw_ref, cb_ref, z_ref, xbc_ref, dt_ref, pad_ref, *,
                   tm, tiles_per_seq, n_lat_tiles, nc):
    i = pl.program_id(0)
    is_lat = i < n_lat_tiles
    pos = i % tiles_per_seq
    has_prev = jnp.logical_and(is_lat, pos > 0)
    has_next = jnp.logical_and(is_lat, pos < tiles_per_seq - 1)
    scale1 = 1.0 + mod_ref[4:5, :]
    shift = mod_ref[3:4, :]
    u = (h_ref[...] * scale1 + shift).astype(BF16)
    u_ext = jnp.concatenate([(hp_ref[...] * scale1 + shift).astype(BF16), u,
                             (hn_ref[...] * scale1 + shift).astype(BF16)], axis=0)
    for lo in range(0, SSD_INNER, nc):
        z_ref[:, lo:lo + nc] = _bdot(u, w_ref[:, lo:lo + nc]).astype(BF16)
    dt = _bdot(u, w_ref[:, _S_DT:])
    dt_ref[0] = dt[:, :LANES]
    dt_ref[1] = dt[:, LANES:]
    half = SSD_CONV // 2
    for lo in range(0, SSD_CONV_DIM, nc):
        cs = slice(lo, lo + nc)
        r = _bdot(u_ext, w_ref[:, _S_XBC + lo:_S_XBC + lo + nc])
        head, tail = r[:HALO], r[HALO + tm:]
        r = jnp.concatenate([jnp.where(has_prev, head, jnp.zeros_like(head)), r[HALO:HALO + tm],
                             jnp.where(has_next, tail, jnp.zeros_like(tail))], axis=0)
        acc = cb_ref[:, cs] + jnp.zeros((tm, nc), F32)
        for k in range(SSD_CONV):
            sh = r if k == half else pltpu.roll(r, (half - k) % r.shape[0], axis=0)
            acc = acc + sh[HALO:HALO + tm] * cw_ref[k:k + 1, cs]
        xbc_ref[:, cs] = (acc * _sigmoid(acc)).astype(BF16)


def _ssd_in_call(geom, h, mods, layer, w_in, conv_w, conv_b, tm):
    n = geom.n_tok
    nc = 512
    hb = tm // HALO
    last_hb = n // HALO - 1
    kern = functools.partial(_ssd_in_kernel, tm=tm, tiles_per_seq=geom.seq // tm, n_lat_tiles=geom.n_lat // tm, nc=nc)
    return pl.pallas_call(
        kern,
        grid=(n // tm,),
        in_specs=[
            _row_spec(tm, D_MODEL),
            pl.BlockSpec((HALO, D_MODEL), lambda i: (jnp.maximum(i * hb - 1, 0), 0)),
            pl.BlockSpec((HALO, D_MODEL), lambda i: (jnp.minimum((i + 1) * hb, last_hb), 0)),
            geom.mod_spec(layer, tm),
            _full_spec(w_in.shape),
            _full_spec(conv_w.shape),
            _full_spec((1, SSD_CONV_DIM)),
        ],
        out_specs=[_row_spec(tm, SSD_INNER), _row_spec(tm, SSD_CONV_DIM),
                   pl.BlockSpec((2, tm, LANES), lambda i: (0, i, 0))],
        out_shape=[jax.ShapeDtypeStruct((n, SSD_INNER), BF16), jax.ShapeDtypeStruct((n, SSD_CONV_DIM), BF16),
                   jax.ShapeDtypeStruct((2, n, LANES), F32)],
        scratch_shapes=[pltpu.VMEM((tm + 2 * HALO, nc), F32)],
        compiler_params=_cparams(("parallel",)),
        name="ssd_in",
    )(h, h, h, mods, w_in, conv_w, conv_b.reshape(1, -1))


_GROUP_W = SSD_INNER // SSD_GROUPS
_HEADS_PER_GROUP = SSD_HEADS // SSD_GROUPS
_B_COL = SSD_INNER
_C_COL = SSD_INNER + SSD_GROUPS * SSD_STATE
SCAN_CPS = 2


def _scan_kernel(xf_ref, xb_ref, dtf_ref, dtb_ref, bias_ref, alog_ref, expand_ref, yf_ref, yb_ref, state_ref):
    @pl.when(pl.program_id(1) == 0)
    def _():
        state_ref[...] = jnp.zeros_like(state_ref)

    for k in range(SCAN_CPS):
        rf = pl.ds(k * SSD_CHUNK, SSD_CHUNK)
        rb = pl.ds((SCAN_CPS - 1 - k) * SSD_CHUNK, SSD_CHUNK)
        _scan_chunk(0, xf_ref.at[rf], dtf_ref.at[rf], bias_ref[0], alog_ref[0], expand_ref, yf_ref.at[rf],
                    state_ref.at[0])
        _scan_chunk(1, xb_ref.at[rb], dtb_ref.at[rb], bias_ref[1], alog_ref[1], expand_ref, yb_ref.at[rb],
                    state_ref.at[1])


def _scan_chunk(d, xbc_ref, dt_ref, bias, alog, expand_ref, y_ref, state_ref):
    t = SSD_CHUNK
    lane = lax.broadcasted_iota(jnp.int32, (1, LANES), 1)
    a = jnp.where(lane < SSD_HEADS, -jnp.exp(alog), 0.0)
    raw = dt_ref[...] + bias
    e = jnp.exp(-jnp.abs(raw))
    u = 1.0 + e
    um1 = u - 1.0
    dt = jnp.maximum(raw, 0.0) + jnp.where(um1 == 0.0, e, jnp.log(u) * (e / jnp.where(um1 == 0.0, 1.0, um1)))
    da = dt * a
    row = lax.broadcasted_iota(jnp.int32, (t, t), 0)
    col = lax.broadcasted_iota(jnp.int32, (t, t), 1)
    mask = (col <= row) if d == 0 else (col >= row)
    tri = mask.astype(BF16)
    da_hi = da.astype(BF16)
    rem = da - da_hi.astype(F32)
    da_mid = rem.astype(BF16)
    da_lo = (rem - da_mid.astype(F32)).astype(BF16)
    cum = _bdot(tri, da_hi) + _bdot(tri, da_mid) + _bdot(tri, da_lo)
    total = jnp.sum(da, axis=0, keepdims=True)
    src_t = (cum - jnp.log(dt)).T
    ecum = jnp.exp(cum)
    dtdec = dt * jnp.exp(total - cum)
    cdec = jnp.broadcast_to(jnp.exp(total), (8, LANES))

    per_head = jnp.concatenate([dtdec, ecum, cdec], axis=0).astype(BF16)

    lane_t = lax.broadcasted_iota(jnp.int32, (t, LANES), 1)
    mask_bias = jnp.where(mask, 0.0, -jnp.inf)
    zero_x = jnp.zeros((t, LANES), BF16)
    for g in range(SSD_GROUPS):
        gs = slice(g * _GROUP_W, (g + 1) * _GROUP_W)
        ex = _bdot(per_head, expand_ref[:, gs])
        w_state, w_off, w_carry = ex[0:t], ex[t:2 * t], ex[2 * t:2 * t + 1]
        b_g = xbc_ref[:, _B_COL + g * SSD_STATE:_B_COL + (g + 1) * SSD_STATE]
        c_g = xbc_ref[:, _C_COL + g * SSD_STATE:_C_COL + (g + 1) * SSD_STATE]
        cb = _bdot_nt(c_g, b_g)
        st = state_ref[g]
        y_off = _bdot(c_g, st.astype(BF16)) * w_off
        for pr in range(_HEADS_PER_GROUP // 2):
            xs = slice(g * _GROUP_W + pr * LANES, g * _GROUP_W + (pr + 1) * LANES)
            x_pair = xbc_ref[:, xs]
            ms = []
            for e in range(2):
                hd = g * _HEADS_PER_GROUP + 2 * pr + e
                seg = cum[:, hd:hd + 1] - src_t[hd:hd + 1, :]
                ms.append((cb * jnp.exp(seg + mask_bias)).astype(BF16))
            x_blk = jnp.concatenate([jnp.where(lane_t < SSD_HEAD_DIM, x_pair, zero_x),
                                     jnp.where(lane_t >= SSD_HEAD_DIM, x_pair, zero_x)], axis=0)
            y_pair = _bdot(jnp.concatenate(ms, axis=1), x_blk)
            y_ref[:, xs] = (y_pair + y_off[:, pr * LANES:(pr + 1) * LANES]).astype(y_ref.dtype)
        xw = (xbc_ref[:, gs].astype(F32) * w_state).astype(BF16)
        b_t = b_g.astype(F32).T.astype(BF16)
        state_ref[g] = st * w_carry + _bdot(b_t, xw)


def _scan_call(geom, xbc, dt, dt_bias, a_log):
    t = SCAN_CPS * SSD_CHUNK
    assert geom.ctx % t == 0 and geom.seq % t == 0
    ncc = geom.ctx // t
    nlc = geom.seq // t
    ctx0 = geom.n_lat // t

    def row_block(b, d, s):
        jc = s + d * (ncc - 1 - 2 * s)
        sl = s - ncc
        jl = sl + d * (nlc - 1 - 2 * sl)
        return jnp.where(s < ncc, ctx0 + b * ncc + jc, b * nlc + jl)

    def pad_lanes(v):
        return jnp.pad(v.astype(F32), ((0, 0), (0, LANES - v.shape[-1]))).reshape(2, 1, LANES)

    expand = (jnp.arange(SSD_INNER)[None, :] // SSD_HEAD_DIM == jnp.arange(LANES)[:, None]).astype(BF16)

    y_shape = jax.ShapeDtypeStruct((geom.n_tok, SSD_INNER), BF16)
    return pl.pallas_call(
        _scan_kernel,
        grid=(geom.batch, ncc + nlc),
        in_specs=[
            pl.BlockSpec((t, SSD_CONV_DIM), lambda b, s: (row_block(b, 0, s), 0)),
            pl.BlockSpec((t, SSD_CONV_DIM), lambda b, s: (row_block(b, 1, s), 0)),
            pl.BlockSpec((None, t, LANES), lambda b, s: (0, row_block(b, 0, s), 0)),
            pl.BlockSpec((None, t, LANES), lambda b, s: (1, row_block(b, 1, s), 0)),
            pl.BlockSpec((2, 1, LANES), lambda b, s: (0, 0, 0)),
            pl.BlockSpec((2, 1, LANES), lambda b, s: (0, 0, 0)),
            pl.BlockSpec((LANES, SSD_INNER), lambda b, s: (0, 0)),
        ],
        out_specs=[pl.BlockSpec((t, SSD_INNER), lambda b, s: (row_block(b, 0, s), 0)),
                   pl.BlockSpec((t, SSD_INNER), lambda b, s: (row_block(b, 1, s), 0))],
        out_shape=[y_shape, y_shape],
        scratch_shapes=[pltpu.VMEM((2, SSD_GROUPS, SSD_STATE, _GROUP_W), F32)],
        compiler_params=_cparams(("parallel", "arbitrary")),
        name="ssd_scan",
    )(xbc, xbc, dt, dt, pad_lanes(dt_bias), pad_lanes(a_log), expand)


def _ssd_out_kernel(yf_ref, yb_ref, xbc_ref, z_ref, dskip_ref, ng_ref, h_ref, mod_ref, w_ref, g_ref, b_ref, o_ref):
    dsk = dskip_ref[0:1, :] + dskip_ref[1:2, :]
    y = None
    for g in range(SSD_GROUPS):
        gs = slice(g * _GROUP_W, (g + 1) * _GROUP_W)
        z = z_ref[:, gs].astype(F32)
        ysum = yf_ref[:, gs].astype(F32) + yb_ref[:, gs].astype(F32) + dsk[:, gs] * xbc_ref[:, gs].astype(F32)
        gy = ysum * (z * _sigmoid(z))
        gy = _rms_rows(gy, ng_ref[:, gs]).astype(BF16)
        part = _bdot(gy, w_ref[gs, :])
        y = part if y is None else y + part
    r = mod_ref[5:6, :] * y
    o_ref[...] = _layer_norm_rows(DEEPNORM_ALPHA * h_ref[...] + r, g_ref[...], b_ref[...])


def _ssd_out_call(geom, y, xbc, z, dskip, norm_g, h, mods, layer, w_out, ln_g, ln_b, n_rows, tm):
    return pl.pallas_call(
        _ssd_out_kernel,
        grid=(n_rows // tm,),
        in_specs=[
            _row_spec(tm, SSD_INNER),
            _row_spec(tm, SSD_INNER),
            _row_spec(tm, SSD_INNER),
            _row_spec(tm, SSD_INNER),
            _full_spec(dskip.shape),
            _full_spec((1, SSD_INNER)),
            _row_spec(tm, D_MODEL),
            geom.mod_spec(layer, tm),
            _full_spec(w_out.shape),
            _full_spec((1, D_MODEL)),
            _full_spec((1, D_MODEL)),
        ],
        out_specs=_row_spec(tm, D_MODEL),
        out_shape=jax.ShapeDtypeStruct((n_rows, D_MODEL), F32),
        compiler_params=_cparams(("parallel",)),
        name="ssd_out",
    )(y[0], y[1], xbc, z, dskip, norm_g.reshape(1, -1), h, mods, w_out, ln_g.reshape(1, -1), ln_b.reshape(1, -1))


def _attn_weights(w_in, w_uq, w_ukv):
    d = w_in.shape[0]
    kr_end = MLA_Q_RANK + MLA_KV_RANK + MLA_ROPE
    dk_lo = kr_end + DIFF_WIDTH
    dv_lo = dk_lo + DIFF_WIDTH
    w_row = jnp.concatenate([w_in[:, :kr_end], jnp.zeros((d, HEAD_BLOCK - MLA_ROPE), w_in.dtype),
                             w_in[:, dk_lo:dv_lo]], axis=1).astype(BF16)
    w_col = jnp.concatenate([w_in[:, kr_end:dk_lo], w_in[:, dv_lo:]], axis=1).T.astype(BF16)
    pad_q = HEAD_BLOCK - MLA_QK_DIM
    w_uq_p = jnp.pad(w_uq.reshape(MLA_Q_RANK, MLA_HEADS, MLA_QK_DIM), ((0, 0), (0, 0), (0, pad_q)))
    w_uqt = w_uq_p.reshape(MLA_Q_RANK, _QA_W).T.astype(BF16)
    kv = w_ukv.reshape(MLA_KV_RANK, MLA_HEADS, MLA_NOPE + MLA_V)
    w_uk = jnp.pad(kv[:, :, :MLA_NOPE], ((0, 0), (0, 0), (0, HEAD_BLOCK - MLA_NOPE)))
    w_uk = w_uk.reshape(MLA_KV_RANK, _QA_W).astype(BF16)
    w_uvt = kv[:, :, MLA_NOPE:].reshape(MLA_KV_RANK, _VA_W).T.astype(BF16)
    src = jnp.arange(HEAD_BLOCK)[:, None]
    dst = jnp.arange(_QA_W)[None, :]
    place = ((dst % HEAD_BLOCK == src + MLA_NOPE) & (src < MLA_ROPE)).astype(BF16)
    return w_row, w_col, w_uqt, w_uk, w_uvt, place


def _ssd_weights(w_in):
    d = w_in.shape[0]
    z = jnp.zeros((d, LANES - SSD_HEADS), w_in.dtype)
    return jnp.concatenate([w_in[:, :_S_DT], w_in[:, _S_DT:_S_DT + SSD_HEADS], z, w_in[:, _S_DT + SSD_HEADS:], z],
                           axis=1).astype(BF16)


def _lambda_init_for(layer):
    return 0.8 - 0.6 * math.exp(-0.3 * layer)


def _pick_tile(seq, ctx, n_ctx, want):
    tm = want
    while seq % tm or n_ctx % tm:
        tm //= 2
    return tm


def kernel(x, c, ctx, c_ctx, ada_w, ada_b, ln_g, ln_b, ffn1_w_gu, ffn1_w_down, ffn2_w_gu, ffn2_w_down, attn_w_in,
           mla_q_norm_g, mla_w_uq, mla_kv_norm_g, mla_w_ukv, diff_lam_q1, diff_lam_k1, diff_lam_q2, diff_lam_k2,
           diff_subln_g, attn_w_out, ssd_w_in, ssd_conv_w, ssd_conv_b, ssd_a_log, ssd_dt_bias, ssd_d, ssd_norm_g,
           ssd_w_out):
    batch, seq, d = x.shape
    n_ctx_tok = ctx.shape[1]
    geom = _Geom(batch, seq, n_ctx_tok)
    assert d == D_MODEL and batch + 1 <= MOD_ROWS
    assert seq % GRID_W == 0 and seq % SSD_CHUNK == 0 and n_ctx_tok % SSD_CHUNK == 0
    tq = math.gcd(seq, ATT_TQ)
    assert seq % tq == 0 and seq % ATT_KC == 0 and n_ctx_tok % ATT_KC == 0
    tm_ffn = _pick_tile(seq, n_ctx_tok, geom.n_ctx, 512)
    tm_proj = _pick_tile(seq, n_ctx_tok, geom.n_ctx, 256)
    assert n_ctx_tok % tm_proj == 0
    fc = 256

    cond = jnp.concatenate([c, c_ctx[None, :], jnp.zeros((MOD_ROWS - batch - 1, d), F32)], axis=0)
    mods = _ada_call(cond, ada_w, ada_b).reshape(DEPTH, MOD_ROWS, N_MOD, d)
    rtab, ttab = _rope_tables(seq, tm_proj)

    h = (x.reshape(batch * seq, d), ctx.reshape(batch * n_ctx_tok, d))
    for l in range(DEPTH):
        last = l == DEPTH - 1
        h = _ffn_call(geom, h, mods, l, 0, ffn1_w_gu[l].astype(BF16), ffn1_w_down[l].astype(BF16),
                      ln_g[l, 0], ln_b[l, 0], geom.n_tok, tm_ffn, fc)
        n_out = geom.n_lat if last else geom.n_tok
        if l % 2 == 0:
            a = l // 2
            wts = _attn_weights(attn_w_in[a], mla_w_uq[a], mla_w_ukv[a])
            qa, ka, va, dq, dk, dv = _attn_in_call(geom, h, mods, l, rtab, ttab, wts, mla_q_norm_g[a].reshape(1, -1),
                                                   mla_kv_norm_g[a].reshape(1, -1), tm_proj)
            lam_vecs = jnp.stack([diff_lam_q1[a], diff_lam_k1[a], diff_lam_q2[a], diff_lam_k2[a]])
            lat, ctx_heads = _attention(geom, qa, ka, va, dq, dk, dv, lam_vecs, diff_subln_g[a].reshape(1, -1),
                                        _lambda_init_for(l), not last, tq)
            h = _attn_out_call(geom, lat, ctx_heads, h, mods, l, attn_w_out[a].astype(BF16), ln_g[l, 1],
                               ln_b[l, 1], tm_ffn)
        else:
            s = l // 2
            z, xbc, dt = _ssd_in_call(geom, h, mods, l, _ssd_weights(ssd_w_in[s]), ssd_conv_w[s], ssd_conv_b[s],
                                      tm_proj)
            y = _scan_call(geom, xbc, dt, ssd_dt_bias[s], ssd_a_log[s])
            dskip = jnp.repeat(ssd_d[s], SSD_HEAD_DIM, axis=-1)
            h = _ssd_out_call(geom, y, xbc, z, dskip, ssd_norm_g[s], h, mods, l, ssd_w_out[s].astype(BF16),
                              ln_g[l, 1], ln_b[l, 1], n_out, tm_ffn)
        h = _ffn_call(geom, h, mods, l, 6, ffn2_w_gu[l].astype(BF16), ffn2_w_down[l].astype(BF16),
                      ln_g[l, 2], ln_b[l, 2], n_out, tm_ffn, fc)
    return h[:geom.n_lat].reshape(batch, seq, d)
```

```python
import functools
import math

import jax
import jax.numpy as jnp
from jax import lax
from jax.experimental import pallas as pl
from jax.experimental.pallas import tpu as pltpu

F32 = jnp.float32
BF16 = jnp.bfloat16

D_MODEL = 1024
DEPTH = 4
GRID_W = 64
N_MOD = 9
FFN_DIM = 2816
MACARON_WEIGHT = 0.5
MLA_HEADS = 8
MLA_Q_RANK = 384
MLA_KV_RANK = 256
MLA_NOPE = 64
MLA_ROPE = 32
MLA_V = 64
MLA_QK_DIM = MLA_NOPE + MLA_ROPE
MLA_SCALE = MLA_QK_DIM ** -0.5
DIFF_HEADS = 4
DIFF_HEAD_DIM = 64
DIFF_WIDTH = DIFF_HEADS * 2 * DIFF_HEAD_DIM
DIFF_SCALE = DIFF_HEAD_DIM ** -0.5
ROPE_BASE = 10000.0
SSD_INNER = 2 * D_MODEL
SSD_HEAD_DIM = 64
SSD_HEADS = SSD_INNER // SSD_HEAD_DIM
SSD_GROUPS = 4
SSD_STATE = 128
SSD_CONV = 5
SSD_CHUNK = 128
SSD_CONV_DIM = SSD_INNER + 2 * SSD_GROUPS * SSD_STATE
DEEPNORM_ALPHA = (2.0 * DEPTH) ** 0.25
LN_EPS = 1e-6
RMS_EPS = 1e-6

LANES = 128
V7X_VMEM_LIMIT = 56 * 1024 * 1024
HALO = 16

ROW_TILE_WIDE = 512
ROW_TILE_PROJ = 256
FFN_COL_CHUNK = 256
PROJ_COL_CHUNK = 512

HEAD_BLOCK = LANES
MOD_ROWS = 24


def _cparams(sem):
    return pltpu.CompilerParams(dimension_semantics=sem, vmem_limit_bytes=V7X_VMEM_LIMIT)


def _sigmoid(v):
    return 0.5 * jnp.tanh(0.5 * v) + 0.5


def _layer_norm_rows(v, g, b):
    mu = jnp.mean(v, axis=-1, keepdims=True)
    c = v - mu
    var = jnp.mean(c * c, axis=-1, keepdims=True)
    return c * lax.rsqrt(var + LN_EPS) * g + b


def _rms_rows(v, g):
    return v * lax.rsqrt(jnp.mean(v * v, axis=-1, keepdims=True) + RMS_EPS) * g


def _bdot(a, b):
    return jnp.dot(a, b, preferred_element_type=F32)


def _bdot_nt(a, b):
    return lax.dot_general(a, b, (((1,), (1,)), ((), ())), preferred_element_type=F32)


def _ada_kernel(c_ref, w_ref, b_ref, o_ref):
    c = c_ref[...]
    s = (c * _sigmoid(c)).astype(BF16)
    o_ref[...] = _bdot(s, w_ref[...].astype(BF16)) + b_ref[...]


def _ada_call(cond, ada_w, ada_b):
    depth, d, n = ada_w.shape
    tn = n // 8
    return pl.pallas_call(
        _ada_kernel,
        grid=(depth, n // tn),
        in_specs=[
            pl.BlockSpec((MOD_ROWS, d), lambda l, j: (0, 0)),
            pl.BlockSpec((None, d, tn), lambda l, j: (l, 0, j)),
            pl.BlockSpec((None, 1, tn), lambda l, j: (l, 0, j)),
        ],
        out_specs=pl.BlockSpec((None, MOD_ROWS, tn), lambda l, j: (l, 0, j)),
        out_shape=jax.ShapeDtypeStruct((depth, MOD_ROWS, n), F32),
        compiler_params=_cparams(("parallel", "parallel")),
        name="ada_mod",
    )(cond, ada_w, ada_b.reshape(depth, 1, n))


class _Geom:
    def __init__(self, batch, seq, ctx):
        self.batch, self.seq, self.ctx = batch, seq, ctx
        self.n_lat = batch * seq
        self.n_ctx = batch * ctx
        self.n_tok = self.n_lat + self.n_ctx

    def mod_spec(self, layer, tm):
        n_lat_tiles = self.n_lat // tm
        tiles_per_batch = self.seq // tm
        batch = self.batch

        def index(i):
            return (layer, jnp.where(i < n_lat_tiles, i // tiles_per_batch, batch), 0, 0)

        return pl.BlockSpec((None, None, N_MOD, D_MODEL), index)


def _row_spec(tm, width):
    return pl.BlockSpec((tm, width), lambda i: (i, 0))


def _full_spec(shape):
    nd = len(shape)
    return pl.BlockSpec(shape, lambda i: (0,) * nd)


def _ffn_kernel(*refs, k0, fc, n_lat_tiles):
    if n_lat_tiles is None:
        h_ref, mod_ref, wgu_ref, wd_ref, g_ref, b_ref, o_ref, acc_ref = refs
        h = h_ref[...]
    else:
        hx_ref, hc_ref, mod_ref, wgu_ref, wd_ref, g_ref, b_ref, o_ref, acc_ref = refs
        h = jnp.where(pl.program_id(0) < n_lat_tiles, hx_ref[...], hc_ref[...])
    shift = mod_ref[k0:k0 + 1, :]
    scale = mod_ref[k0 + 1:k0 + 2, :]
    gate = mod_ref[k0 + 2:k0 + 3, :]
    t = (h * (1.0 + scale) + shift).astype(BF16)
    for j in range(FFN_DIM // fc):
        gj = _bdot(t, wgu_ref[:, j * fc:(j + 1) * fc])
        uj = _bdot(t, wgu_ref[:, FFN_DIM + j * fc:FFN_DIM + (j + 1) * fc])
        a = (gj * _sigmoid(gj) * uj).astype(BF16)
        y = _bdot(a, wd_ref[j * fc:(j + 1) * fc, :])
        if j == 0:
            acc_ref[...] = y
        else:
            acc_ref[...] += y
    r = (MACARON_WEIGHT * gate) * acc_ref[...]
    o_ref[...] = _layer_norm_rows(DEEPNORM_ALPHA * h + r, g_ref[...], b_ref[...])


def _ffn_call(geom, h, mods, layer, k0, w_gu, w_down, ln_g, ln_b, n_rows, tm, fc):
    if isinstance(h, tuple):
        n_lat_tiles = geom.n_lat // tm
        h_args = h
        h_specs = [pl.BlockSpec((tm, D_MODEL), lambda i: (jnp.minimum(i, n_lat_tiles - 1), 0)),
                   pl.BlockSpec((tm, D_MODEL), lambda i: (jnp.maximum(i - n_lat_tiles, 0), 0))]
    else:
        n_lat_tiles = None
        h_args = (h,)
        h_specs = [_row_spec(tm, D_MODEL)]
    kern = functools.partial(_ffn_kernel, k0=k0, fc=fc, n_lat_tiles=n_lat_tiles)
    return pl.pallas_call(
        kern,
        grid=(n_rows // tm,),
        in_specs=h_specs + [
            geom.mod_spec(layer, tm),
            _full_spec(w_gu.shape),
            _full_spec(w_down.shape),
            _full_spec((1, D_MODEL)),
            _full_spec((1, D_MODEL)),
        ],
        out_specs=_row_spec(tm, D_MODEL),
        out_shape=jax.ShapeDtypeStruct((n_rows, D_MODEL), F32),
        scratch_shapes=[pltpu.VMEM((tm, D_MODEL), F32)],
        compiler_params=_cparams(("parallel",)),
        name="ffn",
    )(*h_args, mods, w_gu, w_down, ln_g.reshape(1, -1), ln_b.reshape(1, -1))


_QA_W = MLA_HEADS * HEAD_BLOCK
_VA_W = MLA_HEADS * MLA_V
_R_CKV = MLA_Q_RANK
_R_KR = MLA_Q_RANK + MLA_KV_RANK
_R_DK = _R_KR + HEAD_BLOCK
_MLA_HALF = MLA_ROPE // 2
_DIFF_HALF = DIFF_HEAD_DIM // 2
_T_SIN_M = _MLA_HALF
_T_COS_D = 2 * _MLA_HALF
_T_SIN_D = _T_COS_D + _DIFF_HALF
ROPE_T_ROWS = _T_SIN_D + _DIFF_HALF
ATT_KC = 256
ATT_TQ = 512
LOG2E = math.log2(math.e)
MLA_QSCALE = MLA_SCALE * LOG2E
DIFF_QSCALE = DIFF_SCALE * LOG2E


def _rope_block(v, tab_ref, t0, shift):
    left = pltpu.roll(v, LANES - shift, axis=1)
    right = pltpu.roll(v, shift, axis=1)
    return v * tab_ref[t0] + left * tab_ref[t0 + 1] + right * tab_ref[t0 + 2]


def _rope_rows(dst_ref, src, r0, half, cos, sin, scale):
    t1 = src[r0:r0 + half]
    t2 = src[r0 + half:r0 + 2 * half]
    dst_ref[r0:r0 + half, :] = ((t1 * cos - t2 * sin) * scale).astype(dst_ref.dtype)
    dst_ref[r0 + half:r0 + 2 * half, :] = ((t1 * sin + t2 * cos) * scale).astype(dst_ref.dtype)


def _attn_in_kernel(h_ref, mod_ref, rtab_ref, ttab_ref, wrow_ref, wcol_ref, qg_ref, wuqt_ref, kvg_ref, wuk_ref,
                    wuvt_ref, place_ref, qat_ref, ka_ref, vat_ref, dqt_ref, dk_ref, dvt_ref):
    h = h_ref[...]
    u = (h * (1.0 + mod_ref[4:5, :]) + mod_ref[3:4, :]).astype(BF16)
    proj = _bdot(u, wrow_ref[...])
    projt = _bdot_nt(wcol_ref[...], u)

    cqn = _rms_rows(proj[:, :_R_CKV], qg_ref[...]).astype(BF16)
    qt = _bdot_nt(wuqt_ref[...], cqn)
    cos_m = ttab_ref[0:_T_SIN_M, :]
    sin_m = ttab_ref[_T_SIN_M:_T_COS_D, :]
    for hd in range(MLA_HEADS):
        r0 = hd * HEAD_BLOCK
        qat_ref[r0:r0 + MLA_NOPE, :] = (qt[r0:r0 + MLA_NOPE] * MLA_QSCALE).astype(BF16)
        _rope_rows(qat_ref, qt, r0 + MLA_NOPE, _MLA_HALF, cos_m, sin_m, MLA_QSCALE)
        qat_ref[r0 + MLA_QK_DIM:r0 + HEAD_BLOCK, :] = jnp.zeros((HEAD_BLOCK - MLA_QK_DIM, qt.shape[1]), BF16)

    ckvn = _rms_rows(proj[:, _R_CKV:_R_KR], kvg_ref[...]).astype(BF16)
    kr = _rope_block(proj[:, _R_KR:_R_DK], rtab_ref, 0, _MLA_HALF).astype(BF16)
    ka_ref[...] = (_bdot(ckvn, wuk_ref[...]) + _bdot(kr, place_ref[...])).astype(BF16)
    vat_ref[...] = _bdot_nt(wuvt_ref[...], ckvn).astype(BF16)

    cos_d = ttab_ref[_T_COS_D:_T_SIN_D, :]
    sin_d = ttab_ref[_T_SIN_D:ROPE_T_ROWS, :]
    for sub in range(2 * DIFF_HEADS):
        _rope_rows(dqt_ref, projt, sub * DIFF_HEAD_DIM, _DIFF_HALF, cos_d, sin_d, DIFF_QSCALE)
    dvt_ref[...] = projt[DIFF_WIDTH:].astype(BF16)
    for hd in range(DIFF_HEADS):
        sk = slice(_R_DK + hd * HEAD_BLOCK, _R_DK + (hd + 1) * HEAD_BLOCK)
        so = slice(hd * HEAD_BLOCK, (hd + 1) * HEAD_BLOCK)
        dk_ref[:, so] = _rope_block(proj[:, sk], rtab_ref, 3, _DIFF_HALF).astype(BF16)


def _attn_in_call(geom, h, mods, layer, rtab, ttab, wts, q_g, kv_g, tm):
    n_lat_tiles = geom.n_lat // tm
    tiles_per_seq = geom.seq // tm

    def pos_block(i):
        return jnp.where(i < n_lat_tiles, i % tiles_per_seq, tiles_per_seq)

    n = geom.n_tok
    w_row, w_col, w_uqt, w_uk, w_uvt, place = wts

    def tok_major(width):
        return _row_spec(tm, width), jax.ShapeDtypeStruct((n, width), BF16)

    def chan_major(rows):
        return pl.BlockSpec((rows, tm), lambda i: (0, i)), jax.ShapeDtypeStruct((rows, n), BF16)

    def chan_major_chunked(rows):
        return (pl.BlockSpec((None, rows, tm), lambda i: (i, 0, 0)),
                jax.ShapeDtypeStruct((n // tm, rows, tm), BF16))

    assert tm == ATT_KC
    outs = [chan_major(_QA_W), tok_major(_QA_W), chan_major_chunked(_VA_W),
            chan_major(DIFF_WIDTH), tok_major(DIFF_WIDTH), chan_major_chunked(DIFF_WIDTH)]
    return pl.pallas_call(
        _attn_in_kernel,
        grid=(n // tm,),
        in_specs=[
            _row_spec(tm, D_MODEL),
            geom.mod_spec(layer, tm),
            pl.BlockSpec((6, tm, LANES), lambda i: (0, pos_block(i), 0)),
            pl.BlockSpec((ROPE_T_ROWS, tm), lambda i: (0, pos_block(i))),
            _full_spec(w_row.shape),
            _full_spec(w_col.shape),
            _full_spec(q_g.shape),
            _full_spec(w_uqt.shape),
            _full_spec(kv_g.shape),
            _full_spec(w_uk.shape),
            _full_spec(w_uvt.shape),
            _full_spec(place.shape),
        ],
        out_specs=[o[0] for o in outs],
        out_shape=[o[1] for o in outs],
        compiler_params=_cparams(("parallel",)),
        name="attn_in",
    )(h, mods, rtab, ttab, w_row, w_col, q_g, w_uqt, kv_g, w_uk, w_uvt, place)


def _rope_tables(seq, tm):
    rows = seq // GRID_W
    row = jnp.repeat(jnp.arange(rows, dtype=F32), GRID_W)
    col = jnp.tile(jnp.arange(GRID_W, dtype=F32), rows)

    def angles(rot_dim):
        n_freq = rot_dim // 4
        inv = ROPE_BASE ** (-jnp.arange(n_freq, dtype=F32) / n_freq)
        return jnp.concatenate([row[:, None] * inv, col[:, None] * inv], axis=-1)

    def build(ang, first_lo, half, live):
        lane = jnp.arange(LANES)
        cos, sin = jnp.cos(ang), jnp.sin(ang)
        zeros = jnp.zeros((seq, LANES), F32)
        a = jnp.where(lane < live, 1.0, 0.0)[None, :] + zeros
        bm, cm = zeros, zeros
        for lo in first_lo:
            a = a.at[:, lo:lo + half].set(cos).at[:, lo + half:lo + 2 * half].set(cos)
            bm = bm.at[:, lo:lo + half].set(-sin)
            cm = cm.at[:, lo + half:lo + 2 * half].set(sin)
        ident = jnp.where(lane < live, 1.0, 0.0)[None, :] + jnp.zeros((tm, LANES), F32)
        z = jnp.zeros((tm, LANES), F32)
        return [jnp.concatenate([a, ident]), jnp.concatenate([bm, z]), jnp.concatenate([cm, z])]

    ang_m = angles(MLA_ROPE)
    ang_d = angles(DIFF_HEAD_DIM)
    rtab = jnp.stack(build(ang_m, (0,), _MLA_HALF, MLA_ROPE)
                     + build(ang_d, (0, DIFF_HEAD_DIM), _DIFF_HALF, LANES))

    def chan(ang, fn, fill):
        return jnp.concatenate([fn(ang).T, jnp.full((ang.shape[1], tm), fill, F32)], axis=1)

    ttab = jnp.concatenate([chan(ang_m, jnp.cos, 1.0), chan(ang_m, jnp.sin, 0.0),
                            chan(ang_d, jnp.cos, 1.0), chan(ang_d, jnp.sin, 0.0)], axis=0)
    return rtab, ttab


def _softmax_pv_t(qt, keys, vals_t):
    s = [_bdot(k, qt) for k in keys]
    m = s[0].max(axis=0, keepdims=True)
    for si in s[1:]:
        m = jnp.maximum(m, si.max(axis=0, keepdims=True))
    num, den = None, None
    for si, vt in zip(s, vals_t):
        p = jnp.exp2(si - m)
        d = p.sum(axis=0, keepdims=True)
        o = _bdot(vt, p.astype(BF16))
        num = o if num is None else num + o
        den = d if den is None else den + d
    return num / den


def _fold_rows(v, op):
    parts = [v[r:r + 8] for r in range(0, v.shape[0], 8)]
    while len(parts) > 1:
        parts = [op(parts[i], parts[i + 1]) for i in range(0, len(parts) - 1, 2)] + (
            [parts[-1]] if len(parts) % 2 else [])
    return parts[0]


def _attn_pipelined(t, n_tiles, q_list, k_lanes, kx_ref, kc_ref, vx_ref, vc_ref, s_refs, m_ref, l_ref, acc_ref,
                    o_ref, combine):
    n_maps = len(q_list)
    tq = q_list[0].shape[1]
    chunks = ([(kc_ref, vc_ref, c) for c in range(kc_ref.shape[0])]
              + [(kx_ref, vx_ref, c) for c in range(kx_ref.shape[0])])
    neg = jnp.full((8, tq), -jnp.inf, F32)
    zero = jnp.zeros((8, tq), F32)

    def score_chunk(dst_ref, ci, q_list, mrun):
        k_ref, _, c = chunks[ci]
        out = []
        for j in range(n_maps):
            s = _bdot(k_ref[c, :, k_lanes[j]], q_list[j])
            dst_ref[j, ci] = s
            out.append(jnp.maximum(mrun[j], _fold_rows(s, jnp.maximum)))
        return out

    def exp_chunk(src_ref, par, ci, m_cur, lrun):
        _, v_ref, c = chunks[ci]
        vt = v_ref[c]
        out = []
        for j in range(n_maps):
            p = jnp.exp2(src_ref[j, ci] - m_cur[j])
            out.append(lrun[j] + _fold_rows(p, jnp.add))
            o = _bdot(vt, p.astype(BF16))
            if ci == 0:
                acc_ref[par, j] = o
            else:
                acc_ref[par, j] += o
        return out

    def step(slot, do_scores, do_exp, do_emit):
        if do_emit:
            outs = [acc_ref[slot, j] / jnp.sum(l_ref[slot, j], axis=0, keepdims=True) for j in range(n_maps)]
            o_ref[...] = combine(outs).astype(o_ref.dtype)
        new_ref, old_ref = s_refs[slot], s_refs[1 - slot]
        lrun, mrun = [zero] * n_maps, [neg] * n_maps
        if do_exp:
            m_old = [jnp.max(m_ref[1 - slot, j], axis=0, keepdims=True) for j in range(n_maps)]
        for ci in range(len(chunks)):
            if do_scores:
                mrun = score_chunk(new_ref, ci, q_list, mrun)
            if do_exp:
                lrun = exp_chunk(old_ref, 1 - slot, ci, m_old, lrun)
        for j in range(n_maps):
            if do_scores:
                m_ref[slot, j] = mrun[j]
            if do_exp:
                l_ref[1 - slot, j] = lrun[j]

    pl.when(t == 0)(functools.partial(step, 0, True, False, False))
    pl.when(t == 1)(functools.partial(step, 1, n_tiles > 1, True, False))
    for slot in (0, 1):
        steady = jnp.logical_and(t % 2 == slot, jnp.logical_and(t > 1, t < n_tiles))
        pl.when(steady)(functools.partial(step, slot, True, True, True))
    if n_tiles > 1:
        pl.when(t == n_tiles)(functools.partial(step, n_tiles % 2, False, True, True))
    pl.when(t == n_tiles + 1)(functools.partial(step, (n_tiles + 1) % 2, False, False, True))


def _mla_maps(q_ref):
    return [q_ref[:HEAD_BLOCK, :], q_ref[HEAD_BLOCK:, :]]


_MLA_K_LANES = (slice(0, HEAD_BLOCK), slice(HEAD_BLOCK, 2 * HEAD_BLOCK))
_DIFF_K_LANES = (slice(0, HEAD_BLOCK), slice(0, HEAD_BLOCK))


def _mla_combine(outs):
    row = lax.broadcasted_iota(jnp.int32, outs[0].shape, 0)
    return jnp.where(row < MLA_V, outs[0], outs[1]).T


def _diff_maps(q_ref):
    q = q_ref[...]
    row = lax.broadcasted_iota(jnp.int32, q.shape, 0)
    zero = jnp.zeros_like(q)
    return [jnp.where(row < DIFF_HEAD_DIM, q, zero), jnp.where(row >= DIFF_HEAD_DIM, q, zero)]


def _diff_lambda(lam_ref, lam_init):
    lv = lam_ref[...]
    return (jnp.exp(jnp.sum(lv[0:1] * lv[1:2], axis=-1, keepdims=True))
            - jnp.exp(jnp.sum(lv[2:3] * lv[3:4], axis=-1, keepdims=True)) + lam_init)


def _diff_combine(outs, lam, lam_init, g):
    return _rms_rows((outs[0] - lam * outs[1]).T, g) * (1.0 - lam_init)


def _mla_lat_kernel(q_ref, kx_ref, kc_ref, vx_ref, vc_ref, o_ref, s0_ref, s1_ref, m_ref, l_ref, acc_ref, *,
                    n_tiles):
    _attn_pipelined(pl.program_id(0), n_tiles, _mla_maps(q_ref), _MLA_K_LANES, kx_ref, kc_ref, vx_ref, vc_ref,
                    (s0_ref, s1_ref), m_ref, l_ref, acc_ref, o_ref, _mla_combine)


def _diff_lat_kernel(lam_ref, g_ref, q_ref, kx_ref, kc_ref, vx_ref, vc_ref, o_ref, s0_ref, s1_ref, m_ref, l_ref,
                     acc_ref, *, n_tiles, lam_init):
    combine = functools.partial(_diff_combine, lam=_diff_lambda(lam_ref, lam_init), lam_init=lam_init, g=g_ref[...])
    _attn_pipelined(pl.program_id(0), n_tiles, _diff_maps(q_ref), _DIFF_K_LANES, kx_ref, kc_ref, vx_ref, vc_ref,
                    (s0_ref, s1_ref), m_ref, l_ref, acc_ref, o_ref, combine)


def _ctx_attention(q_maps, k_lanes, kc_ref, vc_ref, grp, qk_w):
    n_cc = kc_ref.shape[0]
    vals = [vc_ref[cc, grp * HEAD_BLOCK:(grp + 1) * HEAD_BLOCK, :] for cc in range(n_cc)]
    outs = []
    for q, lanes in zip(q_maps, k_lanes):
        cols = slice(grp * qk_w + lanes.start, grp * qk_w + lanes.stop)
        outs.append(_softmax_pv_t(q, [kc_ref[cc, :, cols] for cc in range(n_cc)], vals))
    return outs


def _group_rows(q_ref, grp, qk_w):
    return q_ref.at[pl.ds(grp * qk_w, qk_w)]


def _mla_ctx_kernel(q_ref, kc_ref, vc_ref, o_ref):
    qk_w = 2 * HEAD_BLOCK
    for grp in range(MLA_HEADS // 2):
        outs = _ctx_attention(_mla_maps(_group_rows(q_ref, grp, qk_w)), _MLA_K_LANES, kc_ref, vc_ref, grp, qk_w)
        o_ref[:, grp * HEAD_BLOCK:(grp + 1) * HEAD_BLOCK] = _mla_combine(outs).astype(o_ref.dtype)


def _diff_ctx_kernel(lam_ref, g_ref, q_ref, kc_ref, vc_ref, o_ref, *, lam_init):
    lam = _diff_lambda(lam_ref, lam_init)
    for grp in range(DIFF_HEADS):
        outs = _ctx_attention(_diff_maps(_group_rows(q_ref, grp, HEAD_BLOCK)), _DIFF_K_LANES, kc_ref, vc_ref, grp,
                              HEAD_BLOCK)
        o_ref[:, grp * HEAD_BLOCK:(grp + 1) * HEAD_BLOCK] = _diff_combine(outs, lam, lam_init,
                                                                          g_ref[...]).astype(o_ref.dtype)


def _attn_lat_call(geom, kern, name, small, qt, k3, vt3, qk_w, n_groups, tq):
    n_q = geom.seq // tq
    n_lc = geom.seq // ATT_KC
    n_cc = geom.ctx // ATT_KC
    ctx_c0 = geom.n_lat // geom.ctx
    n_tiles = geom.batch * n_groups * n_q

    def item(t):
        t = jnp.clip(t, 0, n_tiles - 1)
        return t // (n_groups * n_q), (t // n_q) % n_groups, t % n_q

    def scored(t):
        return item(t)

    def emitted(t):
        return item(t - 1)

    def written(t):
        return item(t - 2)

    small_specs = [pl.BlockSpec(a.shape, lambda t: (0, 0)) for a in small]
    specs = small_specs + [
        pl.BlockSpec((qk_w, tq), lambda t: (scored(t)[1], scored(t)[0] * n_q + scored(t)[2])),
        pl.BlockSpec((n_lc, ATT_KC, qk_w), lambda t: (scored(t)[0], 0, scored(t)[1])),
        pl.BlockSpec((n_cc, ATT_KC, qk_w), lambda t: (ctx_c0 + scored(t)[0], 0, scored(t)[1])),
        pl.BlockSpec((n_lc, HEAD_BLOCK, ATT_KC), lambda t: (emitted(t)[0], emitted(t)[1], 0)),
        pl.BlockSpec((n_cc, HEAD_BLOCK, ATT_KC), lambda t: (ctx_c0 + emitted(t)[0], emitted(t)[1], 0)),
    ]
    return pl.pallas_call(
        functools.partial(kern, n_tiles=n_tiles),
        grid=(n_tiles + 2,),
        in_specs=specs,
        out_specs=pl.BlockSpec((tq, HEAD_BLOCK), lambda t: (written(t)[0] * n_q + written(t)[2], written(t)[1])),
        out_shape=jax.ShapeDtypeStruct((geom.n_lat, n_groups * HEAD_BLOCK), BF16),
        scratch_shapes=[pltpu.VMEM((2, n_cc + n_lc, ATT_KC, tq), F32),
                        pltpu.VMEM((2, n_cc + n_lc, ATT_KC, tq), F32),
                        pltpu.VMEM((2, 2, 8, tq), F32),
                        pltpu.VMEM((2, 2, 8, tq), F32),
                        pltpu.VMEM((2, 2, HEAD_BLOCK, tq), F32)],
        compiler_params=_cparams(("arbitrary",)),
        name=name,
    )(*small, qt, k3, k3, vt3, vt3)


def _attn_ctx_call(geom, kern, name, small, qt, k3, vt3, qk_w, n_groups):
    n_cc = geom.ctx // ATT_KC
    ctx_c0 = geom.n_lat // geom.ctx
    small_specs = [pl.BlockSpec(a.shape, lambda b: (0, 0)) for a in small]
    specs = small_specs + [
        pl.BlockSpec((n_groups * qk_w, geom.ctx), lambda b: (0, ctx_c0 + b)),
        pl.BlockSpec((n_cc, ATT_KC, n_groups * qk_w), lambda b: (ctx_c0 + b, 0, 0)),
        pl.BlockSpec((n_cc, n_groups * HEAD_BLOCK, ATT_KC), lambda b: (ctx_c0 + b, 0, 0)),
    ]
    return pl.pallas_call(
        kern,
        grid=(geom.batch,),
        in_specs=specs,
        out_specs=pl.BlockSpec((geom.ctx, n_groups * HEAD_BLOCK), lambda b: (b, 0)),
        out_shape=jax.ShapeDtypeStruct((geom.n_ctx, n_groups * HEAD_BLOCK), BF16),
        compiler_params=_cparams(("parallel",)),
        name=name,
    )(*small, qt, k3, vt3)


def _attention(geom, qa, ka, va, dq, dk, dv, lam_vecs, subln_g, lam_init, with_ctx, tq):
    n_chunks = geom.n_tok // ATT_KC
    ka3 = ka.reshape(n_chunks, ATT_KC, _QA_W)
    dk3 = dk.reshape(n_chunks, ATT_KC, DIFF_WIDTH)
    small = [lam_vecs, subln_g]
    lat = (_attn_lat_call(geom, _mla_lat_kernel, "mla_attn", [], qa, ka3, va, 2 * HEAD_BLOCK, MLA_HEADS // 2, tq),
           _attn_lat_call(geom, functools.partial(_diff_lat_kernel, lam_init=lam_init), "diff_attn", small, dq, dk3,
                          dv, HEAD_BLOCK, DIFF_HEADS, tq))
    if not with_ctx:
        return lat, None
    ctx = (_attn_ctx_call(geom, _mla_ctx_kernel, "mla_attn_ctx", [], qa, ka3, va, 2 * HEAD_BLOCK, MLA_HEADS // 2),
           _attn_ctx_call(geom, functools.partial(_diff_ctx_kernel, lam_init=lam_init), "diff_attn_ctx", small, dq,
                          dk3, dv, HEAD_BLOCK, DIFF_HEADS))
    return lat, ctx


def _attn_out_kernel(*refs, n_lat_tiles, with_ctx):
    if with_ctx:
        oa_ref, od_ref, oac_ref, odc_ref, h_ref, mod_ref, w_ref, g_ref, b_ref, o_ref = refs
        is_lat = pl.program_id(0) < n_lat_tiles
        oa = jnp.where(is_lat, oa_ref[...], oac_ref[...])
        od = jnp.where(is_lat, od_ref[...], odc_ref[...])
    else:
        oa_ref, od_ref, h_ref, mod_ref, w_ref, g_ref, b_ref, o_ref = refs
        oa, od = oa_ref[...], od_ref[...]
    y = _bdot(oa, w_ref[:_VA_W, :]) + _bdot(od, w_ref[_VA_W:, :])
    r = mod_ref[5:6, :] * y
    o_ref[...] = _layer_norm_rows(DEEPNORM_ALPHA * h_ref[...] + r, g_ref[...], b_ref[...])


def _attn_out_call(geom, lat, ctx, h, mods, layer, w_out, ln_g, ln_b, tm):
    n_lat_tiles = geom.n_lat // tm
    with_ctx = ctx is not None
    n_rows = geom.n_tok if with_ctx else geom.n_lat
    head_specs = [pl.BlockSpec((tm, _VA_W), lambda i: (jnp.minimum(i, n_lat_tiles - 1), 0)),
                  pl.BlockSpec((tm, DIFF_WIDTH), lambda i: (jnp.minimum(i, n_lat_tiles - 1), 0))]
    heads = list(lat)
    if with_ctx:
        head_specs += [pl.BlockSpec((tm, _VA_W), lambda i: (jnp.maximum(i - n_lat_tiles, 0), 0)),
                       pl.BlockSpec((tm, DIFF_WIDTH), lambda i: (jnp.maximum(i - n_lat_tiles, 0), 0))]
        heads += list(ctx)
    return pl.pallas_call(
        functools.partial(_attn_out_kernel, n_lat_tiles=n_lat_tiles, with_ctx=with_ctx),
        grid=(n_rows // tm,),
        in_specs=head_specs + [
            _row_spec(tm, D_MODEL),
            geom.mod_spec(layer, tm),
            _full_spec(w_out.shape),
            _full_spec((1, D_MODEL)),
            _full_spec((1, D_MODEL)),
        ],
        out_specs=_row_spec(tm, D_MODEL),
        out_shape=jax.ShapeDtypeStruct((n_rows, D_MODEL), F32),
        compiler_params=_cparams(("parallel",)),
        name="attn_out",
    )(*heads, h, mods, w_out, ln_g.reshape(1, -1), ln_b.reshape(1, -1))


_S_XBC = SSD_INNER
_S_DT = SSD_INNER + SSD_CONV_DIM


def _ssd_in_kernel(h_ref, hp_ref, hn_ref, mod_ref, w_ref, cw_ref, cb_ref, z_ref, xbc_ref, dt_ref, pad_ref, *,
                   tm, tiles_per_seq, n_lat_tiles, nc):
    i = pl.program_id(0)
    is_lat = i < n_lat_tiles
    pos = i % tiles_per_seq
    has_prev = jnp.logical_and(is_lat, pos > 0)
    has_next = jnp.logical_and(is_lat, pos < tiles_per_seq - 1)
    scale1 = 1.0 + mod_ref[4:5, :]
    shift = mod_ref[3:4, :]
    u = (h_ref[...] * scale1 + shift).astype(BF16)
    u_ext = jnp.concatenate([(hp_ref[...] * scale1 + shift).astype(BF16), u,
                             (hn_ref[...] * scale1 + shift).astype(BF16)], axis=0)
    for lo in range(0, SSD_INNER, nc):
        z_ref[:, lo:lo + nc] = _bdot(u, w_ref[:, lo:lo + nc]).astype(BF16)
    dt = _bdot(u, w_ref[:, _S_DT:])
    dt_ref[0] = dt[:, :LANES]
    dt_ref[1] = dt[:, LANES:]
    half = SSD_CONV // 2
    for lo in range(0, SSD_CONV_DIM, nc):
        cs = slice(lo, lo + nc)
        r = _bdot(u_ext, w_ref[:, _S_XBC + lo:_S_XBC + lo + nc])
        head, tail = r[:HALO], r[HALO + tm:]
        r = jnp.concatenate([jnp.where(has_prev, head, jnp.zeros_like(head)), r[HALO:HALO + tm],
                             jnp.where(has_next, tail, jnp.zeros_like(tail))], axis=0)
        acc = cb_ref[:, cs] + jnp.zeros((tm, nc), F32)
        for k in range(SSD_CONV):
            sh = r if k == half else pltpu.roll(r, (half - k) % r.shape[0], axis=0)
            acc = acc + sh[HALO:HALO + tm] * cw_ref[k:k + 1, cs]
        xbc_ref[:, cs] = (acc * _sigmoid(acc)).astype(BF16)


def _ssd_in_call(geom, h, mods, layer, w_in, conv_w, conv_b, tm):
    n = geom.n_tok
    nc = PROJ_COL_CHUNK
    hb = tm // HALO
    last_hb = n // HALO - 1
    kern = functools.partial(_ssd_in_kernel, tm=tm, tiles_per_seq=geom.seq // tm, n_lat_tiles=geom.n_lat // tm, nc=nc)
    return pl.pallas_call(
        kern,
        grid=(n // tm,),
        in_specs=[
            _row_spec(tm, D_MODEL),
            pl.BlockSpec((HALO, D_MODEL), lambda i: (jnp.maximum(i * hb - 1, 0), 0)),
            pl.BlockSpec((HALO, D_MODEL), lambda i: (jnp.minimum((i + 1) * hb, last_hb), 0)),
            geom.mod_spec(layer, tm),
            _full_spec(w_in.shape),
            _full_spec(conv_w.shape),
            _full_spec((1, SSD_CONV_DIM)),
        ],
        out_specs=[_row_spec(tm, SSD_INNER), _row_spec(tm, SSD_CONV_DIM),
                   pl.BlockSpec((2, tm, LANES), lambda i: (0, i, 0))],
        out_shape=[jax.ShapeDtypeStruct((n, SSD_INNER), BF16), jax.ShapeDtypeStruct((n, SSD_CONV_DIM), BF16),
                   jax.ShapeDtypeStruct((2, n, LANES), F32)],
        scratch_shapes=[pltpu.VMEM((tm + 2 * HALO, nc), F32)],
        compiler_params=_cparams(("parallel",)),
        name="ssd_in",
    )(h, h, h, mods, w_in, conv_w, conv_b.reshape(1, -1))


_GROUP_W = SSD_INNER // SSD_GROUPS
_HEADS_PER_GROUP = SSD_HEADS // SSD_GROUPS
_B_COL = SSD_INNER
_C_COL = SSD_INNER + SSD_GROUPS * SSD_STATE
SCAN_CPS = 2


def _scan_kernel(xf_ref, xb_ref, dtf_ref, dtb_ref, bias_ref, alog_ref, expand_ref, yf_ref, yb_ref, state_ref):
    @pl.when(pl.program_id(1) == 0)
    def _():
        state_ref[...] = jnp.zeros_like(state_ref)

    for k in range(SCAN_CPS):
        rf = pl.ds(k * SSD_CHUNK, SSD_CHUNK)
        rb = pl.ds((SCAN_CPS - 1 - k) * SSD_CHUNK, SSD_CHUNK)
        _scan_chunk(0, xf_ref.at[rf], dtf_ref.at[rf], bias_ref[0], alog_ref[0], expand_ref, yf_ref.at[rf],
                    state_ref.at[0])
        _scan_chunk(1, xb_ref.at[rb], dtb_ref.at[rb], bias_ref[1], alog_ref[1], expand_ref, yb_ref.at[rb],
                    state_ref.at[1])


def _scan_chunk(d, xbc_ref, dt_ref, bias, alog, expand_ref, y_ref, state_ref):
    t = SSD_CHUNK
    lane = lax.broadcasted_iota(jnp.int32, (1, LANES), 1)
    a = jnp.where(lane < SSD_HEADS, -jnp.exp(alog), 0.0)
    raw = dt_ref[...] + bias
    e = jnp.exp(-jnp.abs(raw))
    u = 1.0 + e
    um1 = u - 1.0
    dt = jnp.maximum(raw, 0.0) + jnp.where(um1 == 0.0, e, jnp.log(u) * (e / jnp.where(um1 == 0.0, 1.0, um1)))
    da = dt * a
    row = lax.broadcasted_iota(jnp.int32, (t, t), 0)
    col = lax.broadcasted_iota(jnp.int32, (t, t), 1)
    mask = (col <= row) if d == 0 else (col >= row)
    tri = mask.astype(BF16)
    da_hi = da.astype(BF16)
    rem = da - da_hi.astype(F32)
    da_mid = rem.astype(BF16)
    da_lo = (rem - da_mid.astype(F32)).astype(BF16)
    cum = _bdot(tri, da_hi) + _bdot(tri, da_mid) + _bdot(tri, da_lo)
    total = jnp.sum(da, axis=0, keepdims=True)
    src_t = (cum - jnp.log(dt)).T
    ecum = jnp.exp(cum)
    dtdec = dt * jnp.exp(total - cum)
    cdec = jnp.broadcast_to(jnp.exp(total), (8, LANES))

    per_head = jnp.concatenate([dtdec, ecum, cdec], axis=0).astype(BF16)

    lane_t = lax.broadcasted_iota(jnp.int32, (t, LANES), 1)
    mask_bias = jnp.where(mask, 0.0, -jnp.inf)
    zero_x = jnp.zeros((t, LANES), BF16)
    for g in range(SSD_GROUPS):
        gs = slice(g * _GROUP_W, (g + 1) * _GROUP_W)
        ex = _bdot(per_head, expand_ref[:, gs])
        w_state, w_off, w_carry = ex[0:t], ex[t:2 * t], ex[2 * t:2 * t + 1]
        b_g = xbc_ref[:, _B_COL + g * SSD_STATE:_B_COL + (g + 1) * SSD_STATE]
        c_g = xbc_ref[:, _C_COL + g * SSD_STATE:_C_COL + (g + 1) * SSD_STATE]
        cb = _bdot_nt(c_g, b_g)
        st = state_ref[g]
        y_off = _bdot(c_g, st.astype(BF16)) * w_off
        for pr in range(_HEADS_PER_GROUP // 2):
            xs = slice(g * _GROUP_W + pr * LANES, g * _GROUP_W + (pr + 1) * LANES)
            x_pair = xbc_ref[:, xs]
            ms = []
            for e in range(2):
                hd = g * _HEADS_PER_GROUP + 2 * pr + e
                seg = cum[:, hd:hd + 1] - src_t[hd:hd + 1, :]
                ms.append((cb * jnp.exp(seg + mask_bias)).astype(BF16))
            x_blk = jnp.concatenate([jnp.where(lane_t < SSD_HEAD_DIM, x_pair, zero_x),
                                     jnp.where(lane_t >= SSD_HEAD_DIM, x_pair, zero_x)], axis=0)
            y_pair = _bdot(jnp.concatenate(ms, axis=1), x_blk)
            y_ref[:, xs] = (y_pair + y_off[:, pr * LANES:(pr + 1) * LANES]).astype(y_ref.dtype)
        xw = (xbc_ref[:, gs].astype(F32) * w_state).astype(BF16)
        b_t = b_g.astype(F32).T.astype(BF16)
        state_ref[g] = st * w_carry + _bdot(b_t, xw)


def _scan_call(geom, xbc, dt, dt_bias, a_log):
    t = SCAN_CPS * SSD_CHUNK
    assert geom.ctx % t == 0 and geom.seq % t == 0
    ncc = geom.ctx // t
    nlc = geom.seq // t
    ctx0 = geom.n_lat // t

    def row_block(b, d, s):
        jc = s + d * (ncc - 1 - 2 * s)
        sl = s - ncc
        jl = sl + d * (nlc - 1 - 2 * sl)
        return jnp.where(s < ncc, ctx0 + b * ncc + jc, b * nlc + jl)

    def pad_lanes(v):
        return jnp.pad(v.astype(F32), ((0, 0), (0, LANES - v.shape[-1]))).reshape(2, 1, LANES)

    expand = (jnp.arange(SSD_INNER)[None, :] // SSD_HEAD_DIM == jnp.arange(LANES)[:, None]).astype(BF16)

    y_shape = jax.ShapeDtypeStruct((geom.n_tok, SSD_INNER), BF16)
    return pl.pallas_call(
        _scan_kernel,
        grid=(geom.batch, ncc + nlc),
        in_specs=[
            pl.BlockSpec((t, SSD_CONV_DIM), lambda b, s: (row_block(b, 0, s), 0)),
            pl.BlockSpec((t, SSD_CONV_DIM), lambda b, s: (row_block(b, 1, s), 0)),
            pl.BlockSpec((None, t, LANES), lambda b, s: (0, row_block(b, 0, s), 0)),
            pl.BlockSpec((None, t, LANES), lambda b, s: (1, row_block(b, 1, s), 0)),
            pl.BlockSpec((2, 1, LANES), lambda b, s: (0, 0, 0)),
            pl.BlockSpec((2, 1, LANES), lambda b, s: (0, 0, 0)),
            pl.BlockSpec((LANES, SSD_INNER), lambda b, s: (0, 0)),
        ],
        out_specs=[pl.BlockSpec((t, SSD_INNER), lambda b, s: (row_block(b, 0, s), 0)),
                   pl.BlockSpec((t, SSD_INNER), lambda b, s: (row_block(b, 1, s), 0))],
        out_shape=[y_shape, y_shape],
        scratch_shapes=[pltpu.VMEM((2, SSD_GROUPS, SSD_STATE, _GROUP_W), F32)],
        compiler_params=_cparams(("parallel", "arbitrary")),
        name="ssd_scan",
    )(xbc, xbc, dt, dt, pad_lanes(dt_bias), pad_lanes(a_log), expand)


def _ssd_out_kernel(yf_ref, yb_ref, xbc_ref, z_ref, dskip_ref, ng_ref, h_ref, mod_ref, w_ref, g_ref, b_ref, o_ref):
    dsk = dskip_ref[0:1, :] + dskip_ref[1:2, :]
    y = None
    for g in range(SSD_GROUPS):
        gs = slice(g * _GROUP_W, (g + 1) * _GROUP_W)
        z = z_ref[:, gs].astype(F32)
        ysum = yf_ref[:, gs].astype(F32) + yb_ref[:, gs].astype(F32) + dsk[:, gs] * xbc_ref[:, gs].astype(F32)
        gy = ysum * (z * _sigmoid(z))
        gy = _rms_rows(gy, ng_ref[:, gs]).astype(BF16)
        part = _bdot(gy, w_ref[gs, :])
        y = part if y is None else y + part
    r = mod_ref[5:6, :] * y
    o_ref[...] = _layer_norm_rows(DEEPNORM_ALPHA * h_ref[...] + r, g_ref[...], b_ref[...])


def _ssd_out_call(geom, y, xbc, z, dskip, norm_g, h, mods, layer, w_out, ln_g, ln_b, n_rows, tm):
    return pl.pallas_call(
        _ssd_out_kernel,
        grid=(n_rows // tm,),
        in_specs=[
            _row_spec(tm, SSD_INNER),
            _row_spec(tm, SSD_INNER),
            _row_spec(tm, SSD_INNER),
            _row_spec(tm, SSD_INNER),
            _full_spec(dskip.shape),
            _full_spec((1, SSD_INNER)),
            _row_spec(tm, D_MODEL),
            geom.mod_spec(layer, tm),
            _full_spec(w_out.shape),
            _full_spec((1, D_MODEL)),
            _full_spec((1, D_MODEL)),
        ],
        out_specs=_row_spec(tm, D_MODEL),
        out_shape=jax.ShapeDtypeStruct((n_rows, D_MODEL), F32),
        compiler_params=_cparams(("parallel",)),
        name="ssd_out",
    )(y[0], y[1], xbc, z, dskip, norm_g.reshape(1, -1), h, mods, w_out, ln_g.reshape(1, -1), ln_b.reshape(1, -1))


def _attn_weights(w_in, w_uq, w_ukv):
    d = w_in.shape[0]
    kr_end = MLA_Q_RANK + MLA_KV_RANK + MLA_ROPE
    dk_lo = kr_end + DIFF_WIDTH
    dv_lo = dk_lo + DIFF_WIDTH
    w_row = jnp.concatenate([w_in[:, :kr_end], jnp.zeros((d, HEAD_BLOCK - MLA_ROPE), w_in.dtype),
                             w_in[:, dk_lo:dv_lo]], axis=1).astype(BF16)
    w_col = jnp.concatenate([w_in[:, kr_end:dk_lo], w_in[:, dv_lo:]], axis=1).T.astype(BF16)
    pad_q = HEAD_BLOCK - MLA_QK_DIM
    w_uq_p = jnp.pad(w_uq.reshape(MLA_Q_RANK, MLA_HEADS, MLA_QK_DIM), ((0, 0), (0, 0), (0, pad_q)))
    w_uqt = w_uq_p.reshape(MLA_Q_RANK, _QA_W).T.astype(BF16)
    kv = w_ukv.reshape(MLA_KV_RANK, MLA_HEADS, MLA_NOPE + MLA_V)
    w_uk = jnp.pad(kv[:, :, :MLA_NOPE], ((0, 0), (0, 0), (0, HEAD_BLOCK - MLA_NOPE)))
    w_uk = w_uk.reshape(MLA_KV_RANK, _QA_W).astype(BF16)
    w_uvt = kv[:, :, MLA_NOPE:].reshape(MLA_KV_RANK, _VA_W).T.astype(BF16)
    src = jnp.arange(HEAD_BLOCK)[:, None]
    dst = jnp.arange(_QA_W)[None, :]
    place = ((dst % HEAD_BLOCK == src + MLA_NOPE) & (src < MLA_ROPE)).astype(BF16)
    return w_row, w_col, w_uqt, w_uk, w_uvt, place


def _ssd_weights(w_in):
    d = w_in.shape[0]
    z = jnp.zeros((d, LANES - SSD_HEADS), w_in.dtype)
    return jnp.concatenate([w_in[:, :_S_DT], w_in[:, _S_DT:_S_DT + SSD_HEADS], z, w_in[:, _S_DT + SSD_HEADS:], z],
                           axis=1).astype(BF16)


def _lambda_init_for(layer):
    return 0.8 - 0.6 * math.exp(-0.3 * layer)


def _pick_tile(seq, ctx, n_ctx, want):
    tm = want
    while seq % tm or n_ctx % tm:
        tm //= 2
    return tm


def kernel(x, c, ctx, c_ctx, ada_w, ada_b, ln_g, ln_b, ffn1_w_gu, ffn1_w_down, ffn2_w_gu, ffn2_w_down, attn_w_in,
           mla_q_norm_g, mla_w_uq, mla_kv_norm_g, mla_w_ukv, diff_lam_q1, diff_lam_k1, diff_lam_q2, diff_lam_k2,
           diff_subln_g, attn_w_out, ssd_w_in, ssd_conv_w, ssd_conv_b, ssd_a_log, ssd_dt_bias, ssd_d, ssd_norm_g,
           ssd_w_out):
    batch, seq, d = x.shape
    n_ctx_tok = ctx.shape[1]
    geom = _Geom(batch, seq, n_ctx_tok)
    assert d == D_MODEL and batch + 1 <= MOD_ROWS
    assert seq % GRID_W == 0 and seq % SSD_CHUNK == 0 and n_ctx_tok % SSD_CHUNK == 0
    tq = math.gcd(seq, ATT_TQ)
    assert seq % tq == 0 and seq % ATT_KC == 0 and n_ctx_tok % ATT_KC == 0
    tm_ffn = _pick_tile(seq, n_ctx_tok, geom.n_ctx, ROW_TILE_WIDE)
    tm_proj = _pick_tile(seq, n_ctx_tok, geom.n_ctx, ROW_TILE_PROJ)
    assert n_ctx_tok % tm_proj == 0
    fc = FFN_COL_CHUNK

    cond = jnp.concatenate([c, c_ctx[None, :], jnp.zeros((MOD_ROWS - batch - 1, d), F32)], axis=0)
    mods = _ada_call(cond, ada_w, ada_b).reshape(DEPTH, MOD_ROWS, N_MOD, d)
    rtab, ttab = _rope_tables(seq, tm_proj)

    h = (x.reshape(batch * seq, d), ctx.reshape(batch * n_ctx_tok, d))
    for l in range(DEPTH):
        last = l == DEPTH - 1
        h = _ffn_call(geom, h, mods, l, 0, ffn1_w_gu[l].astype(BF16), ffn1_w_down[l].astype(BF16),
                      ln_g[l, 0], ln_b[l, 0], geom.n_tok, tm_ffn, fc)
        n_out = geom.n_lat if last else geom.n_tok
        if l % 2 == 0:
            a = l // 2
            wts = _attn_weights(attn_w_in[a], mla_w_uq[a], mla_w_ukv[a])
            qa, ka, va, dq, dk, dv = _attn_in_call(geom, h, mods, l, rtab, ttab, wts, mla_q_norm_g[a].reshape(1, -1),
                                                   mla_kv_norm_g[a].reshape(1, -1), tm_proj)
            lam_vecs = jnp.stack([diff_lam_q1[a], diff_lam_k1[a], diff_lam_q2[a], diff_lam_k2[a]])
            lat, ctx_heads = _attention(geom, qa, ka, va, dq, dk, dv, lam_vecs, diff_subln_g[a].reshape(1, -1),
                                        _lambda_init_for(l), not last, tq)
            h = _attn_out_call(geom, lat, ctx_heads, h, mods, l, attn_w_out[a].astype(BF16), ln_g[l, 1],
                               ln_b[l, 1], tm_ffn)
        else:
            s = l // 2
            z, xbc, dt = _ssd_in_call(geom, h, mods, l, _ssd_weights(ssd_w_in[s]), ssd_conv_w[s], ssd_conv_b[s],
                                      tm_proj)
            y = _scan_call(geom, xbc, dt, ssd_dt_bias[s], ssd_a_log[s])
            dskip = jnp.repeat(ssd_d[s], SSD_HEAD_DIM, axis=-1)
            h = _ssd_out_call(geom, y, xbc, z, dskip, ssd_norm_g[s], h, mods, l, ssd_w_out[s].astype(BF16),
                              ln_g[l, 1], ln_b[l, 1], n_out, tm_ffn)
        h = _ffn_call(geom, h, mods, l, 6, ffn2_w_gu[l].astype(BF16), ffn2_w_down[l].astype(BF16),
                      ln_g[l, 2], ln_b[l, 2], n_out, tm_ffn, fc)
    return h[:geom.n_lat].reshape(batch, seq, d)
```

```python
import functools
import math

import jax
import jax.numpy as jnp
from jax import lax
from jax.experimental import pallas as pl
from jax.experimental.pallas import tpu as pltpu

F32 = jnp.float32
BF16 = jnp.bfloat16

D_MODEL = 1024
DEPTH = 4
GRID_W = 64
N_MOD = 9
FFN_DIM = 2816
MACARON_WEIGHT = 0.5
MLA_HEADS = 8
MLA_Q_RANK = 384
MLA_KV_RANK = 256
MLA_NOPE = 64
MLA_ROPE = 32
MLA_V = 64
MLA_QK_DIM = MLA_NOPE + MLA_ROPE
MLA_SCALE = MLA_QK_DIM ** -0.5
DIFF_HEADS = 4
DIFF_HEAD_DIM = 64
DIFF_WIDTH = DIFF_HEADS * 2 * DIFF_HEAD_DIM
DIFF_SCALE = DIFF_HEAD_DIM ** -0.5
ROPE_BASE = 10000.0
SSD_INNER = 2 * D_MODEL
SSD_HEAD_DIM = 64
SSD_HEADS = SSD_INNER // SSD_HEAD_DIM
SSD_GROUPS = 4
SSD_STATE = 128
SSD_CONV = 5
SSD_CHUNK = 128
SSD_CONV_DIM = SSD_INNER + 2 * SSD_GROUPS * SSD_STATE
DEEPNORM_ALPHA = (2.0 * DEPTH) ** 0.25
LN_EPS = 1e-6
RMS_EPS = 1e-6

LANES = 128
V7X_VMEM_LIMIT = 56 * 1024 * 1024
HALO = 16

ROW_TILE_WIDE = 512
ROW_TILE_PROJ = 256
FFN_COL_CHUNK = 256
PROJ_COL_CHUNK = 512

HEAD_BLOCK = LANES
MOD_ROWS = 24


def _cparams(sem):
    return pltpu.CompilerParams(dimension_semantics=sem, vmem_limit_bytes=V7X_VMEM_LIMIT)


def _sigmoid(v):
    return 0.5 * jnp.tanh(0.5 * v) + 0.5


def _layer_norm_rows(v, g, b):
    mu = jnp.mean(v, axis=-1, keepdims=True)
    c = v - mu
    var = jnp.mean(c * c, axis=-1, keepdims=True)
    return c * lax.rsqrt(var + LN_EPS) * g + b


def _rms_rows(v, g):
    return v * lax.rsqrt(jnp.mean(v * v, axis=-1, keepdims=True) + RMS_EPS) * g


def _bdot(a, b):
    return jnp.dot(a, b, preferred_element_type=F32)


def _bdot_nt(a, b):
    return lax.dot_general(a, b, (((1,), (1,)), ((), ())), preferred_element_type=F32)


def _ada_kernel(c_ref, w_ref, b_ref, o_ref):
    c = c_ref[...]
    s = (c * _sigmoid(c)).astype(BF16)
    o_ref[...] = _bdot(s, w_ref[...].astype(BF16)) + b_ref[...]


def _ada_call(cond, ada_w, ada_b):
    depth, d, n = ada_w.shape
    tn = n // 8
    return pl.pallas_call(
        _ada_kernel,
        grid=(depth, n // tn),
        in_specs=[
            pl.BlockSpec((MOD_ROWS, d), lambda l, j: (0, 0)),
            pl.BlockSpec((None, d, tn), lambda l, j: (l, 0, j)),
            pl.BlockSpec((None, 1, tn), lambda l, j: (l, 0, j)),
        ],
        out_specs=pl.BlockSpec((None, MOD_ROWS, tn), lambda l, j: (l, 0, j)),
        out_shape=jax.ShapeDtypeStruct((depth, MOD_ROWS, n), F32),
        compiler_params=_cparams(("parallel", "parallel")),
        name="ada_mod",
    )(cond, ada_w, ada_b.reshape(depth, 1, n))


class _Geom:
    def __init__(self, batch, seq, ctx):
        self.batch, self.seq, self.ctx = batch, seq, ctx
        self.n_lat = batch * seq
        self.n_ctx = batch * ctx
        self.n_tok = self.n_lat + self.n_ctx

    def mod_spec(self, layer, tm):
        n_lat_tiles = self.n_lat // tm
        tiles_per_batch = self.seq // tm
        batch = self.batch

        def index(i):
            return (layer, jnp.where(i < n_lat_tiles, _div_nonneg(i, tiles_per_batch), batch), 0, 0)

        return pl.BlockSpec((None, None, N_MOD, D_MODEL), index)


def _div_nonneg(v, n):
    if n & (n - 1) == 0:
        return lax.shift_right_logical(v, n.bit_length() - 1)
    return v // n


def _rem_nonneg(v, n):
    if n & (n - 1) == 0:
        return jnp.bitwise_and(v, n - 1)
    return v % n


def _row_spec(tm, width):
    return pl.BlockSpec((tm, width), lambda i: (i, 0))


def _full_spec(shape):
    nd = len(shape)
    return pl.BlockSpec(shape, lambda i: (0,) * nd)


def _ffn_kernel(*refs, k0, fc, n_lat_tiles):
    if n_lat_tiles is None:
        h_ref, mod_ref, wgu_ref, wd_ref, g_ref, b_ref, o_ref, acc_ref = refs
        h = h_ref[...]
    else:
        hx_ref, hc_ref, mod_ref, wgu_ref, wd_ref, g_ref, b_ref, o_ref, acc_ref = refs
        h = jnp.where(pl.program_id(0) < n_lat_tiles, hx_ref[...], hc_ref[...])
    shift = mod_ref[k0:k0 + 1, :]
    scale = mod_ref[k0 + 1:k0 + 2, :]
    gate = mod_ref[k0 + 2:k0 + 3, :]
    t = (h * (1.0 + scale) + shift).astype(BF16)
    for j in range(FFN_DIM // fc):
        gj = _bdot(t, wgu_ref[:, j * fc:(j + 1) * fc])
        uj = _bdot(t, wgu_ref[:, FFN_DIM + j * fc:FFN_DIM + (j + 1) * fc])
        a = (gj * _sigmoid(gj) * uj).astype(BF16)
        y = _bdot(a, wd_ref[j * fc:(j + 1) * fc, :])
        if j == 0:
            acc_ref[...] = y
        else:
            acc_ref[...] += y
    r = (MACARON_WEIGHT * gate) * acc_ref[...]
    o_ref[...] = _layer_norm_rows(DEEPNORM_ALPHA * h + r, g_ref[...], b_ref[...])


def _ffn_call(geom, h, mods, layer, k0, w_gu, w_down, ln_g, ln_b, n_rows, tm, fc):
    if isinstance(h, tuple):
        n_lat_tiles = geom.n_lat // tm
        h_args = h
        h_specs = [pl.BlockSpec((tm, D_MODEL), lambda i: (jnp.minimum(i, n_lat_tiles - 1), 0)),
                   pl.BlockSpec((tm, D_MODEL), lambda i: (jnp.maximum(i - n_lat_tiles, 0), 0))]
    else:
        n_lat_tiles = None
        h_args = (h,)
        h_specs = [_row_spec(tm, D_MODEL)]
    kern = functools.partial(_ffn_kernel, k0=k0, fc=fc, n_lat_tiles=n_lat_tiles)
    return pl.pallas_call(
        kern,
        grid=(n_rows // tm,),
        in_specs=h_specs + [
            geom.mod_spec(layer, tm),
            _full_spec(w_gu.shape),
            _full_spec(w_down.shape),
            _full_spec((1, D_MODEL)),
            _full_spec((1, D_MODEL)),
        ],
        out_specs=_row_spec(tm, D_MODEL),
        out_shape=jax.ShapeDtypeStruct((n_rows, D_MODEL), F32),
        scratch_shapes=[pltpu.VMEM((tm, D_MODEL), F32)],
        compiler_params=_cparams(("parallel",)),
        name="ffn",
    )(*h_args, mods, w_gu, w_down, ln_g.reshape(1, -1), ln_b.reshape(1, -1))


_QA_W = MLA_HEADS * HEAD_BLOCK
_VA_W = MLA_HEADS * MLA_V
_R_CKV = MLA_Q_RANK
_R_KR = MLA_Q_RANK + MLA_KV_RANK
_R_DK = _R_KR + HEAD_BLOCK
_MLA_HALF = MLA_ROPE // 2
_DIFF_HALF = DIFF_HEAD_DIM // 2
_T_SIN_M = _MLA_HALF
_T_COS_D = 2 * _MLA_HALF
_T_SIN_D = _T_COS_D + _DIFF_HALF
ROPE_T_ROWS = _T_SIN_D + _DIFF_HALF
ATT_KC = 256
ATT_TQ = 512
LOG2E = math.log2(math.e)
MLA_QSCALE = MLA_SCALE * LOG2E
DIFF_QSCALE = DIFF_SCALE * LOG2E


def _rope_block(v, tab_ref, t0, shift):
    left = pltpu.roll(v, LANES - shift, axis=1)
    right = pltpu.roll(v, shift, axis=1)
    return v * tab_ref[t0] + left * tab_ref[t0 + 1] + right * tab_ref[t0 + 2]


def _rope_rows(dst_ref, src, r0, half, cos, sin, scale):
    t1 = src[r0:r0 + half]
    t2 = src[r0 + half:r0 + 2 * half]
    dst_ref[r0:r0 + half, :] = ((t1 * cos - t2 * sin) * scale).astype(dst_ref.dtype)
    dst_ref[r0 + half:r0 + 2 * half, :] = ((t1 * sin + t2 * cos) * scale).astype(dst_ref.dtype)


def _attn_in_kernel(h_ref, mod_ref, rtab_ref, ttab_ref, wrow_ref, wcol_ref, qg_ref, wuqt_ref, kvg_ref, wuk_ref,
                    wuvt_ref, place_ref, qat_ref, ka_ref, vat_ref, dqt_ref, dk_ref, dvt_ref):
    h = h_ref[...]
    u = (h * (1.0 + mod_ref[4:5, :]) + mod_ref[3:4, :]).astype(BF16)
    proj = _bdot(u, wrow_ref[...])
    projt = _bdot_nt(wcol_ref[...], u)

    cqn = _rms_rows(proj[:, :_R_CKV], qg_ref[...]).astype(BF16)
    qt = _bdot_nt(wuqt_ref[...], cqn)
    cos_m = ttab_ref[0:_T_SIN_M, :]
    sin_m = ttab_ref[_T_SIN_M:_T_COS_D, :]
    for hd in range(MLA_HEADS):
        r0 = hd * HEAD_BLOCK
        qat_ref[r0:r0 + MLA_NOPE, :] = (qt[r0:r0 + MLA_NOPE] * MLA_QSCALE).astype(BF16)
        _rope_rows(qat_ref, qt, r0 + MLA_NOPE, _MLA_HALF, cos_m, sin_m, MLA_QSCALE)
        qat_ref[r0 + MLA_QK_DIM:r0 + HEAD_BLOCK, :] = jnp.zeros((HEAD_BLOCK - MLA_QK_DIM, qt.shape[1]), BF16)

    ckvn = _rms_rows(proj[:, _R_CKV:_R_KR], kvg_ref[...]).astype(BF16)
    kr = _rope_block(proj[:, _R_KR:_R_DK], rtab_ref, 0, _MLA_HALF).astype(BF16)
    ka_ref[...] = (_bdot(ckvn, wuk_ref[...]) + _bdot(kr, place_ref[...])).astype(BF16)
    vat_ref[...] = _bdot_nt(wuvt_ref[...], ckvn).astype(BF16)

    cos_d = ttab_ref[_T_COS_D:_T_SIN_D, :]
    sin_d = ttab_ref[_T_SIN_D:ROPE_T_ROWS, :]
    for sub in range(2 * DIFF_HEADS):
        _rope_rows(dqt_ref, projt, sub * DIFF_HEAD_DIM, _DIFF_HALF, cos_d, sin_d, DIFF_QSCALE)
    dvt_ref[...] = projt[DIFF_WIDTH:].astype(BF16)
    for hd in range(DIFF_HEADS):
        sk = slice(_R_DK + hd * HEAD_BLOCK, _R_DK + (hd + 1) * HEAD_BLOCK)
        so = slice(hd * HEAD_BLOCK, (hd + 1) * HEAD_BLOCK)
        dk_ref[:, so] = _rope_block(proj[:, sk], rtab_ref, 3, _DIFF_HALF).astype(BF16)


def _attn_in_call(geom, h, mods, layer, rtab, ttab, wts, q_g, kv_g, tm):
    n_lat_tiles = geom.n_lat // tm
    tiles_per_seq = geom.seq // tm

    def pos_block(i):
        return jnp.where(i < n_lat_tiles, _rem_nonneg(i, tiles_per_seq), tiles_per_seq)

    n = geom.n_tok
    w_row, w_col, w_uqt, w_uk, w_uvt, place = wts

    def tok_major(width):
        return _row_spec(tm, width), jax.ShapeDtypeStruct((n, width), BF16)

    def chan_major(rows):
        return pl.BlockSpec((rows, tm), lambda i: (0, i)), jax.ShapeDtypeStruct((rows, n), BF16)

    def chan_major_chunked(rows):
        return (pl.BlockSpec((None, rows, tm), lambda i: (i, 0, 0)),
                jax.ShapeDtypeStruct((n // tm, rows, tm), BF16))

    assert tm == ATT_KC
    outs = [chan_major(_QA_W), tok_major(_QA_W), chan_major_chunked(_VA_W),
            chan_major(DIFF_WIDTH), tok_major(DIFF_WIDTH), chan_major_chunked(DIFF_WIDTH)]
    return pl.pallas_call(
        _attn_in_kernel,
        grid=(n // tm,),
        in_specs=[
            _row_spec(tm, D_MODEL),
            geom.mod_spec(layer, tm),
            pl.BlockSpec((6, tm, LANES), lambda i: (0, pos_block(i), 0)),
            pl.BlockSpec((ROPE_T_ROWS, tm), lambda i: (0, pos_block(i))),
            _full_spec(w_row.shape),
            _full_spec(w_col.shape),
            _full_spec(q_g.shape),
            _full_spec(w_uqt.shape),
            _full_spec(kv_g.shape),
            _full_spec(w_uk.shape),
            _full_spec(w_uvt.shape),
            _full_spec(place.shape),
        ],
        out_specs=[o[0] for o in outs],
        out_shape=[o[1] for o in outs],
        compiler_params=_cparams(("parallel",)),
        name="attn_in",
    )(h, mods, rtab, ttab, w_row, w_col, q_g, w_uqt, kv_g, w_uk, w_uvt, place)


def _rope_tables(seq, tm):
    rows = seq // GRID_W
    row = jnp.repeat(jnp.arange(rows, dtype=F32), GRID_W)
    col = jnp.tile(jnp.arange(GRID_W, dtype=F32), rows)

    def angles(rot_dim):
        n_freq = rot_dim // 4
        inv = ROPE_BASE ** (-jnp.arange(n_freq, dtype=F32) / n_freq)
        return jnp.concatenate([row[:, None] * inv, col[:, None] * inv], axis=-1)

    def build(ang, first_lo, half, live):
        lane = jnp.arange(LANES)
        cos, sin = jnp.cos(ang), jnp.sin(ang)
        zeros = jnp.zeros((seq, LANES), F32)
        a = jnp.where(lane < live, 1.0, 0.0)[None, :] + zeros
        bm, cm = zeros, zeros
        for lo in first_lo:
            a = a.at[:, lo:lo + half].set(cos).at[:, lo + half:lo + 2 * half].set(cos)
            bm = bm.at[:, lo:lo + half].set(-sin)
            cm = cm.at[:, lo + half:lo + 2 * half].set(sin)
        ident = jnp.where(lane < live, 1.0, 0.0)[None, :] + jnp.zeros((tm, LANES), F32)
        z = jnp.zeros((tm, LANES), F32)
        return [jnp.concatenate([a, ident]), jnp.concatenate([bm, z]), jnp.concatenate([cm, z])]

    ang_m = angles(MLA_ROPE)
    ang_d = angles(DIFF_HEAD_DIM)
    rtab = jnp.stack(build(ang_m, (0,), _MLA_HALF, MLA_ROPE)
                     + build(ang_d, (0, DIFF_HEAD_DIM), _DIFF_HALF, LANES))

    def chan(ang, fn, fill):
        return jnp.concatenate([fn(ang).T, jnp.full((ang.shape[1], tm), fill, F32)], axis=1)

    ttab = jnp.concatenate([chan(ang_m, jnp.cos, 1.0), chan(ang_m, jnp.sin, 0.0),
                            chan(ang_d, jnp.cos, 1.0), chan(ang_d, jnp.sin, 0.0)], axis=0)
    return rtab, ttab


def _softmax_pv_t(qt, keys, vals_t):
    s = [_bdot(k, qt) for k in keys]
    m = s[0].max(axis=0, keepdims=True)
    for si in s[1:]:
        m = jnp.maximum(m, si.max(axis=0, keepdims=True))
    num, den = None, None
    for si, vt in zip(s, vals_t):
        p = jnp.exp2(si - m)
        d = p.sum(axis=0, keepdims=True)
        o = _bdot(vt, p.astype(BF16))
        num = o if num is None else num + o
        den = d if den is None else den + d
    return num / den


def _fold_rows(v, op):
    parts = [v[r:r + 8] for r in range(0, v.shape[0], 8)]
    while len(parts) > 1:
        parts = [op(parts[i], parts[i + 1]) for i in range(0, len(parts) - 1, 2)] + (
            [parts[-1]] if len(parts) % 2 else [])
    return parts[0]


def _attn_pipelined(t, n_tiles, q_list, k_lanes, kx_ref, kc_ref, vx_ref, vc_ref, s_refs, m_ref, l_ref, acc_ref,
                    o_ref, combine):
    n_maps = len(q_list)
    tq = q_list[0].shape[1]
    chunks = ([(kc_ref, vc_ref, c) for c in range(kc_ref.shape[0])]
              + [(kx_ref, vx_ref, c) for c in range(kx_ref.shape[0])])
    neg = jnp.full((8, tq), -jnp.inf, F32)
    zero = jnp.zeros((8, tq), F32)

    def score_chunk(dst_ref, ci, q_list, mrun):
        k_ref, _, c = chunks[ci]
        out = []
        for j in range(n_maps):
            s = _bdot(k_ref[c, :, k_lanes[j]], q_list[j])
            dst_ref[j, ci] = s
            out.append(jnp.maximum(mrun[j], _fold_rows(s, jnp.maximum)))
        return out

    def exp_chunk(src_ref, par, ci, m_cur, lrun):
        _, v_ref, c = chunks[ci]
        vt = v_ref[c]
        out = []
        for j in range(n_maps):
            p = jnp.exp2(src_ref[j, ci] - m_cur[j])
            out.append(lrun[j] + _fold_rows(p, jnp.add))
            o = _bdot(vt, p.astype(BF16))
            if ci == 0:
                acc_ref[par, j] = o
            else:
                acc_ref[par, j] += o
        return out

    def step(slot, do_scores, do_exp, do_emit):
        if do_emit:
            outs = [acc_ref[slot, j] / jnp.sum(l_ref[slot, j], axis=0, keepdims=True) for j in range(n_maps)]
            o_ref[...] = combine(outs).astype(o_ref.dtype)
        new_ref, old_ref = s_refs[slot], s_refs[1 - slot]
        lrun, mrun = [zero] * n_maps, [neg] * n_maps
        if do_exp:
            m_old = [jnp.max(m_ref[1 - slot, j], axis=0, keepdims=True) for j in range(n_maps)]
        for ci in range(len(chunks)):
            if do_scores:
                mrun = score_chunk(new_ref, ci, q_list, mrun)
            if do_exp:
                lrun = exp_chunk(old_ref, 1 - slot, ci, m_old, lrun)
        for j in range(n_maps):
            if do_scores:
                m_ref[slot, j] = mrun[j]
            if do_exp:
                l_ref[1 - slot, j] = lrun[j]

    pl.when(t == 0)(functools.partial(step, 0, True, False, False))
    pl.when(t == 1)(functools.partial(step, 1, n_tiles > 1, True, False))
    for slot in (0, 1):
        steady = jnp.logical_and(_rem_nonneg(t, 2) == slot, jnp.logical_and(t > 1, t < n_tiles))
        pl.when(steady)(functools.partial(step, slot, True, True, True))
    if n_tiles > 1:
        pl.when(t == n_tiles)(functools.partial(step, n_tiles % 2, False, True, True))
    pl.when(t == n_tiles + 1)(functools.partial(step, (n_tiles + 1) % 2, False, False, True))


def _mla_maps(q_ref):
    return [q_ref[:HEAD_BLOCK, :], q_ref[HEAD_BLOCK:, :]]


_MLA_K_LANES = (slice(0, HEAD_BLOCK), slice(HEAD_BLOCK, 2 * HEAD_BLOCK))
_DIFF_K_LANES = (slice(0, HEAD_BLOCK), slice(0, HEAD_BLOCK))


def _mla_combine(outs):
    row = lax.broadcasted_iota(jnp.int32, outs[0].shape, 0)
    return jnp.where(row < MLA_V, outs[0], outs[1]).T


def _diff_maps(q_ref):
    q = q_ref[...]
    row = lax.broadcasted_iota(jnp.int32, q.shape, 0)
    zero = jnp.zeros_like(q)
    return [jnp.where(row < DIFF_HEAD_DIM, q, zero), jnp.where(row >= DIFF_HEAD_DIM, q, zero)]


def _diff_lambda(lam_ref, lam_init):
    lv = lam_ref[...]
    return (jnp.exp(jnp.sum(lv[0:1] * lv[1:2], axis=-1, keepdims=True))
            - jnp.exp(jnp.sum(lv[2:3] * lv[3:4], axis=-1, keepdims=True)) + lam_init)


def _diff_combine(outs, lam, lam_init, g):
    return _rms_rows((outs[0] - lam * outs[1]).T, g) * (1.0 - lam_init)


def _mla_lat_kernel(q_ref, kx_ref, kc_ref, vx_ref, vc_ref, o_ref, s0_ref, s1_ref, m_ref, l_ref, acc_ref, *,
                    n_tiles):
    _attn_pipelined(pl.program_id(0), n_tiles, _mla_maps(q_ref), _MLA_K_LANES, kx_ref, kc_ref, vx_ref, vc_ref,
                    (s0_ref, s1_ref), m_ref, l_ref, acc_ref, o_ref, _mla_combine)


def _diff_lat_kernel(lam_ref, g_ref, q_ref, kx_ref, kc_ref, vx_ref, vc_ref, o_ref, s0_ref, s1_ref, m_ref, l_ref,
                     acc_ref, *, n_tiles, lam_init):
    combine = functools.partial(_diff_combine, lam=_diff_lambda(lam_ref, lam_init), lam_init=lam_init, g=g_ref[...])
    _attn_pipelined(pl.program_id(0), n_tiles, _diff_maps(q_ref), _DIFF_K_LANES, kx_ref, kc_ref, vx_ref, vc_ref,
                    (s0_ref, s1_ref), m_ref, l_ref, acc_ref, o_ref, combine)


def _ctx_attention(q_maps, k_lanes, kc_ref, vc_ref, grp, qk_w):
    n_cc = kc_ref.shape[0]
    vals = [vc_ref[cc, grp * HEAD_BLOCK:(grp + 1) * HEAD_BLOCK, :] for cc in range(n_cc)]
    outs = []
    for q, lanes in zip(q_maps, k_lanes):
        cols = slice(grp * qk_w + lanes.start, grp * qk_w + lanes.stop)
        outs.append(_softmax_pv_t(q, [kc_ref[cc, :, cols] for cc in range(n_cc)], vals))
    return outs


def _group_rows(q_ref, grp, qk_w):
    return q_ref.at[pl.ds(grp * qk_w, qk_w)]


def _mla_ctx_kernel(q_ref, kc_ref, vc_ref, o_ref):
    qk_w = 2 * HEAD_BLOCK
    for grp in range(MLA_HEADS // 2):
        outs = _ctx_attention(_mla_maps(_group_rows(q_ref, grp, qk_w)), _MLA_K_LANES, kc_ref, vc_ref, grp, qk_w)
        o_ref[:, grp * HEAD_BLOCK:(grp + 1) * HEAD_BLOCK] = _mla_combine(outs).astype(o_ref.dtype)


def _diff_ctx_kernel(lam_ref, g_ref, q_ref, kc_ref, vc_ref, o_ref, *, lam_init):
    lam = _diff_lambda(lam_ref, lam_init)
    for grp in range(DIFF_HEADS):
        outs = _ctx_attention(_diff_maps(_group_rows(q_ref, grp, HEAD_BLOCK)), _DIFF_K_LANES, kc_ref, vc_ref, grp,
                              HEAD_BLOCK)
        o_ref[:, grp * HEAD_BLOCK:(grp + 1) * HEAD_BLOCK] = _diff_combine(outs, lam, lam_init,
                                                                          g_ref[...]).astype(o_ref.dtype)


def _attn_lat_call(geom, kern, name, small, qt, k3, vt3, qk_w, n_groups, tq):
    n_q = geom.seq // tq
    n_lc = geom.seq // ATT_KC
    n_cc = geom.ctx // ATT_KC
    ctx_c0 = geom.n_lat // geom.ctx
    n_tiles = geom.batch * n_groups * n_q

    def item(t):
        t = jnp.clip(t, 0, n_tiles - 1)
        return _div_nonneg(t, n_groups * n_q), _rem_nonneg(_div_nonneg(t, n_q), n_groups), _rem_nonneg(t, n_q)

    def scored(t):
        return item(t)

    def emitted(t):
        return item(t - 1)

    def written(t):
        return item(t - 2)

    small_specs = [pl.BlockSpec(a.shape, lambda t: (0, 0)) for a in small]
    specs = small_specs + [
        pl.BlockSpec((qk_w, tq), lambda t: (scored(t)[1], scored(t)[0] * n_q + scored(t)[2])),
        pl.BlockSpec((n_lc, ATT_KC, qk_w), lambda t: (scored(t)[0], 0, scored(t)[1])),
        pl.BlockSpec((n_cc, ATT_KC, qk_w), lambda t: (ctx_c0 + scored(t)[0], 0, scored(t)[1])),
        pl.BlockSpec((n_lc, HEAD_BLOCK, ATT_KC), lambda t: (emitted(t)[0], emitted(t)[1], 0)),
        pl.BlockSpec((n_cc, HEAD_BLOCK, ATT_KC), lambda t: (ctx_c0 + emitted(t)[0], emitted(t)[1], 0)),
    ]
    return pl.pallas_call(
        functools.partial(kern, n_tiles=n_tiles),
        grid=(n_tiles + 2,),
        in_specs=specs,
        out_specs=pl.BlockSpec((tq, HEAD_BLOCK), lambda t: (written(t)[0] * n_q + written(t)[2], written(t)[1])),
        out_shape=jax.ShapeDtypeStruct((geom.n_lat, n_groups * HEAD_BLOCK), BF16),
        scratch_shapes=[pltpu.VMEM((2, n_cc + n_lc, ATT_KC, tq), F32),
                        pltpu.VMEM((2, n_cc + n_lc, ATT_KC, tq), F32),
                        pltpu.VMEM((2, 2, 8, tq), F32),
                        pltpu.VMEM((2, 2, 8, tq), F32),
                        pltpu.VMEM((2, 2, HEAD_BLOCK, tq), F32)],
        compiler_params=_cparams(("arbitrary",)),
        name=name,
    )(*small, qt, k3, k3, vt3, vt3)


def _attn_ctx_call(geom, kern, name, small, qt, k3, vt3, qk_w, n_groups):
    n_cc = geom.ctx // ATT_KC
    ctx_c0 = geom.n_lat // geom.ctx
    small_specs = [pl.BlockSpec(a.shape, lambda b: (0, 0)) for a in small]
    specs = small_specs + [
        pl.BlockSpec((n_groups * qk_w, geom.ctx), lambda b: (0, ctx_c0 + b)),
        pl.BlockSpec((n_cc, ATT_KC, n_groups * qk_w), lambda b: (ctx_c0 + b, 0, 0)),
        pl.BlockSpec((n_cc, n_groups * HEAD_BLOCK, ATT_KC), lambda b: (ctx_c0 + b, 0, 0)),
    ]
    return pl.pallas_call(
        kern,
        grid=(geom.batch,),
        in_specs=specs,
        out_specs=pl.BlockSpec((geom.ctx, n_groups * HEAD_BLOCK), lambda b: (b, 0)),
        out_shape=jax.ShapeDtypeStruct((geom.n_ctx, n_groups * HEAD_BLOCK), BF16),
        compiler_params=_cparams(("parallel",)),
        name=name,
    )(*small, qt, k3, vt3)


def _attention(geom, qa, ka, va, dq, dk, dv, lam_vecs, subln_g, lam_init, with_ctx, tq):
    n_chunks = geom.n_tok // ATT_KC
    ka3 = ka.reshape(n_chunks, ATT_KC, _QA_W)
    dk3 = dk.reshape(n_chunks, ATT_KC, DIFF_WIDTH)
    small = [lam_vecs, subln_g]
    lat = (_attn_lat_call(geom, _mla_lat_kernel, "mla_attn", [], qa, ka3, va, 2 * HEAD_BLOCK, MLA_HEADS // 2, tq),
           _attn_lat_call(geom, functools.partial(_diff_lat_kernel, lam_init=lam_init), "diff_attn", small, dq, dk3,
                          dv, HEAD_BLOCK, DIFF_HEADS, tq))
    if not with_ctx:
        return lat, None
    ctx = (_attn_ctx_call(geom, _mla_ctx_kernel, "mla_attn_ctx", [], qa, ka3, va, 2 * HEAD_BLOCK, MLA_HEADS // 2),
           _attn_ctx_call(geom, functools.partial(_diff_ctx_kernel, lam_init=lam_init), "diff_attn_ctx", small, dq,
                          dk3, dv, HEAD_BLOCK, DIFF_HEADS))
    return lat, ctx


def _attn_out_kernel(*refs, n_lat_tiles, with_ctx):
    if with_ctx:
        oa_ref, od_ref, oac_ref, odc_ref, h_ref, mod_ref, w_ref, g_ref, b_ref, o_ref = refs
        is_lat = pl.program_id(0) < n_lat_tiles
        oa = jnp.where(is_lat, oa_ref[...], oac_ref[...])
        od = jnp.where(is_lat, od_ref[...], odc_ref[...])
    else:
        oa_ref, od_ref, h_ref, mod_ref, w_ref, g_ref, b_ref, o_ref = refs
        oa, od = oa_ref[...], od_ref[...]
    y = _bdot(oa, w_ref[:_VA_W, :]) + _bdot(od, w_ref[_VA_W:, :])
    r = mod_ref[5:6, :] * y
    o_ref[...] = _layer_norm_rows(DEEPNORM_ALPHA * h_ref[...] + r, g_ref[...], b_ref[...])


def _attn_out_call(geom, lat, ctx, h, mods, layer, w_out, ln_g, ln_b, tm):
    n_lat_tiles = geom.n_lat // tm
    with_ctx = ctx is not None
    n_rows = geom.n_tok if with_ctx else geom.n_lat
    head_specs = [pl.BlockSpec((tm, _VA_W), lambda i: (jnp.minimum(i, n_lat_tiles - 1), 0)),
                  pl.BlockSpec((tm, DIFF_WIDTH), lambda i: (jnp.minimum(i, n_lat_tiles - 1), 0))]
    heads = list(lat)
    if with_ctx:
        head_specs += [pl.BlockSpec((tm, _VA_W), lambda i: (jnp.maximum(i - n_lat_tiles, 0), 0)),
                       pl.BlockSpec((tm, DIFF_WIDTH), lambda i: (jnp.maximum(i - n_lat_tiles, 0), 0))]
        heads += list(ctx)
    return pl.pallas_call(
        functools.partial(_attn_out_kernel, n_lat_tiles=n_lat_tiles, with_ctx=with_ctx),
        grid=(n_rows // tm,),
        in_specs=head_specs + [
            _row_spec(tm, D_MODEL),
            geom.mod_spec(layer, tm),
            _full_spec(w_out.shape),
            _full_spec((1, D_MODEL)),
            _full_spec((1, D_MODEL)),
        ],
        out_specs=_row_spec(tm, D_MODEL),
        out_shape=jax.ShapeDtypeStruct((n_rows, D_MODEL), F32),
        compiler_params=_cparams(("parallel",)),
        name="attn_out",
    )(*heads, h, mods, w_out, ln_g.reshape(1, -1), ln_b.reshape(1, -1))


_S_XBC = SSD_INNER
_S_DT = SSD_INNER + SSD_CONV_DIM


def _ssd_in_kernel(h_ref, hp_ref, hn_ref, mod_ref, w_ref, cw_ref, cb_ref, z_ref, xbc_ref, dt_ref, pad_ref, *,
                   tm, tiles_per_seq, n_lat_tiles, nc):
    i = pl.program_id(0)
    is_lat = i < n_lat_tiles
    pos = _rem_nonneg(i, tiles_per_seq)
    has_prev = jnp.logical_and(is_lat, pos > 0)
    has_next = jnp.logical_and(is_lat, pos < tiles_per_seq - 1)
    scale1 = 1.0 + mod_ref[4:5, :]
    shift = mod_ref[3:4, :]
    u = (h_ref[...] * scale1 + shift).astype(BF16)
    u_ext = jnp.concatenate([(hp_ref[...] * scale1 + shift).astype(BF16), u,
                             (hn_ref[...] * scale1 + shift).astype(BF16)], axis=0)
    for lo in range(0, SSD_INNER, nc):
        z_ref[:, lo:lo + nc] = _bdot(u, w_ref[:, lo:lo + nc]).astype(BF16)
    dt = _bdot(u, w_ref[:, _S_DT:])
    dt_ref[0] = dt[:, :LANES]
    dt_ref[1] = dt[:, LANES:]
    half = SSD_CONV // 2
    for lo in range(0, SSD_CONV_DIM, nc):
        cs = slice(lo, lo + nc)
        r = _bdot(u_ext, w_ref[:, _S_XBC + lo:_S_XBC + lo + nc])
        head, tail = r[:HALO], r[HALO + tm:]
        r = jnp.concatenate([jnp.where(has_prev, head, jnp.zeros_like(head)), r[HALO:HALO + tm],
                             jnp.where(has_next, tail, jnp.zeros_like(tail))], axis=0)
        acc = cb_ref[:, cs] + jnp.zeros((tm, nc), F32)
        for k in range(SSD_CONV):
            sh = r if k == half else pltpu.roll(r, (half - k) % r.shape[0], axis=0)
            acc = acc + sh[HALO:HALO + tm] * cw_ref[k:k + 1, cs]
        xbc_ref[:, cs] = (acc * _sigmoid(acc)).astype(BF16)


def _ssd_in_call(geom, h, mods, layer, w_in, conv_w, conv_b, tm):
    n = geom.n_tok
    nc = PROJ_COL_CHUNK
    hb = tm // HALO
    last_hb = n // HALO - 1
    kern = functools.partial(_ssd_in_kernel, tm=tm, tiles_per_seq=geom.seq // tm, n_lat_tiles=geom.n_lat // tm, nc=nc)
    return pl.pallas_call(
        kern,
        grid=(n // tm,),
        in_specs=[
            _row_spec(tm, D_MODEL),
            pl.BlockSpec((HALO, D_MODEL), lambda i: (jnp.maximum(i * hb - 1, 0), 0)),
            pl.BlockSpec((HALO, D_MODEL), lambda i: (jnp.minimum((i + 1) * hb, last_hb), 0)),
            geom.mod_spec(layer, tm),
            _full_spec(w_in.shape),
            _full_spec(conv_w.shape),
            _full_spec((1, SSD_CONV_DIM)),
        ],
        out_specs=[_row_spec(tm, SSD_INNER), _row_spec(tm, SSD_CONV_DIM),
                   pl.BlockSpec((2, tm, LANES), lambda i: (0, i, 0))],
        out_shape=[jax.ShapeDtypeStruct((n, SSD_INNER), BF16), jax.ShapeDtypeStruct((n, SSD_CONV_DIM), BF16),
                   jax.ShapeDtypeStruct((2, n, LANES), F32)],
        scratch_shapes=[pltpu.VMEM((tm + 2 * HALO, nc), F32)],
        compiler_params=_cparams(("parallel",)),
        name="ssd_in",
    )(h, h, h, mods, w_in, conv_w, conv_b.reshape(1, -1))


_GROUP_W = SSD_INNER // SSD_GROUPS
_HEADS_PER_GROUP = SSD_HEADS // SSD_GROUPS
_B_COL = SSD_INNER
_C_COL = SSD_INNER + SSD_GROUPS * SSD_STATE
SCAN_CPS = 2


def _scan_kernel(xf_ref, xb_ref, dtf_ref, dtb_ref, bias_ref, alog_ref, expand_ref, yf_ref, yb_ref, state_ref):
    @pl.when(pl.program_id(1) == 0)
    def _():
        state_ref[...] = jnp.zeros_like(state_ref)

    for k in range(SCAN_CPS):
        rf = pl.ds(k * SSD_CHUNK, SSD_CHUNK)
        rb = pl.ds((SCAN_CPS - 1 - k) * SSD_CHUNK, SSD_CHUNK)
        _scan_chunk(0, xf_ref.at[rf], dtf_ref.at[rf], bias_ref[0], alog_ref[0], expand_ref, yf_ref.at[rf],
                    state_ref.at[0])
        _scan_chunk(1, xb_ref.at[rb], dtb_ref.at[rb], bias_ref[1], alog_ref[1], expand_ref, yb_ref.at[rb],
                    state_ref.at[1])


def _scan_chunk(d, xbc_ref, dt_ref, bias, alog, expand_ref, y_ref, state_ref):
    t = SSD_CHUNK
    lane = lax.broadcasted_iota(jnp.int32, (1, LANES), 1)
    a = jnp.where(lane < SSD_HEADS, -jnp.exp(alog), 0.0)
    raw = dt_ref[...] + bias
    e = jnp.exp(-jnp.abs(raw))
    u = 1.0 + e
    um1 = u - 1.0
    dt = jnp.maximum(raw, 0.0) + jnp.where(um1 == 0.0, e, jnp.log(u) * (e / jnp.where(um1 == 0.0, 1.0, um1)))
    da = dt * a
    row = lax.broadcasted_iota(jnp.int32, (t, t), 0)
    col = lax.broadcasted_iota(jnp.int32, (t, t), 1)
    mask = (col <= row) if d == 0 else (col >= row)
    tri = mask.astype(BF16)
    da_hi = da.astype(BF16)
    rem = da - da_hi.astype(F32)
    da_mid = rem.astype(BF16)
    da_lo = (rem - da_mid.astype(F32)).astype(BF16)
    cum = _bdot(tri, da_hi) + _bdot(tri, da_mid) + _bdot(tri, da_lo)
    total = jnp.sum(da, axis=0, keepdims=True)
    src_t = (cum - jnp.log(dt)).T
    ecum = jnp.exp(cum)
    dtdec = dt * jnp.exp(total - cum)
    cdec = jnp.broadcast_to(jnp.exp(total), (8, LANES))

    per_head = jnp.concatenate([dtdec, ecum, cdec], axis=0).astype(BF16)

    lane_t = lax.broadcasted_iota(jnp.int32, (t, LANES), 1)
    mask_bias = jnp.where(mask, 0.0, -jnp.inf)
    zero_x = jnp.zeros((t, LANES), BF16)
    for g in range(SSD_GROUPS):
        gs = slice(g * _GROUP_W, (g + 1) * _GROUP_W)
        ex = _bdot(per_head, expand_ref[:, gs])
        w_state, w_off, w_carry = ex[0:t], ex[t:2 * t], ex[2 * t:2 * t + 1]
        b_g = xbc_ref[:, _B_COL + g * SSD_STATE:_B_COL + (g + 1) * SSD_STATE]
        c_g = xbc_ref[:, _C_COL + g * SSD_STATE:_C_COL + (g + 1) * SSD_STATE]
        cb = _bdot_nt(c_g, b_g)
        st = state_ref[g]
        y_off = _bdot(c_g, st.astype(BF16)) * w_off
        for pr in range(_HEADS_PER_GROUP // 2):
            xs = slice(g * _GROUP_W + pr * LANES, g * _GROUP_W + (pr + 1) * LANES)
            x_pair = xbc_ref[:, xs]
            ms = []
            for e in range(2):
                hd = g * _HEADS_PER_GROUP + 2 * pr + e
                seg = cum[:, hd:hd + 1] - src_t[hd:hd + 1, :]
                ms.append((cb * jnp.exp(seg + mask_bias)).astype(BF16))
            x_blk = jnp.concatenate([jnp.where(lane_t < SSD_HEAD_DIM, x_pair, zero_x),
                                     jnp.where(lane_t >= SSD_HEAD_DIM, x_pair, zero_x)], axis=0)
            y_pair = _bdot(jnp.concatenate(ms, axis=1), x_blk)
            y_ref[:, xs] = (y_pair + y_off[:, pr * LANES:(pr + 1) * LANES]).astype(y_ref.dtype)
        xw = (xbc_ref[:, gs].astype(F32) * w_state).astype(BF16)
        b_t = b_g.astype(F32).T.astype(BF16)
        state_ref[g] = st * w_carry + _bdot(b_t, xw)


def _scan_call(geom, xbc, dt, dt_bias, a_log):
    t = SCAN_CPS * SSD_CHUNK
    assert geom.ctx % t == 0 and geom.seq % t == 0
    ncc = geom.ctx // t
    nlc = geom.seq // t
    ctx0 = geom.n_lat // t

    def row_block(b, d, s):
        jc = s + d * (ncc - 1 - 2 * s)
        sl = s - ncc
        jl = sl + d * (nlc - 1 - 2 * sl)
        return jnp.where(s < ncc, ctx0 + b * ncc + jc, b * nlc + jl)

    def pad_lanes(v):
        return jnp.pad(v.astype(F32), ((0, 0), (0, LANES - v.shape[-1]))).reshape(2, 1, LANES)

    expand = (jnp.arange(SSD_INNER)[None, :] // SSD_HEAD_DIM == jnp.arange(LANES)[:, None]).astype(BF16)

    y_shape = jax.ShapeDtypeStruct((geom.n_tok, SSD_INNER), BF16)
    return pl.pallas_call(
        _scan_kernel,
        grid=(geom.batch, ncc + nlc),
        in_specs=[
            pl.BlockSpec((t, SSD_CONV_DIM), lambda b, s: (row_block(b, 0, s), 0)),
            pl.BlockSpec((t, SSD_CONV_DIM), lambda b, s: (row_block(b, 1, s), 0)),
            pl.BlockSpec((None, t, LANES), lambda b, s: (0, row_block(b, 0, s), 0)),
            pl.BlockSpec((None, t, LANES), lambda b, s: (1, row_block(b, 1, s), 0)),
            pl.BlockSpec((2, 1, LANES), lambda b, s: (0, 0, 0)),
            pl.BlockSpec((2, 1, LANES), lambda b, s: (0, 0, 0)),
            pl.BlockSpec((LANES, SSD_INNER), lambda b, s: (0, 0)),
        ],
        out_specs=[pl.BlockSpec((t, SSD_INNER), lambda b, s: (row_block(b, 0, s), 0)),
                   pl.BlockSpec((t, SSD_INNER), lambda b, s: (row_block(b, 1, s), 0))],
        out_shape=[y_shape, y_shape],
        scratch_shapes=[pltpu.VMEM((2, SSD_GROUPS, SSD_STATE, _GROUP_W), F32)],
        compiler_params=_cparams(("parallel", "arbitrary")),
        name="ssd_scan",
    )(xbc, xbc, dt, dt, pad_lanes(dt_bias), pad_lanes(a_log), expand)


def _ssd_out_kernel(yf_ref, yb_ref, xbc_ref, z_ref, dskip_ref, ng_ref, h_ref, mod_ref, w_ref, g_ref, b_ref, o_ref):
    dsk = dskip_ref[0:1, :] + dskip_ref[1:2, :]
    y = None
    for g in range(SSD_GROUPS):
        gs = slice(g * _GROUP_W, (g + 1) * _GROUP_W)
        z = z_ref[:, gs].astype(F32)
        ysum = yf_ref[:, gs].astype(F32) + yb_ref[:, gs].astype(F32) + dsk[:, gs] * xbc_ref[:, gs].astype(F32)
        gy = ysum * (z * _sigmoid(z))
        gy = _rms_rows(gy, ng_ref[:, gs]).astype(BF16)
        part = _bdot(gy, w_ref[gs, :])
        y = part if y is None else y + part
    r = mod_ref[5:6, :] * y
    o_ref[...] = _layer_norm_rows(DEEPNORM_ALPHA * h_ref[...] + r, g_ref[...], b_ref[...])


def _ssd_out_call(geom, y, xbc, z, dskip, norm_g, h, mods, layer, w_out, ln_g, ln_b, n_rows, tm):
    return pl.pallas_call(
        _ssd_out_kernel,
        grid=(n_rows // tm,),
        in_specs=[
            _row_spec(tm, SSD_INNER),
            _row_spec(tm, SSD_INNER),
            _row_spec(tm, SSD_INNER),
            _row_spec(tm, SSD_INNER),
            _full_spec(dskip.shape),
            _full_spec((1, SSD_INNER)),
            _row_spec(tm, D_MODEL),
            geom.mod_spec(layer, tm),
            _full_spec(w_out.shape),
            _full_spec((1, D_MODEL)),
            _full_spec((1, D_MODEL)),
        ],
        out_specs=_row_spec(tm, D_MODEL),
        out_shape=jax.ShapeDtypeStruct((n_rows, D_MODEL), F32),
        compiler_params=_cparams(("parallel",)),
        name="ssd_out",
    )(y[0], y[1], xbc, z, dskip, norm_g.reshape(1, -1), h, mods, w_out, ln_g.reshape(1, -1), ln_b.reshape(1, -1))


def _attn_weights(w_in, w_uq, w_ukv):
    d = w_in.shape[0]
    kr_end = MLA_Q_RANK + MLA_KV_RANK + MLA_ROPE
    dk_lo = kr_end + DIFF_WIDTH
    dv_lo = dk_lo + DIFF_WIDTH
    w_row = jnp.concatenate([w_in[:, :kr_end], jnp.zeros((d, HEAD_BLOCK - MLA_ROPE), w_in.dtype),
                             w_in[:, dk_lo:dv_lo]], axis=1).astype(BF16)
    w_col = jnp.concatenate([w_in[:, kr_end:dk_lo], w_in[:, dv_lo:]], axis=1).T.astype(BF16)
    pad_q = HEAD_BLOCK - MLA_QK_DIM
    w_uq_p = jnp.pad(w_uq.reshape(MLA_Q_RANK, MLA_HEADS, MLA_QK_DIM), ((0, 0), (0, 0), (0, pad_q)))
    w_uqt = w_uq_p.reshape(MLA_Q_RANK, _QA_W).T.astype(BF16)
    kv = w_ukv.reshape(MLA_KV_RANK, MLA_HEADS, MLA_NOPE + MLA_V)
    w_uk = jnp.pad(kv[:, :, :MLA_NOPE], ((0, 0), (0, 0), (0, HEAD_BLOCK - MLA_NOPE)))
    w_uk = w_uk.reshape(MLA_KV_RANK, _QA_W).astype(BF16)
    w_uvt = kv[:, :, MLA_NOPE:].reshape(MLA_KV_RANK, _VA_W).T.astype(BF16)
    src = jnp.arange(HEAD_BLOCK)[:, None]
    dst = jnp.arange(_QA_W)[None, :]
    place = ((dst % HEAD_BLOCK == src + MLA_NOPE) & (src < MLA_ROPE)).astype(BF16)
    return w_row, w_col, w_uqt, w_uk, w_uvt, place


def _ssd_weights(w_in):
    d = w_in.shape[0]
    z = jnp.zeros((d, LANES - SSD_HEADS), w_in.dtype)
    return jnp.concatenate([w_in[:, :_S_DT], w_in[:, _S_DT:_S_DT + SSD_HEADS], z, w_in[:, _S_DT + SSD_HEADS:], z],
                           axis=1).astype(BF16)


def _lambda_init_for(layer):
    return 0.8 - 0.6 * math.exp(-0.3 * layer)


def _pick_tile(seq, ctx, n_ctx, want):
    tm = want
    while seq % tm or n_ctx % tm:
        tm //= 2
    return tm


def kernel(x, c, ctx, c_ctx, ada_w, ada_b, ln_g, ln_b, ffn1_w_gu, ffn1_w_down, ffn2_w_gu, ffn2_w_down, attn_w_in,
           mla_q_norm_g, mla_w_uq, mla_kv_norm_g, mla_w_ukv, diff_lam_q1, diff_lam_k1, diff_lam_q2, diff_lam_k2,
           diff_subln_g, attn_w_out, ssd_w_in, ssd_conv_w, ssd_conv_b, ssd_a_log, ssd_dt_bias, ssd_d, ssd_norm_g,
           ssd_w_out):
    batch, seq, d = x.shape
    n_ctx_tok = ctx.shape[1]
    geom = _Geom(batch, seq, n_ctx_tok)
    assert d == D_MODEL and batch + 1 <= MOD_ROWS
    assert seq % GRID_W == 0 and seq % SSD_CHUNK == 0 and n_ctx_tok % SSD_CHUNK == 0
    tq = math.gcd(seq, ATT_TQ)
    assert seq % tq == 0 and seq % ATT_KC == 0 and n_ctx_tok % ATT_KC == 0
    tm_ffn = _pick_tile(seq, n_ctx_tok, geom.n_ctx, ROW_TILE_WIDE)
    tm_proj = _pick_tile(seq, n_ctx_tok, geom.n_ctx, ROW_TILE_PROJ)
    assert n_ctx_tok % tm_proj == 0
    fc = FFN_COL_CHUNK

    cond = jnp.concatenate([c, c_ctx[None, :], jnp.zeros((MOD_ROWS - batch - 1, d), F32)], axis=0)
    mods = _ada_call(cond, ada_w, ada_b).reshape(DEPTH, MOD_ROWS, N_MOD, d)
    rtab, ttab = _rope_tables(seq, tm_proj)

    h = (x.reshape(batch * seq, d), ctx.reshape(batch * n_ctx_tok, d))
    for l in range(DEPTH):
        last = l == DEPTH - 1
        h = _ffn_call(geom, h, mods, l, 0, ffn1_w_gu[l].astype(BF16), ffn1_w_down[l].astype(BF16),
                      ln_g[l, 0], ln_b[l, 0], geom.n_tok, tm_ffn, fc)
        n_out = geom.n_lat if last else geom.n_tok
        if l % 2 == 0:
            a = l // 2
            wts = _attn_weights(attn_w_in[a], mla_w_uq[a], mla_w_ukv[a])
            qa, ka, va, dq, dk, dv = _attn_in_call(geom, h, mods, l, rtab, ttab, wts, mla_q_norm_g[a].reshape(1, -1),
                                                   mla_kv_norm_g[a].reshape(1, -1), tm_proj)
            lam_vecs = jnp.stack([diff_lam_q1[a], diff_lam_k1[a], diff_lam_q2[a], diff_lam_k2[a]])
            lat, ctx_heads = _attention(geom, qa, ka, va, dq, dk, dv, lam_vecs, diff_subln_g[a].reshape(1, -1),
                                        _lambda_init_for(l), not last, tq)
            h = _attn_out_call(geom, lat, ctx_heads, h, mods, l, attn_w_out[a].astype(BF16), ln_g[l, 1],
                               ln_b[l, 1], tm_ffn)
        else:
            s = l // 2
            z, xbc, dt = _ssd_in_call(geom, h, mods, l, _ssd_weights(ssd_w_in[s]), ssd_conv_w[s], ssd_conv_b[s],
                                      tm_proj)
            y = _scan_call(geom, xbc, dt, ssd_dt_bias[s], ssd_a_log[s])
            dskip = jnp.repeat(ssd_d[s], SSD_HEAD_DIM, axis=-1)
            h = _ssd_out_call(geom, y, xbc, z, dskip, ssd_norm_g[s], h, mods, l, ssd_w_out[s].astype(BF16),
                              ln_g[l, 1], ln_b[l, 1], n_out, tm_ffn)
        h = _ffn_call(geom, h, mods, l, 6, ffn2_w_gu[l].astype(BF16), ffn2_w_down[l].astype(BF16),
                      ln_g[l, 2], ln_b[l, 2], n_out, tm_ffn, fc)
    return h[:geom.n_lat].reshape(batch, seq, d)
```
